```python
import math
import jax, jax.numpy as jnp
from jax import lax
import numpy as np

D_MODEL = 2048
BATCH = 8
SEQ = 8192
DEPTH = 4

N_META = 16
W_A = D_MODEL
H_A = 16
HD_A = W_A // H_A
CONV_A = 4
LRU_C = 8.0
W_B = D_MODEL
CONV_B = 31
EPS = 1e-6
SPLIT_SIZES = (W_A, W_A, W_B, W_B, W_B, D_MODEL, D_MODEL)
N_IN = sum(SPLIT_SIZES)
SPLIT_IDX = tuple(int(v) for v in np.cumsum(SPLIT_SIZES)[:-1])

kernel_name = "hybrid_rglru_conformer_conv_block"


def rms_norm(x, g):
    xf = x.astype(jnp.float32)
    y = xf * lax.rsqrt(jnp.mean(xf * xf, axis=-1, keepdims=True) + EPS)
    return (y * g.astype(jnp.float32)).astype(x.dtype)


def layer_norm(x, g, b):
    xf = x.astype(jnp.float32)
    mu = jnp.mean(xf, axis=-1, keepdims=True)
    xc = xf - mu
    var = jnp.mean(xc * xc, axis=-1, keepdims=True)
    y = xc * lax.rsqrt(var + EPS) * g.astype(jnp.float32) + b.astype(jnp.float32)
    return y.astype(x.dtype)


def causal_depthwise_conv(x, w, b):
    k, c = w.shape
    out = lax.conv_general_dilated(
        x, w[:, None, :].astype(x.dtype), window_strides=(1,), padding=[(k - 1, 0)],
        dimension_numbers=("NWC", "WIO", "NWC"), feature_group_count=c)
    return out + b.astype(x.dtype)


def rg_lru(x, w_r, b_r, w_i, b_i, lam):
    bsz, t, c = x.shape
    xh = x.reshape(bsz, t, H_A, HD_A)
    r = jax.nn.sigmoid(jnp.einsum("bthi,hij->bthj", xh, w_r).reshape(bsz, t, c) + b_r)
    i = jax.nn.sigmoid(jnp.einsum("bthi,hij->bthj", xh, w_i).reshape(bsz, t, c) + b_i)
    log_a = -LRU_C * r.astype(jnp.float32) * jax.nn.softplus(-lam.astype(jnp.float32))
    a = jnp.exp(log_a)
    mult = jnp.sqrt(jnp.maximum(-jnp.expm1(2.0 * log_a), 0.0))
    u = mult * (i * x).astype(jnp.float32)

    def combine(left, right):
        a1, b1 = left
        a2, b2 = right
        return a1 * a2, a2 * b1 + b2

    _, h = lax.associative_scan(combine, (a, u), axis=1)
    return h.astype(x.dtype)


def _fwd_setup_inputs(seed: int = 0) -> dict:
    key = jax.random.key(seed)
    ks = jax.random.split(key, 24)
    f32 = jnp.float32
    nrm = lambda k, shape, s: (jax.random.normal(k, shape, f32) * s).astype(f32)
    x = jax.random.normal(ks[0], (BATCH, SEQ, D_MODEL), f32)
    meta = nrm(ks[1], (N_META, D_MODEL), 1.0)
    norm_g = 1.0 + nrm(ks[2], (DEPTH, D_MODEL), 0.05)
    w_in = nrm(ks[3], (DEPTH, D_MODEL, N_IN), D_MODEL ** -0.5)
    conv_a_w = nrm(ks[4], (DEPTH, CONV_A, W_A), CONV_A ** -0.5)
    conv_a_b = nrm(ks[5], (DEPTH, W_A), 0.02)
    w_rg = nrm(ks[6], (DEPTH, H_A, HD_A, HD_A), HD_A ** -0.5)
    b_rg = nrm(ks[7], (DEPTH, W_A), 0.02)
    w_ig = nrm(ks[8], (DEPTH, H_A, HD_A, HD_A), HD_A ** -0.5)
    b_ig = nrm(ks[9], (DEPTH, W_A), 0.02)
    a_c = jax.random.uniform(ks[10], (DEPTH, W_A), f32, 0.9, 0.999)
    s = a_c ** (1.0 / LRU_C)
    lru_lambda = jnp.log(s) - jnp.log1p(-s)
    conv_b_w = nrm(ks[11], (DEPTH, CONV_B, W_B), CONV_B ** -0.5)
    conv_b_b = nrm(ks[12], (DEPTH, W_B), 0.02)
    ln_b_g = 1.0 + nrm(ks[13], (DEPTH, W_B), 0.05)
    ln_b_b = nrm(ks[14], (DEPTH, W_B), 0.02)
    w_proj_a = nrm(ks[15], (DEPTH, W_A, D_MODEL), W_A ** -0.5)
    w_proj_b = nrm(ks[16], (DEPTH, W_B, D_MODEL), W_B ** -0.5)
    w_out = nrm(ks[17], (DEPTH, D_MODEL, D_MODEL), D_MODEL ** -0.5)
    final_g = 1.0 + nrm(ks[18], (D_MODEL,), 0.05)
    return {"x": x, "meta": meta, "norm_g": norm_g, "w_in": w_in,
            "conv_a_w": conv_a_w, "conv_a_b": conv_a_b, "w_rg": w_rg, "b_rg": b_rg,
            "w_ig": w_ig, "b_ig": b_ig, "lru_lambda": lru_lambda,
            "conv_b_w": conv_b_w, "conv_b_b": conv_b_b, "ln_b_g": ln_b_g, "ln_b_b": ln_b_b,
            "w_proj_a": w_proj_a, "w_proj_b": w_proj_b, "w_out": w_out, "final_g": final_g}


def _fwd_reference(x, meta, norm_g, w_in, conv_a_w, conv_a_b, w_rg, b_rg, w_ig, b_ig, lru_lambda,
              conv_b_w, conv_b_b, ln_b_g, ln_b_b, w_proj_a, w_proj_b, w_out, final_g):
    bsz = x.shape[0]
    meta_b = jnp.broadcast_to(meta.astype(x.dtype)[None], (bsz, N_META, x.shape[-1]))
    h = jnp.concatenate([meta_b, x], axis=1)
    for l in range(DEPTH):
        hn = rms_norm(h, norm_g[l])
        z = jnp.einsum("btd,dn->btn", hn, w_in[l])
        xa, sa, vb, gb, sb, ma, mb = jnp.split(z, SPLIT_IDX, axis=-1)
        ya = causal_depthwise_conv(xa, conv_a_w[l], conv_a_b[l])
        ya = rg_lru(ya, w_rg[l], b_rg[l], w_ig[l], b_ig[l], lru_lambda[l])
        ya = ya * jax.nn.silu(sa)
        pa = jnp.einsum("btc,cd->btd", ya, w_proj_a[l])
        yb = vb * jax.nn.sigmoid(gb)
        yb = causal_depthwise_conv(yb, conv_b_w[l], conv_b_b[l])
        yb = jax.nn.silu(layer_norm(yb, ln_b_g[l], ln_b_b[l])) * jax.nn.silu(sb)
        pb = jnp.einsum("btc,cd->btd", yb, w_proj_b[l])
        merged = jax.nn.sigmoid(ma) * pa + jax.nn.sigmoid(mb) * pb
        h = h + jnp.einsum("btd,de->bte", merged, w_out[l])
    out = rms_norm(h, final_g)
    return out[:, N_META:, :]


import jax as _jax
import jax.numpy as _jnp

TWIN_FORMAT = 'train_step'
FWD_PARAMS = ['x', 'meta', 'norm_g', 'w_in', 'conv_a_w', 'conv_a_b', 'w_rg', 'b_rg', 'w_ig', 'b_ig', 'lru_lambda', 'conv_b_w', 'conv_b_b', 'ln_b_g', 'ln_b_b', 'w_proj_a', 'w_proj_b', 'w_out', 'final_g']
TWIN_WEIGHTS = ['meta', 'norm_g', 'w_in', 'conv_a_w', 'conv_a_b', 'w_rg', 'b_rg', 'w_ig', 'b_ig', 'lru_lambda', 'conv_b_w', 'conv_b_b', 'ln_b_g', 'ln_b_b', 'w_proj_a', 'w_proj_b', 'w_out', 'final_g']
TWIN_DIFF_INPUT = 'x'
TWIN_INPUTS = ['x', 'meta', 'norm_g', 'w_in', 'conv_a_w', 'conv_a_b', 'w_rg', 'b_rg', 'w_ig', 'b_ig', 'lru_lambda', 'conv_b_w', 'conv_b_b', 'ln_b_g', 'ln_b_b', 'w_proj_a', 'w_proj_b', 'w_out', 'final_g', 'loss_target', 'm_meta', 'm_norm_g', 'm_w_in', 'm_conv_a_w', 'm_conv_a_b', 'm_w_rg', 'm_b_rg', 'm_w_ig', 'm_b_ig', 'm_lru_lambda', 'm_conv_b_w', 'm_conv_b_b', 'm_ln_b_g', 'm_ln_b_b', 'm_w_proj_a', 'm_w_proj_b', 'm_w_out', 'm_final_g', 'v_meta', 'v_norm_g', 'v_w_in', 'v_conv_a_w', 'v_conv_a_b', 'v_w_rg', 'v_b_rg', 'v_w_ig', 'v_b_ig', 'v_lru_lambda', 'v_conv_b_w', 'v_conv_b_b', 'v_ln_b_g', 'v_ln_b_b', 'v_w_proj_a', 'v_w_proj_b', 'v_w_out', 'v_final_g']
TWIN_OUTPUTS = ['loss', 'grad_x', 'grad_meta', 'grad_norm_g', 'grad_w_in', 'grad_conv_a_w', 'grad_conv_a_b', 'grad_w_rg', 'grad_b_rg', 'grad_w_ig', 'grad_b_ig', 'grad_lru_lambda', 'grad_conv_b_w', 'grad_conv_b_b', 'grad_ln_b_g', 'grad_ln_b_b', 'grad_w_proj_a', 'grad_w_proj_b', 'grad_w_out', 'grad_final_g', 'delta_meta', 'delta_norm_g', 'delta_w_in', 'delta_conv_a_w', 'delta_conv_a_b', 'delta_w_rg', 'delta_b_rg', 'delta_w_ig', 'delta_b_ig', 'delta_lru_lambda', 'delta_conv_b_w', 'delta_conv_b_b', 'delta_ln_b_g', 'delta_ln_b_b', 'delta_w_proj_a', 'delta_w_proj_b', 'delta_w_out', 'delta_final_g', 'new_m_meta', 'new_m_norm_g', 'new_m_w_in', 'new_m_conv_a_w', 'new_m_conv_a_b', 'new_m_w_rg', 'new_m_b_rg', 'new_m_w_ig', 'new_m_b_ig', 'new_m_lru_lambda', 'new_m_conv_b_w', 'new_m_conv_b_b', 'new_m_ln_b_g', 'new_m_ln_b_b', 'new_m_w_proj_a', 'new_m_w_proj_b', 'new_m_w_out', 'new_m_final_g', 'new_v_meta', 'new_v_norm_g', 'new_v_w_in', 'new_v_conv_a_w', 'new_v_conv_a_b', 'new_v_w_rg', 'new_v_b_rg', 'new_v_w_ig', 'new_v_b_ig', 'new_v_lru_lambda', 'new_v_conv_b_w', 'new_v_conv_b_b', 'new_v_ln_b_g', 'new_v_ln_b_b', 'new_v_w_proj_a', 'new_v_w_proj_b', 'new_v_w_out', 'new_v_final_g']
TWIN_LEAF_KINDS = {'loss': 'loss', 'grad_x': 'grad_x', 'grad_meta': 'grad_w', 'grad_norm_g': 'grad_w', 'grad_w_in': 'grad_w', 'grad_conv_a_w': 'grad_w', 'grad_conv_a_b': 'grad_w', 'grad_w_rg': 'grad_w', 'grad_b_rg': 'grad_w', 'grad_w_ig': 'grad_w', 'grad_b_ig': 'grad_w', 'grad_lru_lambda': 'grad_w', 'grad_conv_b_w': 'grad_w', 'grad_conv_b_b': 'grad_w', 'grad_ln_b_g': 'grad_w', 'grad_ln_b_b': 'grad_w', 'grad_w_proj_a': 'grad_w', 'grad_w_proj_b': 'grad_w', 'grad_w_out': 'grad_w', 'grad_final_g': 'grad_w', 'delta_meta': 'delta_w', 'delta_norm_g': 'delta_w', 'delta_w_in': 'delta_w', 'delta_conv_a_w': 'delta_w', 'delta_conv_a_b': 'delta_w', 'delta_w_rg': 'delta_w', 'delta_b_rg': 'delta_w', 'delta_w_ig': 'delta_w', 'delta_b_ig': 'delta_w', 'delta_lru_lambda': 'delta_w', 'delta_conv_b_w': 'delta_w', 'delta_conv_b_b': 'delta_w', 'delta_ln_b_g': 'delta_w', 'delta_ln_b_b': 'delta_w', 'delta_w_proj_a': 'delta_w', 'delta_w_proj_b': 'delta_w', 'delta_w_out': 'delta_w', 'delta_final_g': 'delta_w', 'new_m_meta': 'new_m', 'new_m_norm_g': 'new_m', 'new_m_w_in': 'new_m', 'new_m_conv_a_w': 'new_m', 'new_m_conv_a_b': 'new_m', 'new_m_w_rg': 'new_m', 'new_m_b_rg': 'new_m', 'new_m_w_ig': 'new_m', 'new_m_b_ig': 'new_m', 'new_m_lru_lambda': 'new_m', 'new_m_conv_b_w': 'new_m', 'new_m_conv_b_b': 'new_m', 'new_m_ln_b_g': 'new_m', 'new_m_ln_b_b': 'new_m', 'new_m_w_proj_a': 'new_m', 'new_m_w_proj_b': 'new_m', 'new_m_w_out': 'new_m', 'new_m_final_g': 'new_m', 'new_v_meta': 'new_v', 'new_v_norm_g': 'new_v', 'new_v_w_in': 'new_v', 'new_v_conv_a_w': 'new_v', 'new_v_conv_a_b': 'new_v', 'new_v_w_rg': 'new_v', 'new_v_b_rg': 'new_v', 'new_v_w_ig': 'new_v', 'new_v_b_ig': 'new_v', 'new_v_lru_lambda': 'new_v', 'new_v_conv_b_w': 'new_v', 'new_v_conv_b_b': 'new_v', 'new_v_ln_b_g': 'new_v', 'new_v_ln_b_b': 'new_v', 'new_v_w_proj_a': 'new_v', 'new_v_w_proj_b': 'new_v', 'new_v_w_out': 'new_v', 'new_v_final_g': 'new_v'}


def _forward(args):
    return _fwd_reference(*[args[k] for k in FWD_PARAMS])


def _output_shape():
    def fwd():
        inp = _fwd_setup_inputs(0)
        return _fwd_reference(*[inp[k] for k in FWD_PARAMS])
    out = _jax.eval_shape(fwd)
    return out.shape, out.dtype

N_MICROBATCH = 1
ADAM_LR = 0.001
ADAM_B1 = 0.9
ADAM_B2 = 0.999
ADAM_EPS = 1e-08
ADAM_WD = 0.01
ADAM_STEP = 10
PER_EXAMPLE_BATCH_AXIS = {'x': 0, 'loss_target': 0}
SHARED_INPUTS = []
_WEIGHT_DTYPES = {'meta': _jnp.float32, 'norm_g': _jnp.float32, 'w_in': _jnp.float32, 'conv_a_w': _jnp.float32, 'conv_a_b': _jnp.float32, 'w_rg': _jnp.float32, 'b_rg': _jnp.float32, 'w_ig': _jnp.float32, 'b_ig': _jnp.float32, 'lru_lambda': _jnp.float32, 'conv_b_w': _jnp.float32, 'conv_b_b': _jnp.float32, 'ln_b_g': _jnp.float32, 'ln_b_b': _jnp.float32, 'w_proj_a': _jnp.float32, 'w_proj_b': _jnp.float32, 'w_out': _jnp.float32, 'final_g': _jnp.float32}
MOMENT_SCALE = {'meta': 4.313341e-03, 'norm_g': 6.082878e-02, 'w_in': 2.290897e-02, 'conv_a_w': 3.802239e-02, 'conv_a_b': 4.545436e-01, 'w_rg': 1.188504e-02, 'b_rg': 1.019890e-02, 'w_ig': 2.178444e-02, 'b_ig': 1.419493e-02, 'lru_lambda': 1.973089e-02, 'conv_b_w': 2.478909e-02, 'conv_b_b': 5.464916e-02, 'ln_b_g': 3.047808e-02, 'ln_b_b': 3.038347e-02, 'w_proj_a': 3.820880e-02, 'w_proj_b': 2.497240e-02, 'w_out': 4.455036e-02, 'final_g': 3.202987e+01}


def _to_microbatches(a, axis):
    t = _jnp.moveaxis(a, axis, 0)
    t = t.reshape((N_MICROBATCH, t.shape[0] // N_MICROBATCH) + t.shape[1:])
    return _jnp.moveaxis(t, 1, axis + 1)


def setup_inputs(seed: int = 0) -> dict:
    inp = _fwd_setup_inputs(seed)
    key = _jax.random.fold_in(_jax.random.key(seed), 7919)
    shape, _ = _output_shape()
    out = dict(inp)
    out["loss_target"] = _jax.random.normal(_jax.random.fold_in(key, 0), shape, _jnp.float32)
    for i, name in enumerate(TWIN_WEIGHTS):
        w = inp[name].astype(_jnp.float32)
        if MOMENT_SCALE is None:
            s = _jnp.sqrt(_jnp.mean(_jnp.square(w)) + 1e-30)
        else:
            s = MOMENT_SCALE[name]
        km, kv = _jax.random.split(_jax.random.fold_in(key, i + 1))
        out[name] = w
        out["m_" + name] = s * _jax.random.normal(km, w.shape, _jnp.float32)
        out["v_" + name] = (s * s) * _jax.random.uniform(kv, w.shape, _jnp.float32, 0.5, 1.5)
    if N_MICROBATCH > 1:
        for name, axis in PER_EXAMPLE_BATCH_AXIS.items():
            out[name] = _to_microbatches(out[name], axis)
    return {'x': out['x'], 'meta': out['meta'], 'norm_g': out['norm_g'], 'w_in': out['w_in'], 'conv_a_w': out['conv_a_w'], 'conv_a_b': out['conv_a_b'], 'w_rg': out['w_rg'], 'b_rg': out['b_rg'], 'w_ig': out['w_ig'], 'b_ig': out['b_ig'], 'lru_lambda': out['lru_lambda'], 'conv_b_w': out['conv_b_w'], 'conv_b_b': out['conv_b_b'], 'ln_b_g': out['ln_b_g'], 'ln_b_b': out['ln_b_b'], 'w_proj_a': out['w_proj_a'], 'w_proj_b': out['w_proj_b'], 'w_out': out['w_out'], 'final_g': out['final_g'], 'loss_target': out['loss_target'], 'm_meta': out['m_meta'], 'm_norm_g': out['m_norm_g'], 'm_w_in': out['m_w_in'], 'm_conv_a_w': out['m_conv_a_w'], 'm_conv_a_b': out['m_conv_a_b'], 'm_w_rg': out['m_w_rg'], 'm_b_rg': out['m_b_rg'], 'm_w_ig': out['m_w_ig'], 'm_b_ig': out['m_b_ig'], 'm_lru_lambda': out['m_lru_lambda'], 'm_conv_b_w': out['m_conv_b_w'], 'm_conv_b_b': out['m_conv_b_b'], 'm_ln_b_g': out['m_ln_b_g'], 'm_ln_b_b': out['m_ln_b_b'], 'm_w_proj_a': out['m_w_proj_a'], 'm_w_proj_b': out['m_w_proj_b'], 'm_w_out': out['m_w_out'], 'm_final_g': out['m_final_g'], 'v_meta': out['v_meta'], 'v_norm_g': out['v_norm_g'], 'v_w_in': out['v_w_in'], 'v_conv_a_w': out['v_conv_a_w'], 'v_conv_a_b': out['v_conv_a_b'], 'v_w_rg': out['v_w_rg'], 'v_b_rg': out['v_b_rg'], 'v_w_ig': out['v_w_ig'], 'v_b_ig': out['v_b_ig'], 'v_lru_lambda': out['v_lru_lambda'], 'v_conv_b_w': out['v_conv_b_w'], 'v_conv_b_b': out['v_conv_b_b'], 'v_ln_b_g': out['v_ln_b_g'], 'v_ln_b_b': out['v_ln_b_b'], 'v_w_proj_a': out['v_w_proj_a'], 'v_w_proj_b': out['v_w_proj_b'], 'v_w_out': out['v_w_out'], 'v_final_g': out['v_final_g']}


def _loss(weights, diff, rest, loss_target):
    with _jax.named_scope("forward"):
        args = {**rest, TWIN_DIFF_INPUT: diff, **{k: w.astype(_WEIGHT_DTYPES[k]) for k, w in weights.items()}}
        y = _forward(args)
    with _jax.named_scope("loss_head"):
        err = _jnp.square(y.astype(_jnp.float32) - loss_target)
        return 0.5 * _jnp.sum(_jnp.mean(err, axis=-1)) if err.ndim else 0.5 * err


def _adamw(w, g, m, v):
    m = ADAM_B1 * m + (1.0 - ADAM_B1) * g
    v = ADAM_B2 * v + (1.0 - ADAM_B2) * _jnp.square(g)
    m_hat = m / (1.0 - ADAM_B1 ** ADAM_STEP)
    v_hat = v / (1.0 - ADAM_B2 ** ADAM_STEP)
    delta = -ADAM_LR * (m_hat / (_jnp.sqrt(v_hat) + ADAM_EPS) + ADAM_WD * w)
    return delta, m, v


def reference(x, meta, norm_g, w_in, conv_a_w, conv_a_b, w_rg, b_rg, w_ig, b_ig, lru_lambda, conv_b_w, conv_b_b, ln_b_g, ln_b_b, w_proj_a, w_proj_b, w_out, final_g, loss_target, m_meta, m_norm_g, m_w_in, m_conv_a_w, m_conv_a_b, m_w_rg, m_b_rg, m_w_ig, m_b_ig, m_lru_lambda, m_conv_b_w, m_conv_b_b, m_ln_b_g, m_ln_b_b, m_w_proj_a, m_w_proj_b, m_w_out, m_final_g, v_meta, v_norm_g, v_w_in, v_conv_a_w, v_conv_a_b, v_w_rg, v_b_rg, v_w_ig, v_b_ig, v_lru_lambda, v_conv_b_w, v_conv_b_b, v_ln_b_g, v_ln_b_b, v_w_proj_a, v_w_proj_b, v_w_out, v_final_g):
    given = dict(x=x, meta=meta, norm_g=norm_g, w_in=w_in, conv_a_w=conv_a_w, conv_a_b=conv_a_b, w_rg=w_rg, b_rg=b_rg, w_ig=w_ig, b_ig=b_ig, lru_lambda=lru_lambda, conv_b_w=conv_b_w, conv_b_b=conv_b_b, ln_b_g=ln_b_g, ln_b_b=ln_b_b, w_proj_a=w_proj_a, w_proj_b=w_proj_b, w_out=w_out, final_g=final_g, loss_target=loss_target, m_meta=m_meta, m_norm_g=m_norm_g, m_w_in=m_w_in, m_conv_a_w=m_conv_a_w, m_conv_a_b=m_conv_a_b, m_w_rg=m_w_rg, m_b_rg=m_b_rg, m_w_ig=m_w_ig, m_b_ig=m_b_ig, m_lru_lambda=m_lru_lambda, m_conv_b_w=m_conv_b_w, m_conv_b_b=m_conv_b_b, m_ln_b_g=m_ln_b_g, m_ln_b_b=m_ln_b_b, m_w_proj_a=m_w_proj_a, m_w_proj_b=m_w_proj_b, m_w_out=m_w_out, m_final_g=m_final_g, v_meta=v_meta, v_norm_g=v_norm_g, v_w_in=v_w_in, v_conv_a_w=v_conv_a_w, v_conv_a_b=v_conv_a_b, v_w_rg=v_w_rg, v_b_rg=v_b_rg, v_w_ig=v_w_ig, v_b_ig=v_b_ig, v_lru_lambda=v_lru_lambda, v_conv_b_w=v_conv_b_w, v_conv_b_b=v_conv_b_b, v_ln_b_g=v_ln_b_g, v_ln_b_b=v_ln_b_b, v_w_proj_a=v_w_proj_a, v_w_proj_b=v_w_proj_b, v_w_out=v_w_out, v_final_g=v_final_g)
    weights = {n: given[n] for n in TWIN_WEIGHTS}
    shared = {n: given[n] for n in SHARED_INPUTS}
    per_example = {n: given[n] for n in ['x']}
    grad_fn = _jax.value_and_grad(_loss, argnums=(0, 1))

    def one_microbatch(ex, loss_target):
        ex = dict(ex)
        diff = ex.pop(TWIN_DIFF_INPUT)
        return grad_fn(weights, diff, {**shared, **ex}, loss_target)

    if N_MICROBATCH == 1:
        loss, (grad_w, grad_x) = one_microbatch(per_example, given["loss_target"])
    else:
        def body(carry, xs):
            loss_sum, grad_sum = carry
            l_k, (gw_k, gx_k) = one_microbatch(xs[0], xs[1])
            with _jax.named_scope("update"):
                return (loss_sum + l_k, _jax.tree.map(_jnp.add, grad_sum, gw_k)), gx_k

        init = (_jnp.zeros((), _jnp.float32), _jax.tree.map(_jnp.zeros_like, weights))
        (loss, grad_w), grad_x = _jax.lax.scan(body, init, (per_example, given["loss_target"]))
    with _jax.named_scope("update"):
        delta_w, new_m, new_v = {}, {}, {}
        for n in TWIN_WEIGHTS:
            delta_w[n], new_m[n], new_v[n] = _adamw(weights[n], grad_w[n], given["m_" + n], given["v_" + n])
    return (loss, grad_x, *[grad_w[n] for n in TWIN_WEIGHTS], *[delta_w[n] for n in TWIN_WEIGHTS],
            *[new_m[n] for n in TWIN_WEIGHTS], *[new_v[n] for n in TWIN_WEIGHTS])
```

```python
import functools

import jax
import jax.numpy as jnp
from jax import lax
from jax.experimental import pallas as pl
from jax.experimental.pallas import tpu as pltpu

F32 = jnp.float32
BF16 = jnp.bfloat16
MESH = pl.DeviceIdType.MESH

EPS = 1e-6
N_META = 16
HEAD_DIM = 128
CONV_A = 4
CONV_B = 31
LRU_C = 8.0
N_SPLIT = 7
N_CHIPS = 4
N_DEV = 8

ADAM_LR = 0.001
ADAM_B1 = 0.9
ADAM_B2 = 0.999
ADAM_EPS = 1e-08
ADAM_WD = 0.01
ADAM_STEP = 10

HALO_A = 16
HALO_B = 48
ROW_TILE_MM = 912
ROW_TILE_MM_BWD = 432
ROW_TILE_EW = 144
COL_TILE = 512
VMEM_LIMIT = 56 * 1024 * 1024
BLOCK_BYTES = 2 * 1024 * 1024

NT_DIMS = (((1,), (1,)), ((), ()))
TN_DIMS = (((0,), (0,)), ((), ()))


def _pick(n, cap, mult):
    best = None
    for d in range(mult, min(n, cap) + 1, mult):
        if n % d == 0:
            best = d
    if best is None:
        raise ValueError(f"no tile for {n} (cap {cap}, multiple of {mult})")
    return best


def _params(*semantics):
    return pltpu.CompilerParams(dimension_semantics=semantics, vmem_limit_bytes=VMEM_LIMIT)


def _whole(shape):
    return pl.BlockSpec(shape, lambda *_: (0,) * len(shape))


def _sigmoid(v):
    return jax.nn.sigmoid(v)


def _dsilu(v, s):
    return s * (1.0 + v * (1.0 - s))


def _log1p(e):
    w = 1.0 + e
    return jnp.where(w == 1.0, e, e * jnp.log(w) / (w - 1.0))


def _softplus(v):
    return jnp.maximum(v, 0.0) + _log1p(jnp.exp(-jnp.abs(v)))


def _neg_expm1(v):
    series = -v * (1.0 + v * 0.5 * (1.0 + v * (1.0 / 3.0) * (1.0 + v * 0.25 * (1.0 + v * 0.2))))
    return jnp.where(v > -0.05, series, 1.0 - jnp.exp(v))


def _row_iota(shape):
    return lax.broadcasted_iota(jnp.int32, shape, 0)


def _in_proj_fwd(h, g, w):
    t, d = h.shape
    nq = w.shape[2]
    tm = _pick(t, ROW_TILE_MM, 16)
    tn = _pick(nq, COL_TILE, 128)
    bps = nq // tn

    def body(h_ref, g_ref, w_ref, z_ref, hn_ref):
        @pl.when(pl.program_id(1) == 0)
        def _():
            v = h_ref[...]
            rinv = lax.rsqrt(jnp.mean(v * v, axis=-1, keepdims=True) + EPS)
            hn_ref[...] = (v * rinv * g_ref[...]).astype(BF16)
        z_ref[...] = jnp.dot(hn_ref[...], w_ref[...], preferred_element_type=F32)

    return pl.pallas_call(
        body, grid=(t // tm, N_CHIPS * bps),
        in_specs=[pl.BlockSpec((tm, d), lambda i, j: (i, 0)), _whole((1, d)),
                  pl.BlockSpec((None, d, tn), lambda i, j: (j // bps, 0, j % bps))],
        out_specs=[pl.BlockSpec((tm, tn), lambda i, j: (i, j)), pl.BlockSpec((tm, d), lambda i, j: (i, 0))],
        out_shape=[jax.ShapeDtypeStruct((t, N_CHIPS * nq), F32), jax.ShapeDtypeStruct((t, d), BF16)],
        compiler_params=_params("parallel", "arbitrary"), name="in_proj_fwd")(h, g, w)


def _conv_a(xext, wa_ref, ba, tt):
    acc = ba
    for k in range(CONV_A):
        acc = acc + wa_ref[k:k + 1, :] * xext[pl.ds(HALO_A - (CONV_A - 1) + k, tt), :]
    return acc


def _gates_a(ca, wr_ref, wi_ref, br, bi, gr_s, gi_s):
    cab = ca.astype(BF16)
    for hh in range(ca.shape[1] // HEAD_DIM):
        sl = slice(hh * HEAD_DIM, (hh + 1) * HEAD_DIM)
        gr_s[:, sl] = jnp.dot(cab[:, sl], wr_ref[hh], preferred_element_type=F32)
        gi_s[:, sl] = jnp.dot(cab[:, sl], wi_ref[hh], preferred_element_type=F32)
    return _sigmoid(gr_s[...] + br), _sigmoid(gi_s[...] + bi)


def _decay_a(r, sp):
    log_a = -LRU_C * r * sp
    a = jnp.exp(log_a)
    mult = jnp.sqrt(jnp.maximum(_neg_expm1(2.0 * log_a), 0.0))
    return a, mult


def _scan_fwd(a_ref, u_ref, out_ref, h0, nblk):
    def blk(b, hprev):
        rows = pl.ds(pl.multiple_of(b * 8, 8), 8)
        ca, cb = a_ref[rows, :], u_ref[rows, :]
        row = _row_iota(ca.shape)
        for dist in (1, 2, 4):
            m = row >= dist
            cb = jnp.where(m, ca * pltpu.roll(cb, dist, 0) + cb, cb)
            ca = jnp.where(m, ca * pltpu.roll(ca, dist, 0), ca)
        hb = ca * hprev + cb
        out_ref[rows, :] = hb
        return hb[7:8, :]
    return lax.fori_loop(0, nblk, blk, h0)


def _scan_rev(a_ref, d_ref, out_ref, g0, nblk):
    def blk(k, g):
        rows = pl.ds(pl.multiple_of((nblk - 1 - k) * 8, 8), 8)
        a, cb = a_ref[rows, :], d_ref[rows, :]
        row = _row_iota(a.shape)
        ca = jnp.where(row == 7, 1.0, pltpu.roll(a, 7, 0))
        for dist in (1, 2, 4):
            m = row < 8 - dist
            cb = jnp.where(m, cb + ca * pltpu.roll(cb, 8 - dist, 0), cb)
            ca = jnp.where(m, ca * pltpu.roll(ca, 8 - dist, 0), ca)
        lam = cb + ca * g
        out_ref[rows, :] = lam
        return a[0:1, :] * lam[0:1, :]
    return lax.fori_loop(0, nblk, blk, g0)


def _branch_a_fwd(z, wa, ba, wr, br, wi, bi, lam):
    t = z.shape[0]
    d = wa.shape[1]
    tt = _pick(t, ROW_TILE_EW, HALO_B)
    hb = tt // HALO_A
    nh = d // HEAD_DIM

    def body(xa_ref, halo_ref, sa_ref, wa_ref, ba_ref, wr_ref, br_ref, wi_ref, bi_ref, lam_ref,
             ya_ref, hl_ref, xext, gr_s, gi_s, a_s, carry):
        i = pl.program_id(0)

        @pl.when(i == 0)
        def _():
            carry[...] = jnp.zeros_like(carry)
        xext[0:HALO_A, :] = jnp.where(i == 0, 0.0, halo_ref[...])
        xext[HALO_A:, :] = xa_ref[...]
        ca = _conv_a(xext, wa_ref, ba_ref[...], tt)
        r, ig = _gates_a(ca, wr_ref, wi_ref, br_ref[...], bi_ref[...], gr_s, gi_s)
        a, mult = _decay_a(r, _softplus(-lam_ref[...]))
        a_s[...] = a
        hl_ref[...] = mult * (ig * ca)
        carry[...] = _scan_fwd(a_s, hl_ref, hl_ref, carry[...], tt // 8)
        sa = sa_ref[...]
        ya_ref[...] = (hl_ref[...] * (sa * _sigmoid(sa))).astype(BF16)

    vec = _whole((1, d))
    return pl.pallas_call(
        body, grid=(t // tt,),
        in_specs=[pl.BlockSpec((tt, d), lambda i: (i, 0)),
                  pl.BlockSpec((HALO_A, d), lambda i: (jnp.maximum(i * hb - 1, 0), 0)),
                  pl.BlockSpec((tt, d), lambda i: (i, 1)),
                  _whole((CONV_A, d)), vec, _whole((nh, HEAD_DIM, HEAD_DIM)), vec,
                  _whole((nh, HEAD_DIM, HEAD_DIM)), vec, vec],
        out_specs=[pl.BlockSpec((tt, d), lambda i: (i, 0)), pl.BlockSpec((tt, d), lambda i: (i, 0))],
        out_shape=[jax.ShapeDtypeStruct((t, d), BF16), jax.ShapeDtypeStruct((t, d), F32)],
        scratch_shapes=[pltpu.VMEM((tt + HALO_A, d), F32), pltpu.VMEM((tt, d), F32), pltpu.VMEM((tt, d), F32),
                        pltpu.VMEM((tt, d), F32), pltpu.VMEM((1, d), F32)],
        compiler_params=_params("arbitrary"), name="branch_a_fwd")(z, z, z, wa, ba, wr, br, wi, bi, lam)


def _glu_ext(gext, vb_ref, gb_ref, vbh_ref, gbh_ref, first):
    gext[0:HALO_B, :] = jnp.where(first, 0.0, vbh_ref[...] * _sigmoid(gbh_ref[...]))
    vb = vb_ref[...]
    sg = _sigmoid(gb_ref[...])
    gext[HALO_B:, :] = vb * sg
    return vb, sg


def _conv_b(gext, wb_ref, bb, tt):
    acc = bb
    for k in range(CONV_B):
        acc = acc + wb_ref[k:k + 1, :] * gext[pl.ds(HALO_B - (CONV_B - 1) + k, tt), :]
    return acc


def _layer_norm_stats(cb):
    mu = jnp.mean(cb, axis=-1, keepdims=True)
    xc = cb - mu
    rstd = lax.rsqrt(jnp.mean(xc * xc, axis=-1, keepdims=True) + EPS)
    return xc * rstd, rstd


def _branch_b_fwd(z, wb, bb, lg, lb):
    t = z.shape[0]
    d = wb.shape[1]
    tt = _pick(t, ROW_TILE_EW, HALO_B)
    hb = tt // HALO_B

    def body(vb_ref, gb_ref, vbh_ref, gbh_ref, sb_ref, wb_ref, bb_ref, lg_ref, lb_ref, yb_ref, gext):
        _glu_ext(gext, vb_ref, gb_ref, vbh_ref, gbh_ref, pl.program_id(0) == 0)
        cb = _conv_b(gext, wb_ref, bb_ref[...], tt)
        xhat, _ = _layer_norm_stats(cb)
        ln = xhat * lg_ref[...] + lb_ref[...]
        sb = sb_ref[...]
        yb_ref[...] = (ln * _sigmoid(ln) * (sb * _sigmoid(sb))).astype(BF16)

    vec = _whole((1, d))
    tile = lambda s: pl.BlockSpec((tt, d), lambda i: (i, s))
    halo = lambda s: pl.BlockSpec((HALO_B, d), lambda i: (jnp.maximum(i * hb - 1, 0), s))
    return pl.pallas_call(
        body, grid=(t // tt,),
        in_specs=[tile(2), tile(3), halo(2), halo(3), tile(4), _whole((CONV_B, d)), vec, vec, vec],
        out_specs=pl.BlockSpec((tt, d), lambda i: (i, 0)),
        out_shape=jax.ShapeDtypeStruct((t, d), BF16),
        scratch_shapes=[pltpu.VMEM((tt + HALO_B, d), F32)],
        compiler_params=_params("parallel"), name="branch_b_fwd")(z, z, z, z, z, wb, bb, lg, lb)


def _merge_fwd(ya, yb, z, wpa, wpb):
    t, d = ya.shape
    tm = _pick(t, ROW_TILE_MM, 16)
    tn = _pick(d, COL_TILE, 128)
    nb = d // tn

    def body(ya_ref, yb_ref, ma_ref, mb_ref, wpa_ref, wpb_ref, mg_ref, pa_ref, pb_ref):
        pa = jnp.dot(ya_ref[...], wpa_ref[...], preferred_element_type=F32)
        pb = jnp.dot(yb_ref[...], wpb_ref[...], preferred_element_type=F32)
        mg_ref[...] = (_sigmoid(ma_ref[...]) * pa + _sigmoid(mb_ref[...]) * pb).astype(BF16)
        pa_ref[...] = pa.astype(BF16)
        pb_ref[...] = pb.astype(BF16)

    rows = pl.BlockSpec((tm, d), lambda i, j: (i, 0))
    wcol = pl.BlockSpec((d, tn), lambda i, j: (0, j))
    outb = pl.BlockSpec((tm, tn), lambda i, j: (i, j))
    return pl.pallas_call(
        body, grid=(t // tm, nb),
        in_specs=[rows, rows, pl.BlockSpec((tm, tn), lambda i, j: (i, 5 * nb + j)),
                  pl.BlockSpec((tm, tn), lambda i, j: (i, 6 * nb + j)), wcol, wcol],
        out_specs=[outb, outb, outb],
        out_shape=[jax.ShapeDtypeStruct((t, d), BF16)] * 3,
        compiler_params=_params("parallel", "arbitrary"), name="merge_fwd")(ya, yb, z, z, wpa, wpb)


def _out_fwd(h, merged, wout):
    t, d = h.shape
    tm = _pick(t, ROW_TILE_MM, 16)
    tn = _pick(d, COL_TILE, 128)

    def body(h_ref, mg_ref, w_ref, o_ref):
        o_ref[...] = h_ref[...] + jnp.dot(mg_ref[...], w_ref[...], preferred_element_type=F32)

    return pl.pallas_call(
        body, grid=(t // tm, d // tn),
        in_specs=[pl.BlockSpec((tm, tn), lambda i, j: (i, j)), pl.BlockSpec((tm, d), lambda i, j: (i, 0)),
                  pl.BlockSpec((d, tn), lambda i, j: (0, j))],
        out_specs=pl.BlockSpec((tm, tn), lambda i, j: (i, j)),
        out_shape=jax.ShapeDtypeStruct((t, d), F32),
        compiler_params=_params("parallel", "arbitrary"), name="out_fwd")(h, merged, wout)


def _loss_head(h, g, target):
    t, d = h.shape
    tt = _pick(t, ROW_TILE_EW, 16)

    def body(h_ref, g_ref, tg_ref, loss_ref, dh_ref, dg_ref):
        i = pl.program_id(0)

        @pl.when(i == 0)
        def _():
            loss_ref[...] = jnp.zeros_like(loss_ref)
            dg_ref[...] = jnp.zeros_like(dg_ref)
        v = h_ref[...]
        g = g_ref[...]
        rinv = lax.rsqrt(jnp.mean(v * v, axis=-1, keepdims=True) + EPS)
        xh = v * rinv
        valid = (i * tt + _row_iota(v.shape)) >= N_META
        diff = jnp.where(valid, xh * g - tg_ref[...], 0.0)
        loss_ref[...] += (0.5 / d) * jnp.sum(diff * diff, axis=0, keepdims=True)
        dy = diff * (1.0 / d)
        dg_ref[...] += jnp.sum(dy * xh, axis=0, keepdims=True)
        dxh = dy * g
        dh_ref[...] = rinv * (dxh - xh * jnp.mean(dxh * xh, axis=-1, keepdims=True))

    tile = pl.BlockSpec((tt, d), lambda i: (i, 0))
    vec = _whole((1, d))
    return pl.pallas_call(
        body, grid=(t // tt,), in_specs=[tile, vec, tile], out_specs=[vec, tile, vec],
        out_shape=[jax.ShapeDtypeStruct((1, d), F32), jax.ShapeDtypeStruct((t, d), F32),
                   jax.ShapeDtypeStruct((1, d), F32)],
        compiler_params=_params("arbitrary"), name="loss_head")(h, g, target)


def _dmerged_bwd(dho, wout, z, pa, pb):
    t, d = dho.shape
    tm = _pick(t, ROW_TILE_MM, 16)
    tn = _pick(d, COL_TILE, 128)
    nb = d // tn

    def body(dho_ref, w_ref, ma_ref, mb_ref, pa_ref, pb_ref, dpa_ref, dpb_ref, dma_ref, dmb_ref, dho_s):
        @pl.when(pl.program_id(1) == 0)
        def _():
            dho_s[...] = dho_ref[...].astype(BF16)
        dm = lax.dot_general(dho_s[...], w_ref[...], NT_DIMS, preferred_element_type=F32)
        sa = _sigmoid(ma_ref[...])
        sb = _sigmoid(mb_ref[...])
        dpa_ref[...] = (dm * sa).astype(BF16)
        dpb_ref[...] = (dm * sb).astype(BF16)
        dma_ref[...] = (dm * pa_ref[...].astype(F32) * (sa * (1.0 - sa))).astype(BF16)
        dmb_ref[...] = (dm * pb_ref[...].astype(F32) * (sb * (1.0 - sb))).astype(BF16)

    blk = pl.BlockSpec((tm, tn), lambda i, j: (i, j))
    return pl.pallas_call(
        body, grid=(t // tm, nb),
        in_specs=[pl.BlockSpec((tm, d), lambda i, j: (i, 0)), pl.BlockSpec((tn, d), lambda i, j: (j, 0)),
                  pl.BlockSpec((tm, tn), lambda i, j: (i, 5 * nb + j)),
                  pl.BlockSpec((tm, tn), lambda i, j: (i, 6 * nb + j)), blk, blk],
        out_specs=[blk, blk, blk, blk],
        out_shape=[jax.ShapeDtypeStruct((t, d), BF16)] * 4,
        scratch_shapes=[pltpu.VMEM((tm, d), BF16)],
        compiler_params=_params("parallel", "arbitrary"), name="dmerged_bwd")(dho, wout, z, z, pa, pb)


def _matmul_nt(a, w, name):
    t, k = a.shape
    n = w.shape[0]
    tm = _pick(t, ROW_TILE_MM, 16)
    tn = _pick(n, COL_TILE, 128)

    def body(a_ref, w_ref, o_ref):
        o_ref[...] = lax.dot_general(a_ref[...], w_ref[...], NT_DIMS, preferred_element_type=F32)

    return pl.pallas_call(
        body, grid=(t // tm, n // tn),
        in_specs=[pl.BlockSpec((tm, k), lambda i, j: (i, 0)), pl.BlockSpec((tn, k), lambda i, j: (j, 0))],
        out_specs=pl.BlockSpec((tm, tn), lambda i, j: (i, j)),
        out_shape=jax.ShapeDtypeStruct((t, n), F32),
        compiler_params=_params("parallel", "arbitrary"), name=name)(a, w)


def _matmul_tn(a, b, name):
    t, m = a.shape
    n = b.shape[1]
    tk = _pick(t, ROW_TILE_MM, 16)
    tn = _pick(n, COL_TILE, 128)

    def body(a_ref, b_ref, o_ref):
        @pl.when(pl.program_id(1) == 0)
        def _():
            o_ref[...] = jnp.zeros_like(o_ref)
        o_ref[...] += lax.dot_general(a_ref[...].astype(BF16), b_ref[...].astype(BF16), TN_DIMS,
                                      preferred_element_type=F32)

    return pl.pallas_call(
        body, grid=(n // tn, t // tk),
        in_specs=[pl.BlockSpec((tk, m), lambda j, s: (s, 0)), pl.BlockSpec((tk, tn), lambda j, s: (s, j))],
        out_specs=pl.BlockSpec((m, tn), lambda j, s: (0, j)),
        out_shape=jax.ShapeDtypeStruct((m, n), F32),
        compiler_params=_params("parallel", "arbitrary"), name=name)(a, b)


def _dwin_bwd(hn, dzs):
    t, d = hn.shape
    nq = N_SPLIT * d // N_CHIPS
    tk = _pick(t, ROW_TILE_MM, 16)
    tn = _pick(nq, COL_TILE, 128)
    bps = nq // tn
    bpz = d // tn

    def body(hn_ref, *rest):
        dz_refs, o_ref = rest[:N_SPLIT], rest[N_SPLIT]
        j = pl.program_id(0)

        @pl.when(pl.program_id(1) == 0)
        def _():
            o_ref[...] = jnp.zeros_like(o_ref)
        for s in range(N_SPLIT):
            @pl.when(j // bpz == s)
            def _(s=s):
                o_ref[...] += lax.dot_general(hn_ref[...], dz_refs[s][...], TN_DIMS, preferred_element_type=F32)

    def dz_spec(s):
        def imap(j, r):
            on = (j // bpz) == s
            return (jnp.where(on, r, 0), jnp.where(on, j % bpz, 0))
        return pl.BlockSpec((tk, tn), imap)

    return pl.pallas_call(
        body, grid=(N_SPLIT * bpz, t // tk),
        in_specs=[pl.BlockSpec((tk, d), lambda j, r: (r, 0))] + [dz_spec(s) for s in range(N_SPLIT)],
        out_specs=pl.BlockSpec((None, d, tn), lambda j, r: (j // bps, 0, j % bps)),
        out_shape=jax.ShapeDtypeStruct((N_CHIPS, d, nq), F32),
        compiler_params=_params("parallel", "arbitrary"), name="dwin_bwd")(hn, *dzs)


def _din_bwd(dzs, w, h, g, dho):
    t, d = h.shape
    nq = w.shape[2]
    tm = _pick(t, ROW_TILE_MM_BWD, 16)
    tk = _pick(nq, COL_TILE, 128)
    bps = nq // tk
    bpz = d // tk
    nk = N_SPLIT * bpz

    def body(*refs):
        dz_refs = refs[:N_SPLIT]
        w_ref, h_ref, g_ref, dho_ref, dh_ref, dg_ref, acc = refs[N_SPLIT:]
        i, k = pl.program_id(0), pl.program_id(1)

        @pl.when(k == 0)
        def _():
            acc[...] = jnp.zeros_like(acc)

        @pl.when((i == 0) & (k == 0))
        def _():
            dg_ref[...] = jnp.zeros_like(dg_ref)
        for s in range(N_SPLIT):
            @pl.when(k // bpz == s)
            def _(s=s):
                acc[...] += lax.dot_general(dz_refs[s][...], w_ref[...], NT_DIMS, preferred_element_type=F32)

        @pl.when(k == nk - 1)
        def _():
            v = h_ref[...]
            rinv = lax.rsqrt(jnp.mean(v * v, axis=-1, keepdims=True) + EPS)
            xh = v * rinv
            dhn = acc[...]
            dg_ref[...] += jnp.sum(dhn * xh, axis=0, keepdims=True)
            dxh = dhn * g_ref[...]
            dh_ref[...] = dho_ref[...] + rinv * (dxh - xh * jnp.mean(dxh * xh, axis=-1, keepdims=True))

    def dz_spec(s):
        return pl.BlockSpec((tm, tk), lambda i, k: (i, jnp.clip(k - s * bpz, 0, bpz - 1)))

    rows = pl.BlockSpec((tm, d), lambda i, k: (i, 0))
    vec = _whole((1, d))
    return pl.pallas_call(
        body, grid=(t // tm, nk),
        in_specs=[dz_spec(s) for s in range(N_SPLIT)]
        + [pl.BlockSpec((None, d, tk), lambda i, k: (k // bps, 0, k % bps)), rows, vec, rows],
        out_specs=[rows, vec],
        out_shape=[jax.ShapeDtypeStruct((t, d), F32), jax.ShapeDtypeStruct((1, d), F32)],
        scratch_shapes=[pltpu.VMEM((tm, d), F32)],
        compiler_params=_params("arbitrary", "arbitrary"), name="din_bwd")(*dzs, w, h, g, dho)


def _branch_b_bwd(z, dyb, wb, bb, lg, lb):
    t = z.shape[0]
    d = wb.shape[1]
    tt = _pick(t, ROW_TILE_EW, HALO_B)
    hb = tt // HALO_B
    nt = t // tt

    def body(vb_ref, gb_ref, vbh_ref, gbh_ref, sb_ref, dyb_ref, wb_ref, bb_ref, lg_ref, lb_ref,
             dvb_ref, dgb_ref, dsb_ref, dwb_ref, dbb_ref, dlg_ref, dlb_ref, gext, dext, carry):
        i = pl.program_id(0)

        @pl.when(i == 0)
        def _():
            for ref in (dwb_ref, dbb_ref, dlg_ref, dlb_ref, carry):
                ref[...] = jnp.zeros_like(ref)
        vb, sg = _glu_ext(gext, vb_ref, gb_ref, vbh_ref, gbh_ref, i == nt - 1)
        cb = _conv_b(gext, wb_ref, bb_ref[...], tt)
        xhat, rstd = _layer_norm_stats(cb)
        lg = lg_ref[...]
        ln = xhat * lg + lb_ref[...]
        sl = _sigmoid(ln)
        sb = sb_ref[...]
        ss = _sigmoid(sb)
        dyb = dyb_ref[...]
        dln = dyb * (sb * ss) * _dsilu(ln, sl)
        dsb_ref[...] = (dyb * (ln * sl) * _dsilu(sb, ss)).astype(BF16)
        dlg_ref[...] += jnp.sum(dln * xhat, axis=0, keepdims=True)
        dlb_ref[...] += jnp.sum(dln, axis=0, keepdims=True)
        dxh = dln * lg
        dcb = rstd * (dxh - jnp.mean(dxh, axis=-1, keepdims=True)
                      - xhat * jnp.mean(dxh * xhat, axis=-1, keepdims=True))
        dbb_ref[...] += jnp.sum(dcb, axis=0, keepdims=True)
        dext[0:tt, :] = dcb
        dext[tt:, :] = carry[...]
        carry[...] = dcb[0:HALO_B, :]
        dglu = jnp.zeros_like(dcb)
        for k in range(CONV_B):
            off = HALO_B - (CONV_B - 1) + k
            dwb_ref[k:k + 1, :] += jnp.sum(dext[0:tt, :] * gext[pl.ds(off, tt), :], axis=0, keepdims=True)
            dglu = dglu + wb_ref[k:k + 1, :] * dext[pl.ds(CONV_B - 1 - k, tt), :]
        dvb_ref[...] = (dglu * sg).astype(BF16)
        dgb_ref[...] = (dglu * vb * (sg * (1.0 - sg))).astype(BF16)

    rev = lambda i: nt - 1 - i
    vec = _whole((1, d))
    tile = lambda s: pl.BlockSpec((tt, d), lambda i: (rev(i), s))
    halo = lambda s: pl.BlockSpec((HALO_B, d), lambda i: (jnp.maximum(rev(i) * hb - 1, 0), s))
    otile = pl.BlockSpec((tt, d), lambda i: (rev(i), 0))
    return pl.pallas_call(
        body, grid=(nt,),
        in_specs=[tile(2), tile(3), halo(2), halo(3), tile(4), otile, _whole((CONV_B, d)), vec, vec, vec],
        out_specs=[otile, otile, otile, _whole((CONV_B, d)), vec, vec, vec],
        out_shape=[jax.ShapeDtypeStruct((t, d), BF16)] * 3
        + [jax.ShapeDtypeStruct((CONV_B, d), F32)] + [jax.ShapeDtypeStruct((1, d), F32)] * 3,
        scratch_shapes=[pltpu.VMEM((tt + HALO_B, d), F32), pltpu.VMEM((tt + HALO_B, d), F32),
                        pltpu.VMEM((HALO_B, d), F32)],
        compiler_params=_params("arbitrary"), name="branch_b_bwd")(z, z, z, z, z, dyb, wb, bb, lg, lb)


def _branch_a_bwd(z, hl, dya, wa, ba, wr, br, wi, bi, lam):
    t = z.shape[0]
    d = wa.shape[1]
    tt = _pick(t, ROW_TILE_EW, HALO_B)
    hb = tt // HALO_A
    nt = t // tt
    nh = d // HEAD_DIM

    def body(xa_ref, xah_ref, sa_ref, hl_ref, hlh_ref, dya_ref, wa_ref, ba_ref, wr_ref, br_ref, wi_ref, bi_ref,
             lam_ref, dxa_ref, dsa_ref, dwa_ref, dba_ref, dbr_ref, dbi_ref, dlam_ref, dwr_ref, dwi_ref,
             xext, hext, dext, gr_s, gi_s, a_s, lam_s, gcarry, dcarry):
        i = pl.program_id(0)
        first = i == nt - 1

        @pl.when(i == 0)
        def _():
            for ref in (dwa_ref, dba_ref, dbr_ref, dbi_ref, dlam_ref, dwr_ref, dwi_ref, gcarry, dcarry):
                ref[...] = jnp.zeros_like(ref)
        xext[0:HALO_A, :] = jnp.where(first, 0.0, xah_ref[...])
        xext[HALO_A:, :] = xa_ref[...]
        hext[0:HALO_A, :] = jnp.where(first, 0.0, hlh_ref[...])
        hext[HALO_A:, :] = hl_ref[...]
        ca = _conv_a(xext, wa_ref, ba_ref[...], tt)
        r, ig = _gates_a(ca, wr_ref, wi_ref, br_ref[...], bi_ref[...], gr_s, gi_s)
        lam = lam_ref[...]
        sp = _softplus(-lam)
        a, mult = _decay_a(r, sp)
        sa = sa_ref[...]
        ss = _sigmoid(sa)
        dya = dya_ref[...]
        dsa_ref[...] = (dya * hl_ref[...] * _dsilu(sa, ss)).astype(BF16)
        a_s[...] = a
        lam_s[...] = dya * (sa * ss)
        gcarry[...] = _scan_rev(a_s, lam_s, lam_s, gcarry[...], tt // 8)
        du = lam_s[...]
        da = du * hext[pl.ds(HALO_A - 1, tt), :]
        dig = du * mult * ca
        dca = du * mult * ig
        dlog_a = da * a - (du * ig * ca) * jnp.where(mult > 0.0, a * a / mult, 0.0)
        dgr = (dlog_a * (-LRU_C * sp)) * r * (1.0 - r)
        dgi = dig * ig * (1.0 - ig)
        dlam_ref[...] += jnp.sum(dlog_a * r, axis=0, keepdims=True) * (LRU_C * _sigmoid(-lam))
        dbr_ref[...] += jnp.sum(dgr, axis=0, keepdims=True)
        dbi_ref[...] += jnp.sum(dgi, axis=0, keepdims=True)
        cab = ca.astype(BF16)
        dgrb = dgr.astype(BF16)
        dgib = dgi.astype(BF16)
        for hh in range(nh):
            sl = slice(hh * HEAD_DIM, (hh + 1) * HEAD_DIM)
            dwr_ref[hh] += lax.dot_general(cab[:, sl], dgrb[:, sl], TN_DIMS, preferred_element_type=F32)
            dwi_ref[hh] += lax.dot_general(cab[:, sl], dgib[:, sl], TN_DIMS, preferred_element_type=F32)
            gr_s[:, sl] = (lax.dot_general(dgrb[:, sl], wr_ref[hh], NT_DIMS, preferred_element_type=F32)
                           + lax.dot_general(dgib[:, sl], wi_ref[hh], NT_DIMS, preferred_element_type=F32))
        dca = dca + gr_s[...]
        dba_ref[...] += jnp.sum(dca, axis=0, keepdims=True)
        dext[0:tt, :] = dca
        dext[tt:, :] = dcarry[...]
        dcarry[...] = dca[0:HALO_A, :]
        dxa = jnp.zeros_like(dca)
        for k in range(CONV_A):
            off = HALO_A - (CONV_A - 1) + k
            dwa_ref[k:k + 1, :] += jnp.sum(dext[0:tt, :] * xext[pl.ds(off, tt), :], axis=0, keepdims=True)
            dxa = dxa + wa_ref[k:k + 1, :] * dext[pl.ds(CONV_A - 1 - k, tt), :]
        dxa_ref[...] = dxa.astype(BF16)

    rev = lambda i: nt - 1 - i
    vec = _whole((1, d))
    hw = _whole((nh, HEAD_DIM, HEAD_DIM))
    tile = lambda s: pl.BlockSpec((tt, d), lambda i: (rev(i), s))
    halo = pl.BlockSpec((HALO_A, d), lambda i: (jnp.maximum(rev(i) * hb - 1, 0), 0))
    big = pltpu.VMEM((tt + HALO_A, d), F32)
    full = pltpu.VMEM((tt, d), F32)
    return pl.pallas_call(
        body, grid=(nt,),
        in_specs=[tile(0), halo, tile(1), tile(0), halo, tile(0), _whole((CONV_A, d)), vec, hw, vec, hw, vec, vec],
        out_specs=[tile(0), tile(0), _whole((CONV_A, d)), vec, vec, vec, vec, hw, hw],
        out_shape=[jax.ShapeDtypeStruct((t, d), BF16)] * 2 + [jax.ShapeDtypeStruct((CONV_A, d), F32)]
        + [jax.ShapeDtypeStruct((1, d), F32)] * 4 + [jax.ShapeDtypeStruct((nh, HEAD_DIM, HEAD_DIM), F32)] * 2,
        scratch_shapes=[big, big, big, full, full, full, full, pltpu.VMEM((1, d), F32), pltpu.VMEM((HALO_A, d), F32)],
        compiler_params=_params("arbitrary"), name="branch_a_bwd")(z, z, z, hl, hl, dya, wa, ba, wr, br, wi, bi, lam)


def _rows_block(rows, cols, n_arrays=1):
    cap = max(8, BLOCK_BYTES // (4 * cols * n_arrays))
    return rows if rows <= cap else _pick(rows, cap, 8)


def _sum_halves(g, xo, c_idx):
    nq, r, c = g.shape
    rh = r // 2
    tr = _rows_block(rh, c)
    nb = rh // tr

    def body(c_ref, g_ref, x_ref, o_ref):
        o_ref[...] = (g_ref[...] + x_ref[...]).astype(BF16)

    grid_spec = pltpu.PrefetchScalarGridSpec(
        num_scalar_prefetch=1, grid=(nq, nb),
        in_specs=[pl.BlockSpec((None, tr, c), lambda q, i, cr: (q, cr[0] * nb + i, 0)),
                  pl.BlockSpec((None, tr, c), lambda q, i, cr: (q, i, 0))],
        out_specs=pl.BlockSpec((None, tr, c), lambda q, i, cr: (q, i, 0)))
    return pl.pallas_call(
        body, grid_spec=grid_spec, out_shape=jax.ShapeDtypeStruct((nq, rh, c), BF16),
        compiler_params=_params("parallel", "parallel"), name="sum_halves")(c_idx, g, xo)


def _sum_slots(y, name):
    ns, r, c = y.shape
    tr = _rows_block(r, c, ns)

    def body(y_ref, o_ref):
        acc = y_ref[0].astype(F32)
        for s in range(1, ns):
            acc = acc + y_ref[s].astype(F32)
        o_ref[...] = acc

    return pl.pallas_call(
        body, grid=(r // tr,), in_specs=[pl.BlockSpec((ns, tr, c), lambda i: (0, i, 0))],
        out_specs=pl.BlockSpec((tr, c), lambda i: (i, 0)), out_shape=jax.ShapeDtypeStruct((r, c), F32),
        compiler_params=_params("parallel"), name=name)(y)


def _adamw(w, g, m, v, name):
    shape = w.shape
    c = shape[-1]
    r = w.size // c
    tr = _rows_block(r, c)
    c1 = 1.0 / (1.0 - ADAM_B1 ** ADAM_STEP)
    c2 = 1.0 / (1.0 - ADAM_B2 ** ADAM_STEP)

    def body(w_ref, g_ref, m_ref, v_ref, go_ref, d_ref, mo_ref, vo_ref):
        gv = g_ref[...]
        mn = ADAM_B1 * m_ref[...] + (1.0 - ADAM_B1) * gv
        vn = ADAM_B2 * v_ref[...] + (1.0 - ADAM_B2) * (gv * gv)
        go_ref[...] = gv
        mo_ref[...] = mn
        vo_ref[...] = vn
        d_ref[...] = -ADAM_LR * ((mn * c1) / (jnp.sqrt(vn * c2) + ADAM_EPS) + ADAM_WD * w_ref[...])

    blk = pl.BlockSpec((tr, c), lambda i: (i, 0))
    outs = pl.pallas_call(
        body, grid=(r // tr,), in_specs=[blk] * 4, out_specs=[blk] * 4,
        out_shape=[jax.ShapeDtypeStruct((r, c), F32)] * 4,
        compiler_params=_params("parallel"), name=name)(*(a.reshape(r, c) for a in (w, g, m, v)))
    return tuple(o.reshape(shape) for o in outs)


HBM_SPEC = pl.BlockSpec(memory_space=pltpu.HBM)


def _place():
    return lax.axis_index("x"), lax.axis_index("y"), lax.axis_index("c")


def _other_chips(x, y):
    return [(1 - x, y), (x, 1 - y), (1 - x, 1 - y)]


def _half(ref, which):
    rows = ref.shape[0] // 2
    return ref.at[pl.ds(which * rows, rows)]


def _gather_chips(arrays, name):
    n = len(arrays)

    def body(*refs):
        ins, outs = refs[:n], refs[n:2 * n]
        send1, recv1, send2, recv2, lsem = refs[2 * n:]
        x, y, c = _place()
        me = 2 * x + y
        chips = _other_chips(x, y)
        sibling = (x, y, 1 - c)
        local = [pltpu.make_async_copy(ins[a], outs[a].at[me], lsem.at[a]) for a in range(n)]
        for cp in local:
            cp.start()

        def over_ici(a, j):
            return pltpu.make_async_remote_copy(
                src_ref=_half(ins[a], c), dst_ref=_half(outs[a].at[me], c),
                send_sem=send1.at[a, j], recv_sem=recv1.at[a, j], device_id=(*chips[j], c), device_id_type=MESH)

        def landed(a, j, which):
            q = 2 * chips[j][0] + chips[j][1]
            return _half(outs[a].at[q], which)

        def to_sibling(a, j):
            return pltpu.make_async_remote_copy(
                src_ref=landed(a, j, c), dst_ref=landed(a, j, c),
                send_sem=send2.at[a, j], recv_sem=recv2.at[a, j], device_id=sibling, device_id_type=MESH)

        sent = [over_ici(a, j) for a in range(n) for j in range(3)]
        for cp in sent:
            cp.start()
        for a in range(n):
            for j in range(3):
                pltpu.make_async_remote_copy(
                    src_ref=_half(ins[a], c), dst_ref=landed(a, j, c), send_sem=send1.at[a, j],
                    recv_sem=recv1.at[a, j], device_id=(*chips[j], c), device_id_type=MESH).wait_recv()
                cp = to_sibling(a, j)
                cp.start()
                sent.append(cp)
        for a in range(n):
            for j in range(3):
                pltpu.make_async_remote_copy(
                    src_ref=landed(a, j, 1 - c), dst_ref=landed(a, j, 1 - c), send_sem=send2.at[a, j],
                    recv_sem=recv2.at[a, j], device_id=sibling, device_id_type=MESH).wait_recv()
        for cp in sent:
            cp.wait_send()
        for cp in local:
            cp.wait()

    sems = pltpu.SemaphoreType.DMA((n, 3))
    return pl.pallas_call(
        body, in_specs=[HBM_SPEC] * n, out_specs=[HBM_SPEC] * n,
        out_shape=[jax.ShapeDtypeStruct((N_CHIPS,) + a.shape, a.dtype) for a in arrays],
        scratch_shapes=[sems, sems, sems, sems, pltpu.SemaphoreType.DMA((n,))], name=name)(*arrays)


def _swap_halves(grads, name):
    n = len(grads)

    def body(*refs):
        ins, outs = refs[:n], refs[n:2 * n]
        send, recv = refs[2 * n:]
        x, y, c = _place()
        copies = []
        for a in range(n):
            rows = ins[a].shape[1] // 2
            copies.append(pltpu.make_async_remote_copy(
                src_ref=ins[a].at[:, pl.ds((1 - c) * rows, rows)], dst_ref=outs[a],
                send_sem=send.at[a], recv_sem=recv.at[a], device_id=(x, y, 1 - c), device_id_type=MESH))
        for cp in copies:
            cp.start()
        for cp in copies:
            cp.wait()

    sems = pltpu.SemaphoreType.DMA((n,))
    return pl.pallas_call(
        body, in_specs=[HBM_SPEC] * n, out_specs=[HBM_SPEC] * n,
        out_shape=[jax.ShapeDtypeStruct((g.shape[0], g.shape[1] // 2) + g.shape[2:], g.dtype) for g in grads],
        scratch_shapes=[sems, sems], name=name)(*grads)


def _scatter_chips(parts, name):
    n = len(parts)

    def body(*refs):
        ins, outs = refs[:n], refs[n:2 * n]
        send, recv, lsem = refs[2 * n:]
        x, y, c = _place()
        me = 2 * x + y
        chips = _other_chips(x, y)
        local = [pltpu.make_async_copy(ins[a].at[me], outs[a].at[me], lsem.at[a]) for a in range(n)]
        for cp in local:
            cp.start()
        copies = []
        for a in range(n):
            for j in range(3):
                q = 2 * chips[j][0] + chips[j][1]
                copies.append(pltpu.make_async_remote_copy(
                    src_ref=ins[a].at[q], dst_ref=outs[a].at[me], send_sem=send.at[a, j], recv_sem=recv.at[a, j],
                    device_id=(*chips[j], c), device_id_type=MESH))
        for cp in copies:
            cp.start()
        for a in range(n):
            for j in range(3):
                q = 2 * chips[j][0] + chips[j][1]
                pltpu.make_async_remote_copy(
                    src_ref=ins[a].at[q], dst_ref=outs[a].at[q], send_sem=send.at[a, j], recv_sem=recv.at[a, j],
                    device_id=(*chips[j], c), device_id_type=MESH).wait_recv()
        for cp in copies:
            cp.wait_send()
        for cp in local:
            cp.wait()

    sems = pltpu.SemaphoreType.DMA((n, 3))
    return pl.pallas_call(
        body, in_specs=[HBM_SPEC] * n, out_specs=[HBM_SPEC] * n,
        out_shape=[jax.ShapeDtypeStruct(p.shape, p.dtype) for p in parts],
        scratch_shapes=[sems, sems, pltpu.SemaphoreType.DMA((n,))], name=name)(*parts)


def _join_halves(halves, stacks, layer, name):
    n = len(halves)

    def body(*refs):
        ins, outs = refs[:n], refs[2 * n:3 * n]
        send, recv, lsem = refs[3 * n:]
        x, y, c = _place()
        local, copies = [], []
        for a in range(n):
            dst = _half(outs[a].at[layer], c)
            local.append(pltpu.make_async_copy(ins[a], dst, lsem.at[a]))
            copies.append(pltpu.make_async_remote_copy(
                src_ref=ins[a], dst_ref=dst, send_sem=send.at[a], recv_sem=recv.at[a],
                device_id=(x, y, 1 - c), device_id_type=MESH))
        for cp in local + copies:
            cp.start()
        for a in range(n):
            pltpu.make_async_remote_copy(
                src_ref=ins[a], dst_ref=_half(outs[a].at[layer], 1 - c), send_sem=send.at[a], recv_sem=recv.at[a],
                device_id=(x, y, 1 - c), device_id_type=MESH).wait_recv()
        for cp in copies:
            cp.wait_send()
        for cp in local:
            cp.wait()

    sems = pltpu.SemaphoreType.DMA((n,))
    return pl.pallas_call(
        body, in_specs=[HBM_SPEC] * (2 * n), out_specs=[HBM_SPEC] * n,
        out_shape=[jax.ShapeDtypeStruct(s.shape, s.dtype) for s in stacks],
        input_output_aliases={n + k: k for k in range(n)},
        scratch_shapes=[sems, sems, sems], name=name)(*halves, *stacks)


def _gather_all(arrays, name):
    n = len(arrays)

    def body(*refs):
        ins, outs = refs[:n], refs[n:2 * n]
        send, recv, lsem = refs[2 * n:]
        x, y, c = _place()
        chips = _other_chips(x, y)
        sibling = (x, y, 1 - c)

        def slot(a, px, py, pc):
            return outs[a].at[4 * px + 2 * py + pc]

        def copy(a, k, block, to, src=None):
            return pltpu.make_async_remote_copy(
                src_ref=slot(a, *block) if src is None else src, dst_ref=slot(a, *block),
                send_sem=send.at[a, k], recv_sem=recv.at[a, k], device_id=to, device_id_type=MESH)

        local = [pltpu.make_async_copy(ins[a], slot(a, x, y, c), lsem.at[a]) for a in range(n)]
        for cp in local:
            cp.start()
        sent = []
        for a in range(n):
            sent.append(copy(a, 0, (x, y, c), sibling, src=ins[a]))
            sent += [copy(a, 1 + j, (x, y, c), (*chip, c), src=ins[a]) for j, chip in enumerate(chips)]
        for cp in sent:
            cp.start()
        for a in range(n):
            for j, chip in enumerate(chips):
                copy(a, 1 + j, (*chip, c), (x, y, c)).wait_recv()
                cp = copy(a, 4 + j, (*chip, c), sibling)
                cp.start()
                sent.append(cp)
        for a in range(n):
            copy(a, 0, sibling, (x, y, c)).wait_recv()
            for j, chip in enumerate(chips):
                copy(a, 4 + j, (*chip, 1 - c), (x, y, c)).wait_recv()
        for cp in sent:
            cp.wait_send()
        for cp in local:
            cp.wait()

    sems = pltpu.SemaphoreType.DMA((n, 7))
    return pl.pallas_call(
        body, in_specs=[HBM_SPEC] * n, out_specs=[HBM_SPEC] * n,
        out_shape=[jax.ShapeDtypeStruct((N_DEV,) + a.shape, a.dtype) for a in arrays],
        scratch_shapes=[sems, sems, pltpu.SemaphoreType.DMA((n,))], name=name)(*arrays)


def _layer_fwd(h, p):
    z, hn = _in_proj_fwd(h, p["norm_g"], p["w_in"])
    ya, hl = _branch_a_fwd(z, p["conv_a_w"], p["conv_a_b"], p["w_rg"], p["b_rg"], p["w_ig"], p["b_ig"], p["lam"])
    yb = _branch_b_fwd(z, p["conv_b_w"], p["conv_b_b"], p["ln_b_g"], p["ln_b_b"])
    merged, pa, pb = _merge_fwd(ya, yb, z, p["w_pa"], p["w_pb"])
    return _out_fwd(h, merged, p["w_out"]), (h, hn, z, hl, ya, yb, pa, pb, merged)


def _layer_bwd(dho, p, saved):
    h, hn, z, hl, ya, yb, pa, pb, merged = saved
    dpa, dpb, dma, dmb = _dmerged_bwd(dho, p["w_out"], z, pa, pb)
    g = {"w_out": _matmul_tn(merged, dho, "dw_out"), "w_proj_a": _matmul_tn(ya, dpa, "dw_proj_a"),
         "w_proj_b": _matmul_tn(yb, dpb, "dw_proj_b")}
    dya = _matmul_nt(dpa, p["w_pa"], "dya")
    dyb = _matmul_nt(dpb, p["w_pb"], "dyb")
    (dxa, dsa, g["conv_a_w"], g["conv_a_b"], g["b_rg"], g["b_ig"], g["lru_lambda"], g["w_rg"], g["w_ig"]) = \
        _branch_a_bwd(z, hl, dya, p["conv_a_w"], p["conv_a_b"], p["w_rg"], p["b_rg"], p["w_ig"], p["b_ig"], p["lam"])
    dvb, dgb, dsb, g["conv_b_w"], g["conv_b_b"], g["ln_b_g"], g["ln_b_b"] = \
        _branch_b_bwd(z, dyb, p["conv_b_w"], p["conv_b_b"], p["ln_b_g"], p["ln_b_b"])
    dzs = (dxa, dsa, dvb, dgb, dsb, dma, dmb)
    g["w_in"] = _dwin_bwd(hn, dzs)
    dh, g["norm_g"] = _din_bwd(dzs, p["w_in"], h, p["norm_g"], dho)
    return dh, g


def _forward_backward(h, target, layers, final_g, on_layer):
    saved = []
    for p in layers:
        h, s = _layer_fwd(h, p)
        saved.append(s)
    target = jnp.concatenate([jnp.zeros((N_META, h.shape[1]), F32), target], axis=0)
    loss_lanes, dh, g_final = _loss_head(h, final_g, target)
    for l in reversed(range(len(layers))):
        dh, g = _layer_bwd(dh, layers[l], saved[l])
        on_layer(l, g)
    return loss_lanes, dh, g_final


BIG = ("w_in", "w_proj_a", "w_proj_b", "w_out")
VECS = ("norm_g", "conv_a_b", "b_rg", "b_ig", "lru_lambda", "conv_b_b", "ln_b_g", "ln_b_b")


def _reduce_big(g, stacks, layer, c_idx):
    d = g["w_out"].shape[0]
    parts = [g["w_in"]] + [g[k].reshape(N_CHIPS, d // N_CHIPS, d) for k in BIG[1:]]
    theirs = _swap_halves(parts, "swap_halves")
    chip_sums = [_sum_halves(a, b, c_idx) for a, b in zip(parts, theirs)]
    landed = _scatter_chips(chip_sums, "scatter_chips")
    halves = [_sum_slots(y, "sum_chips") for y in landed]
    return _join_halves(halves, stacks, layer, f"join_halves_{layer}")


def kernel(x, meta, norm_g, w_in, conv_a_w, conv_a_b, w_rg, b_rg, w_ig, b_ig, lru_lambda, conv_b_w, conv_b_b, ln_b_g, ln_b_b, w_proj_a, w_proj_b, w_out, final_g, loss_target, m_meta, m_norm_g, m_w_in, m_conv_a_w, m_conv_a_b, m_w_rg, m_b_rg, m_w_ig, m_b_ig, m_lru_lambda, m_conv_b_w, m_conv_b_b, m_ln_b_g, m_ln_b_b, m_w_proj_a, m_w_proj_b, m_w_out, m_final_g, v_meta, v_norm_g, v_w_in, v_conv_a_w, v_conv_a_b, v_w_rg, v_b_rg, v_w_ig, v_b_ig, v_lru_lambda, v_conv_b_w, v_conv_b_b, v_ln_b_g, v_ln_b_b, v_w_proj_a, v_w_proj_b, v_w_out, v_final_g):
    names = ("meta", "norm_g", "w_in", "conv_a_w", "conv_a_b", "w_rg", "b_rg", "w_ig", "b_ig", "lru_lambda",
             "conv_b_w", "conv_b_b", "ln_b_g", "ln_b_b", "w_proj_a", "w_proj_b", "w_out", "final_g")
    w = dict(zip(names, (meta, norm_g, w_in, conv_a_w, conv_a_b, w_rg, b_rg, w_ig, b_ig, lru_lambda, conv_b_w,
                         conv_b_b, ln_b_g, ln_b_b, w_proj_a, w_proj_b, w_out, final_g)))
    m = dict(zip(names, (m_meta, m_norm_g, m_w_in, m_conv_a_w, m_conv_a_b, m_w_rg, m_b_rg, m_w_ig, m_b_ig,
                         m_lru_lambda, m_conv_b_w, m_conv_b_b, m_ln_b_g, m_ln_b_b, m_w_proj_a, m_w_proj_b, m_w_out,
                         m_final_g)))
    v = dict(zip(names, (v_meta, v_norm_g, v_w_in, v_conv_a_w, v_conv_a_b, v_w_rg, v_b_rg, v_w_ig, v_b_ig,
                         v_lru_lambda, v_conv_b_w, v_conv_b_b, v_ln_b_g, v_ln_b_b, v_w_proj_a, v_w_proj_b, v_w_out,
                         v_final_g)))
    depth, d = norm_g.shape
    dq = d // N_CHIPS
    px, py, pc = _place()
    chip = 2 * px + py
    c_idx = pc.astype(jnp.int32).reshape(1)

    meta_s, conv_a_s, conv_b_s = _gather_chips([meta, conv_a_w, conv_b_w], "gather_small")
    unshard = lambda s: jnp.concatenate([s[q] for q in range(N_CHIPS)], axis=-1)
    meta_f, conv_a_f, conv_b_f = unshard(meta_s), unshard(conv_a_s), unshard(conv_b_s)
    layers = []
    for l in range(depth):
        win, wpa, wpb, wo = _gather_chips(
            [w_in[l].astype(BF16), w_proj_a[l].astype(BF16), w_proj_b[l].astype(BF16), w_out[l].astype(BF16)],
            "gather_weights")
        layers.append({
            "w_in": win, "w_pa": wpa.reshape(d, d), "w_pb": wpb.reshape(d, d), "w_out": wo.reshape(d, d),
            "norm_g": norm_g[l][None], "conv_a_w": conv_a_f[l], "conv_a_b": conv_a_b[l][None],
            "w_rg": w_rg[l].astype(BF16), "b_rg": b_rg[l][None], "w_ig": w_ig[l].astype(BF16), "b_ig": b_ig[l][None],
            "lam": lru_lambda[l][None], "conv_b_w": conv_b_f[l], "conv_b_b": conv_b_b[l][None],
            "ln_b_g": ln_b_g[l][None], "ln_b_b": ln_b_b[l][None]})

    state = {"stacks": [jnp.zeros((depth, d, N_SPLIT * dq), F32)] + [jnp.zeros((depth, dq, d), F32)] * 3}
    small = [None] * depth

    def on_layer(l, g):
        state["stacks"] = _reduce_big(g, state["stacks"], l, c_idx)
        small[l] = g

    loss_lanes, dh, g_final = _forward_backward(jnp.concatenate([meta_f, x[0]], axis=0), loss_target[0], layers,
                                                final_g[None], on_layer)
    loss = lax.psum(jnp.sum(loss_lanes), ("x", "y", "c"))
    grads = dict(zip(BIG, state["stacks"]))

    vec_rows = [small[l][k] for k in VECS for l in range(depth)]
    vec_rows += [small[l]["conv_a_w"] for l in range(depth)] + [small[l]["conv_b_w"] for l in range(depth)]
    vec_rows += [dh[:N_META], g_final]
    packed_v = jnp.concatenate(vec_rows, axis=0)
    pad = (-packed_v.shape[0]) % 8
    packed_v = jnp.concatenate([packed_v, jnp.zeros((pad, d), F32)], axis=0)
    packed_g = jnp.concatenate([small[l][k].reshape(-1, HEAD_DIM) for k in ("w_rg", "w_ig") for l in range(depth)],
                               axis=0)
    all_v, all_g = _gather_all([packed_v, packed_g], "gather_small_grads")
    sum_v = _sum_slots(all_v, "sum_devices_v")
    sum_g = _sum_slots(all_g, "sum_devices_g")
    cols = lambda a: lax.dynamic_slice_in_dim(a, chip * dq, dq, axis=-1)
    for n_k, k in enumerate(VECS):
        grads[k] = sum_v[n_k * depth:(n_k + 1) * depth]
    o = len(VECS) * depth
    grads["conv_a_w"] = cols(sum_v[o:o + depth * CONV_A].reshape(depth, CONV_A, d))
    o += depth * CONV_A
    grads["conv_b_w"] = cols(sum_v[o:o + depth * CONV_B].reshape(depth, CONV_B, d))
    o += depth * CONV_B
    grads["meta"] = cols(sum_v[o:o + N_META])
    grads["final_g"] = sum_v[o + N_META]
    gate_rows = depth * (d // HEAD_DIM) * HEAD_DIM
    grads["w_rg"] = sum_g[:gate_rows].reshape(w_rg.shape)
    grads["w_ig"] = sum_g[gate_rows:].reshape(w_ig.shape)

    outs = {k: _adamw(w[k], grads[k], m[k], v[k], f"adamw_{k}") for k in names}
    dx = dh[N_META:][None]
    return (loss, dx, *[outs[k][0] for k in names], *[outs[k][1] for k in names],
            *[outs[k][2] for k in names], *[outs[k][3] for k in names])
```

```python
import functools

import jax
import jax.numpy as jnp
from jax import lax
from jax.experimental import pallas as pl
from jax.experimental.pallas import tpu as pltpu

F32 = jnp.float32
BF16 = jnp.bfloat16
MESH = pl.DeviceIdType.MESH

EPS = 1e-6
N_META = 16
HEAD_DIM = 128
CONV_A = 4
CONV_B = 31
LRU_C = 8.0
N_SPLIT = 7
N_CHIPS = 4
N_DEV = 8

ADAM_LR = 0.001
ADAM_B1 = 0.9
ADAM_B2 = 0.999
ADAM_EPS = 1e-08
ADAM_WD = 0.01
ADAM_STEP = 10

HALO_A = 16
HALO_B = 48
ROW_TILE_MM = 912
ROW_TILE_MM_BWD = 432
ROW_TILE_EW = 144
ROW_BLOCK = 16
LANE_CHUNK = 256
COL_TILE = 512
VMEM_LIMIT = 56 * 1024 * 1024
BLOCK_BYTES = 2 * 1024 * 1024

NT_DIMS = (((1,), (1,)), ((), ()))
TN_DIMS = (((0,), (0,)), ((), ()))


def _pick(n, cap, mult):
    best = None
    for d in range(mult, min(n, cap) + 1, mult):
        if n % d == 0:
            best = d
    if best is None:
        raise ValueError(f"no tile for {n} (cap {cap}, multiple of {mult})")
    return best


def _params(*semantics):
    return pltpu.CompilerParams(dimension_semantics=semantics, vmem_limit_bytes=VMEM_LIMIT)


def _whole(shape):
    return pl.BlockSpec(shape, lambda *_: (0,) * len(shape))


def _sigmoid(v):
    return jax.nn.sigmoid(v)


def _dsilu(v, s):
    return s * (1.0 + v * (1.0 - s))


def _log1p(e):
    w = 1.0 + e
    return jnp.where(w == 1.0, e, e * jnp.log(w) / (w - 1.0))


def _softplus(v):
    return jnp.maximum(v, 0.0) + _log1p(jnp.exp(-jnp.abs(v)))


def _neg_expm1(v):
    series = -v * (1.0 + v * 0.5 * (1.0 + v * (1.0 / 3.0) * (1.0 + v * 0.25 * (1.0 + v * 0.2))))
    return jnp.where(v > -0.05, series, 1.0 - jnp.exp(v))


def _row_iota(shape):
    return lax.broadcasted_iota(jnp.int32, shape, 0)


def _in_proj_fwd(h, g, w):
    t, d = h.shape
    nq = w.shape[2]
    tm = _pick(t, ROW_TILE_MM, 16)
    tn = _pick(nq, COL_TILE, 128)
    bps = nq // tn

    def body(h_ref, g_ref, w_ref, z_ref, hn_ref):
        @pl.when(pl.program_id(1) == 0)
        def _():
            v = h_ref[...]
            rinv = lax.rsqrt(jnp.mean(v * v, axis=-1, keepdims=True) + EPS)
            hn_ref[...] = (v * rinv * g_ref[...]).astype(BF16)
        z_ref[...] = jnp.dot(hn_ref[...], w_ref[...], preferred_element_type=F32)

    return pl.pallas_call(
        body, grid=(t // tm, N_CHIPS * bps),
        in_specs=[pl.BlockSpec((tm, d), lambda i, j: (i, 0)), _whole((1, d)),
                  pl.BlockSpec((None, d, tn), lambda i, j: (j // bps, 0, j % bps))],
        out_specs=[pl.BlockSpec((tm, tn), lambda i, j: (i, j)), pl.BlockSpec((tm, d), lambda i, j: (i, 0))],
        out_shape=[jax.ShapeDtypeStruct((t, N_CHIPS * nq), F32), jax.ShapeDtypeStruct((t, d), BF16)],
        compiler_params=_params("parallel", "arbitrary"), name="in_proj_fwd")(h, g, w)


def _conv_a(xext, wa_ref, ba, tt):
    acc = ba
    for k in range(CONV_A):
        acc = acc + wa_ref[k:k + 1, :] * xext[pl.ds(HALO_A - (CONV_A - 1) + k, tt), :]
    return acc


def _gates_a(ca, wr_ref, wi_ref, br, bi, gr_s, gi_s):
    cab = ca.astype(BF16)
    for hh in range(ca.shape[1] // HEAD_DIM):
        sl = slice(hh * HEAD_DIM, (hh + 1) * HEAD_DIM)
        gr_s[:, sl] = jnp.dot(cab[:, sl], wr_ref[hh], preferred_element_type=F32)
        gi_s[:, sl] = jnp.dot(cab[:, sl], wi_ref[hh], preferred_element_type=F32)
    return _sigmoid(gr_s[...] + br), _sigmoid(gi_s[...] + bi)


def _decay_a(r, sp):
    log_a = -LRU_C * r * sp
    a = jnp.exp(log_a)
    mult = jnp.sqrt(jnp.maximum(_neg_expm1(2.0 * log_a), 0.0))
    return a, mult


def _scan_fwd(a_ref, u_ref, out_ref, h0, nblk):
    def blk(b, hprev):
        rows = pl.ds(pl.multiple_of(b * 8, 8), 8)
        ca, cb = a_ref[rows, :], u_ref[rows, :]
        row = _row_iota(ca.shape)
        for dist in (1, 2, 4):
            m = row >= dist
            cb = jnp.where(m, ca * pltpu.roll(cb, dist, 0) + cb, cb)
            ca = jnp.where(m, ca * pltpu.roll(ca, dist, 0), ca)
        hb = ca * hprev + cb
        out_ref[rows, :] = hb
        return hb[7:8, :]
    return lax.fori_loop(0, nblk, blk, h0)


def _scan_rev(a_ref, d_ref, out_ref, g0, nblk):
    def blk(k, g):
        rows = pl.ds(pl.multiple_of((nblk - 1 - k) * 8, 8), 8)
        a, cb = a_ref[rows, :], d_ref[rows, :]
        row = _row_iota(a.shape)
        ca = jnp.where(row == 7, 1.0, pltpu.roll(a, 7, 0))
        for dist in (1, 2, 4):
            m = row < 8 - dist
            cb = jnp.where(m, cb + ca * pltpu.roll(cb, 8 - dist, 0), cb)
            ca = jnp.where(m, ca * pltpu.roll(ca, 8 - dist, 0), ca)
        lam = cb + ca * g
        out_ref[rows, :] = lam
        return a[0:1, :] * lam[0:1, :]
    return lax.fori_loop(0, nblk, blk, g0)


def _branch_a_fwd(z, wa, ba, wr, br, wi, bi, lam):
    t = z.shape[0]
    d = wa.shape[1]
    tt = _pick(t, ROW_TILE_EW, HALO_B)
    hb = tt // HALO_A
    nh = d // HEAD_DIM

    def body(xa_ref, halo_ref, sa_ref, wa_ref, ba_ref, wr_ref, br_ref, wi_ref, bi_ref, lam_ref,
             ya_ref, hl_ref, xext, gr_s, gi_s, a_s, carry):
        i = pl.program_id(0)

        @pl.when(i == 0)
        def _():
            carry[...] = jnp.zeros_like(carry)
        xext[0:HALO_A, :] = jnp.where(i == 0, 0.0, halo_ref[...])
        xext[HALO_A:, :] = xa_ref[...]
        ca = _conv_a(xext, wa_ref, ba_ref[...], tt)
        r, ig = _gates_a(ca, wr_ref, wi_ref, br_ref[...], bi_ref[...], gr_s, gi_s)
        a, mult = _decay_a(r, _softplus(-lam_ref[...]))
        a_s[...] = a
        hl_ref[...] = mult * (ig * ca)
        carry[...] = _scan_fwd(a_s, hl_ref, hl_ref, carry[...], tt // 8)
        sa = sa_ref[...]
        ya_ref[...] = (hl_ref[...] * (sa * _sigmoid(sa))).astype(BF16)

    vec = _whole((1, d))
    return pl.pallas_call(
        body, grid=(t // tt,),
        in_specs=[pl.BlockSpec((tt, d), lambda i: (i, 0)),
                  pl.BlockSpec((HALO_A, d), lambda i: (jnp.maximum(i * hb - 1, 0), 0)),
                  pl.BlockSpec((tt, d), lambda i: (i, 1)),
                  _whole((CONV_A, d)), vec, _whole((nh, HEAD_DIM, HEAD_DIM)), vec,
                  _whole((nh, HEAD_DIM, HEAD_DIM)), vec, vec],
        out_specs=[pl.BlockSpec((tt, d), lambda i: (i, 0)), pl.BlockSpec((tt, d), lambda i: (i, 0))],
        out_shape=[jax.ShapeDtypeStruct((t, d), BF16), jax.ShapeDtypeStruct((t, d), F32)],
        scratch_shapes=[pltpu.VMEM((tt + HALO_A, d), F32), pltpu.VMEM((tt, d), F32), pltpu.VMEM((tt, d), F32),
                        pltpu.VMEM((tt, d), F32), pltpu.VMEM((1, d), F32)],
        compiler_params=_params("arbitrary"), name="branch_a_fwd")(z, z, z, wa, ba, wr, br, wi, bi, lam)


def _row_blocks(n_rows, body):
    def step(b, carry):
        body(pl.multiple_of(b * ROW_BLOCK, ROW_BLOCK))
        return carry
    lax.fori_loop(0, n_rows // ROW_BLOCK, step, 0)


def _rows(r0, offset=0):
    return pl.ds(pl.multiple_of(r0 + offset, ROW_BLOCK), ROW_BLOCK)


def _glu_ext(gext, vb_ref, gb_ref, vbh_ref, gbh_ref, first, tt):
    def halo(r0):
        rows = _rows(r0)
        gext[rows, :] = jnp.where(first, 0.0, vbh_ref[rows, :] * _sigmoid(gbh_ref[rows, :]))
    _row_blocks(HALO_B, halo)

    def tile(r0):
        rows = _rows(r0)
        gext[_rows(r0, HALO_B), :] = vb_ref[rows, :] * _sigmoid(gb_ref[rows, :])
    _row_blocks(tt, tile)


def _shifted(src, dst, cs, n_rows):
    for b in range(1, 8):
        dst[b - 1, 0:n_rows, :] = src[pl.ds(b, n_rows), cs]


def _tap(src, shifted, cs, off, r0, n):
    b = off % 8
    if b == 0:
        return src[pl.ds(off + r0, n), cs]
    return shifted[b - 1, pl.ds(off - b + r0, n), :]


def _conv_rows(tt):
    return tt // 2 if tt % 16 == 0 else tt


def _conv_b_fwd(gext, gs, wb_ref, bb_ref, cb_s, tt, d):
    rc = _conv_rows(tt)
    for c in range(d // LANE_CHUNK):
        cs = slice(c * LANE_CHUNK, (c + 1) * LANE_CHUNK)
        _shifted(gext, gs, cs, tt + HALO_B - 8)
        for r0 in range(0, tt, rc):
            acc = jnp.broadcast_to(bb_ref[:, cs], (rc, LANE_CHUNK))
            for k in range(CONV_B):
                acc = acc + wb_ref[k:k + 1, cs] * _tap(gext, gs, cs, HALO_B - (CONV_B - 1) + k, r0, rc)
            cb_s[r0:r0 + rc, cs] = acc


def _layer_norm_stats(cb):
    mu = jnp.mean(cb, axis=-1, keepdims=True)
    xc = cb - mu
    rstd = lax.rsqrt(jnp.mean(xc * xc, axis=-1, keepdims=True) + EPS)
    return xc * rstd, rstd


def _branch_b_fwd(z, wb, bb, lg, lb):
    t = z.shape[0]
    d = wb.shape[1]
    tt = _pick(t, ROW_TILE_EW, HALO_B)
    hb = tt // HALO_B

    def body(vb_ref, gb_ref, vbh_ref, gbh_ref, sb_ref, wb_ref, bb_ref, lg_ref, lb_ref, yb_ref, gext, gs, cb_s):
        _glu_ext(gext, vb_ref, gb_ref, vbh_ref, gbh_ref, pl.program_id(0) == 0, tt)
        _conv_b_fwd(gext, gs, wb_ref, bb_ref, cb_s, tt, d)

        def finish(r0):
            rows = _rows(r0)
            xhat, _ = _layer_norm_stats(cb_s[rows, :])
            ln = xhat * lg_ref[...] + lb_ref[...]
            sb = sb_ref[rows, :]
            yb_ref[rows, :] = (ln * _sigmoid(ln) * (sb * _sigmoid(sb))).astype(BF16)
        _row_blocks(tt, finish)

    vec = _whole((1, d))
    tile = lambda s: pl.BlockSpec((tt, d), lambda i: (i, s))
    halo = lambda s: pl.BlockSpec((HALO_B, d), lambda i: (jnp.maximum(i * hb - 1, 0), s))
    return pl.pallas_call(
        body, grid=(t // tt,),
        in_specs=[tile(2), tile(3), halo(2), halo(3), tile(4), _whole((CONV_B, d)), vec, vec, vec],
        out_specs=pl.BlockSpec((tt, d), lambda i: (i, 0)),
        out_shape=jax.ShapeDtypeStruct((t, d), BF16),
        scratch_shapes=[pltpu.VMEM((tt + HALO_B, d), F32), pltpu.VMEM((7, tt + HALO_B - 8, LANE_CHUNK), F32),
                        pltpu.VMEM((tt, d), F32)],
        compiler_params=_params("parallel"), name="branch_b_fwd")(z, z, z, z, z, wb, bb, lg, lb)


def _merge_fwd(ya, yb, z, wpa, wpb):
    t, d = ya.shape
    tm = _pick(t, ROW_TILE_MM, 16)
    tn = _pick(d, COL_TILE, 128)
    nb = d // tn

    def body(ya_ref, yb_ref, ma_ref, mb_ref, wpa_ref, wpb_ref, mg_ref, pa_ref, pb_ref):
        pa = jnp.dot(ya_ref[...], wpa_ref[...], preferred_element_type=F32)
        pb = jnp.dot(yb_ref[...], wpb_ref[...], preferred_element_type=F32)
        mg_ref[...] = (_sigmoid(ma_ref[...]) * pa + _sigmoid(mb_ref[...]) * pb).astype(BF16)
        pa_ref[...] = pa.astype(BF16)
        pb_ref[...] = pb.astype(BF16)

    rows = pl.BlockSpec((tm, d), lambda i, j: (i, 0))
    wcol = pl.BlockSpec((d, tn), lambda i, j: (0, j))
    outb = pl.BlockSpec((tm, tn), lambda i, j: (i, j))
    return pl.pallas_call(
        body, grid=(t // tm, nb),
        in_specs=[rows, rows, pl.BlockSpec((tm, tn), lambda i, j: (i, 5 * nb + j)),
                  pl.BlockSpec((tm, tn), lambda i, j: (i, 6 * nb + j)), wcol, wcol],
        out_specs=[outb, outb, outb],
        out_shape=[jax.ShapeDtypeStruct((t, d), BF16)] * 3,
        compiler_params=_params("parallel", "arbitrary"), name="merge_fwd")(ya, yb, z, z, wpa, wpb)


def _out_fwd(h, merged, wout):
    t, d = h.shape
    tm = _pick(t, ROW_TILE_MM, 16)
    tn = _pick(d, COL_TILE, 128)

    def body(h_ref, mg_ref, w_ref, o_ref):
        o_ref[...] = h_ref[...] + jnp.dot(mg_ref[...], w_ref[...], preferred_element_type=F32)

    return pl.pallas_call(
        body, grid=(t // tm, d // tn),
        in_specs=[pl.BlockSpec((tm, tn), lambda i, j: (i, j)), pl.BlockSpec((tm, d), lambda i, j: (i, 0)),
                  pl.BlockSpec((d, tn), lambda i, j: (0, j))],
        out_specs=pl.BlockSpec((tm, tn), lambda i, j: (i, j)),
        out_shape=jax.ShapeDtypeStruct((t, d), F32),
        compiler_params=_params("parallel", "arbitrary"), name="out_fwd")(h, merged, wout)


def _loss_head(h, g, target):
    t, d = h.shape
    tt = _pick(t, ROW_TILE_EW, 16)

    def body(h_ref, g_ref, tg_ref, loss_ref, dh_ref, dg_ref):
        i = pl.program_id(0)

        @pl.when(i == 0)
        def _():
            loss_ref[...] = jnp.zeros_like(loss_ref)
            dg_ref[...] = jnp.zeros_like(dg_ref)
        v = h_ref[...]
        g = g_ref[...]
        rinv = lax.rsqrt(jnp.mean(v * v, axis=-1, keepdims=True) + EPS)
        xh = v * rinv
        valid = (i * tt + _row_iota(v.shape)) >= N_META
        diff = jnp.where(valid, xh * g - tg_ref[...], 0.0)
        loss_ref[...] += (0.5 / d) * jnp.sum(diff * diff, axis=0, keepdims=True)
        dy = diff * (1.0 / d)
        dg_ref[...] += jnp.sum(dy * xh, axis=0, keepdims=True)
        dxh = dy * g
        dh_ref[...] = rinv * (dxh - xh * jnp.mean(dxh * xh, axis=-1, keepdims=True))

    tile = pl.BlockSpec((tt, d), lambda i: (i, 0))
    vec = _whole((1, d))
    return pl.pallas_call(
        body, grid=(t // tt,), in_specs=[tile, vec, tile], out_specs=[vec, tile, vec],
        out_shape=[jax.ShapeDtypeStruct((1, d), F32), jax.ShapeDtypeStruct((t, d), F32),
                   jax.ShapeDtypeStruct((1, d), F32)],
        compiler_params=_params("arbitrary"), name="loss_head")(h, g, target)


def _dmerged_bwd(dho, wout, z, pa, pb):
    t, d = dho.shape
    tm = _pick(t, ROW_TILE_MM, 16)
    tn = _pick(d, COL_TILE, 128)
    nb = d // tn

    def body(dho_ref, w_ref, ma_ref, mb_ref, pa_ref, pb_ref, dpa_ref, dpb_ref, dma_ref, dmb_ref, dho_s):
        @pl.when(pl.program_id(1) == 0)
        def _():
            dho_s[...] = dho_ref[...].astype(BF16)
        dm = lax.dot_general(dho_s[...], w_ref[...], NT_DIMS, preferred_element_type=F32)
        sa = _sigmoid(ma_ref[...])
        sb = _sigmoid(mb_ref[...])
        dpa_ref[...] = (dm * sa).astype(BF16)
        dpb_ref[...] = (dm * sb).astype(BF16)
        dma_ref[...] = (dm * pa_ref[...].astype(F32) * (sa * (1.0 - sa))).astype(BF16)
        dmb_ref[...] = (dm * pb_ref[...].astype(F32) * (sb * (1.0 - sb))).astype(BF16)

    blk = pl.BlockSpec((tm, tn), lambda i, j: (i, j))
    return pl.pallas_call(
        body, grid=(t // tm, nb),
        in_specs=[pl.BlockSpec((tm, d), lambda i, j: (i, 0)), pl.BlockSpec((tn, d), lambda i, j: (j, 0)),
                  pl.BlockSpec((tm, tn), lambda i, j: (i, 5 * nb + j)),
                  pl.BlockSpec((tm, tn), lambda i, j: (i, 6 * nb + j)), blk, blk],
        out_specs=[blk, blk, blk, blk],
        out_shape=[jax.ShapeDtypeStruct((t, d), BF16)] * 4,
        scratch_shapes=[pltpu.VMEM((tm, d), BF16)],
        compiler_params=_params("parallel", "arbitrary"), name="dmerged_bwd")(dho, wout, z, z, pa, pb)


def _matmul_nt(a, w, name):
    t, k = a.shape
    n = w.shape[0]
    tm = _pick(t, ROW_TILE_MM, 16)
    tn = _pick(n, COL_TILE, 128)

    def body(a_ref, w_ref, o_ref):
        o_ref[...] = lax.dot_general(a_ref[...], w_ref[...], NT_DIMS, preferred_element_type=F32)

    return pl.pallas_call(
        body, grid=(t // tm, n // tn),
        in_specs=[pl.BlockSpec((tm, k), lambda i, j: (i, 0)), pl.BlockSpec((tn, k), lambda i, j: (j, 0))],
        out_specs=pl.BlockSpec((tm, tn), lambda i, j: (i, j)),
        out_shape=jax.ShapeDtypeStruct((t, n), F32),
        compiler_params=_params("parallel", "arbitrary"), name=name)(a, w)


def _matmul_tn(a, b, name):
    t, m = a.shape
    n = b.shape[1]
    tk = _pick(t, ROW_TILE_MM, 16)
    tn = _pick(n, COL_TILE, 128)

    def body(a_ref, b_ref, o_ref):
        @pl.when(pl.program_id(1) == 0)
        def _():
            o_ref[...] = jnp.zeros_like(o_ref)
        o_ref[...] += lax.dot_general(a_ref[...].astype(BF16), b_ref[...].astype(BF16), TN_DIMS,
                                      preferred_element_type=F32)

    return pl.pallas_call(
        body, grid=(n // tn, t // tk),
        in_specs=[pl.BlockSpec((tk, m), lambda j, s: (s, 0)), pl.BlockSpec((tk, tn), lambda j, s: (s, j))],
        out_specs=pl.BlockSpec((m, tn), lambda j, s: (0, j)),
        out_shape=jax.ShapeDtypeStruct((m, n), F32),
        compiler_params=_params("parallel", "arbitrary"), name=name)(a, b)


def _dwin_bwd(hn, dzs):
    t, d = hn.shape
    nq = N_SPLIT * d // N_CHIPS
    tk = _pick(t, ROW_TILE_MM, 16)
    tn = _pick(nq, COL_TILE, 128)
    bps = nq // tn
    bpz = d // tn

    def body(hn_ref, *rest):
        dz_refs, o_ref = rest[:N_SPLIT], rest[N_SPLIT]
        j = pl.program_id(0)

        @pl.when(pl.program_id(1) == 0)
        def _():
            o_ref[...] = jnp.zeros_like(o_ref)
        for s in range(N_SPLIT):
            @pl.when(j // bpz == s)
            def _(s=s):
                o_ref[...] += lax.dot_general(hn_ref[...], dz_refs[s][...], TN_DIMS, preferred_element_type=F32)

    def dz_spec(s):
        def imap(j, r):
            on = (j // bpz) == s
            return (jnp.where(on, r, 0), jnp.where(on, j % bpz, 0))
        return pl.BlockSpec((tk, tn), imap)

    return pl.pallas_call(
        body, grid=(N_SPLIT * bpz, t // tk),
        in_specs=[pl.BlockSpec((tk, d), lambda j, r: (r, 0))] + [dz_spec(s) for s in range(N_SPLIT)],
        out_specs=pl.BlockSpec((None, d, tn), lambda j, r: (j // bps, 0, j % bps)),
        out_shape=jax.ShapeDtypeStruct((N_CHIPS, d, nq), F32),
        compiler_params=_params("parallel", "arbitrary"), name="dwin_bwd")(hn, *dzs)


def _din_bwd(dzs, w, h, g, dho):
    t, d = h.shape
    nq = w.shape[2]
    tm = _pick(t, ROW_TILE_MM_BWD, 16)
    tk = _pick(nq, COL_TILE, 128)
    bps = nq // tk
    bpz = d // tk
    nk = N_SPLIT * bpz

    def body(*refs):
        dz_refs = refs[:N_SPLIT]
        w_ref, h_ref, g_ref, dho_ref, dh_ref, dg_ref, acc = refs[N_SPLIT:]
        i, k = pl.program_id(0), pl.program_id(1)

        @pl.when(k == 0)
        def _():
            acc[...] = jnp.zeros_like(acc)

        @pl.when((i == 0) & (k == 0))
        def _():
            dg_ref[...] = jnp.zeros_like(dg_ref)
        for s in range(N_SPLIT):
            @pl.when(k // bpz == s)
            def _(s=s):
                acc[...] += lax.dot_general(dz_refs[s][...], w_ref[...], NT_DIMS, preferred_element_type=F32)

        @pl.when(k == nk - 1)
        def _():
            v = h_ref[...]
            rinv = lax.rsqrt(jnp.mean(v * v, axis=-1, keepdims=True) + EPS)
            xh = v * rinv
            dhn = acc[...]
            dg_ref[...] += jnp.sum(dhn * xh, axis=0, keepdims=True)
            dxh = dhn * g_ref[...]
            dh_ref[...] = dho_ref[...] + rinv * (dxh - xh * jnp.mean(dxh * xh, axis=-1, keepdims=True))

    def dz_spec(s):
        return pl.BlockSpec((tm, tk), lambda i, k: (i, jnp.clip(k - s * bpz, 0, bpz - 1)))

    rows = pl.BlockSpec((tm, d), lambda i, k: (i, 0))
    vec = _whole((1, d))
    return pl.pallas_call(
        body, grid=(t // tm, nk),
        in_specs=[dz_spec(s) for s in range(N_SPLIT)]
        + [pl.BlockSpec((None, d, tk), lambda i, k: (k // bps, 0, k % bps)), rows, vec, rows],
        out_specs=[rows, vec],
        out_shape=[jax.ShapeDtypeStruct((t, d), F32), jax.ShapeDtypeStruct((1, d), F32)],
        scratch_shapes=[pltpu.VMEM((tm, d), F32)],
        compiler_params=_params("arbitrary", "arbitrary"), name="din_bwd")(*dzs, w, h, g, dho)


def _branch_b_bwd(z, dyb, wb, bb, lg, lb):
    t = z.shape[0]
    d = wb.shape[1]
    tt = _pick(t, ROW_TILE_EW, HALO_B)
    hb = tt // HALO_B
    nt = t // tt
    rc = _conv_rows(tt)

    def body(vb_ref, gb_ref, vbh_ref, gbh_ref, sb_ref, dyb_ref, wb_ref, bb_ref, lg_ref, lb_ref,
             dvb_ref, dgb_ref, dsb_ref, dwb_ref, dbb_ref, dlg_ref, dlb_ref,
             gext, dext, gs, ds, cb_s, carry, dwb8, vec8):
        i = pl.program_id(0)

        @pl.when(i == 0)
        def _():
            for ref in (carry, dwb8, vec8):
                ref[...] = jnp.zeros_like(ref)
        _glu_ext(gext, vb_ref, gb_ref, vbh_ref, gbh_ref, i == nt - 1, tt)
        _conv_b_fwd(gext, gs, wb_ref, bb_ref, cb_s, tt, d)

        def norm_bwd(r0):
            rows = _rows(r0)
            xhat, rstd = _layer_norm_stats(cb_s[rows, :])
            lg = lg_ref[...]
            ln = xhat * lg + lb_ref[...]
            sl = _sigmoid(ln)
            sb = sb_ref[rows, :]
            ss = _sigmoid(sb)
            dyb = dyb_ref[rows, :]
            dln = dyb * (sb * ss) * _dsilu(ln, sl)
            dsb_ref[rows, :] = (dyb * (ln * sl) * _dsilu(sb, ss)).astype(BF16)
            dxh = dln * lg
            dcb = rstd * (dxh - jnp.mean(dxh, axis=-1, keepdims=True)
                          - xhat * jnp.mean(dxh * xhat, axis=-1, keepdims=True))
            dext[rows, :] = dcb
            for n, val in enumerate((dln * xhat, dln, dcb)):
                vec8[n] += val[0:8, :] + val[8:16, :]
        _row_blocks(tt, norm_bwd)
        dext[tt:, :] = carry[...]
        carry[...] = dext[0:HALO_B, :]
        for c in range(d // LANE_CHUNK):
            cs = slice(c * LANE_CHUNK, (c + 1) * LANE_CHUNK)
            _shifted(gext, gs, cs, tt + HALO_B - 8)
            _shifted(dext, ds, cs, tt + HALO_B - 8)
            for r0 in range(0, tt, rc):
                dcb = dext[r0:r0 + rc, cs]
                acc = jnp.zeros((rc, LANE_CHUNK), F32)
                for k in range(CONV_B):
                    acc = acc + wb_ref[k:k + 1, cs] * _tap(dext, ds, cs, CONV_B - 1 - k, r0, rc)
                    prod = dcb * _tap(gext, gs, cs, HALO_B - (CONV_B - 1) + k, r0, rc)
                    part = prod[0:8, :]
                    for j in range(8, rc, 8):
                        part = part + prod[j:j + 8, :]
                    dwb8[k, :, cs] += part
                cb_s[r0:r0 + rc, cs] = acc

        def glu_bwd(r0):
            rows = _rows(r0)
            dglu = cb_s[rows, :]
            sg = _sigmoid(gb_ref[rows, :])
            dvb_ref[rows, :] = (dglu * sg).astype(BF16)
            dgb_ref[rows, :] = (dglu * vb_ref[rows, :] * (sg * (1.0 - sg))).astype(BF16)
        _row_blocks(tt, glu_bwd)

        @pl.when(i == nt - 1)
        def _():
            dwb_ref[...] = jnp.sum(dwb8[...], axis=1)
            dlg_ref[...] = jnp.sum(vec8[0], axis=0, keepdims=True)
            dlb_ref[...] = jnp.sum(vec8[1], axis=0, keepdims=True)
            dbb_ref[...] = jnp.sum(vec8[2], axis=0, keepdims=True)

    rev = lambda i: nt - 1 - i
    vec = _whole((1, d))
    tile = lambda s: pl.BlockSpec((tt, d), lambda i: (rev(i), s))
    halo = lambda s: pl.BlockSpec((HALO_B, d), lambda i: (jnp.maximum(rev(i) * hb - 1, 0), s))
    otile = pl.BlockSpec((tt, d), lambda i: (rev(i), 0))
    ext = pltpu.VMEM((tt + HALO_B, d), F32)
    shifts = pltpu.VMEM((7, tt + HALO_B - 8, LANE_CHUNK), F32)
    return pl.pallas_call(
        body, grid=(nt,),
        in_specs=[tile(2), tile(3), halo(2), halo(3), tile(4), otile, _whole((CONV_B, d)), vec, vec, vec],
        out_specs=[otile, otile, otile, _whole((CONV_B, d)), vec, vec, vec],
        out_shape=[jax.ShapeDtypeStruct((t, d), BF16)] * 3
        + [jax.ShapeDtypeStruct((CONV_B, d), F32)] + [jax.ShapeDtypeStruct((1, d), F32)] * 3,
        scratch_shapes=[ext, ext, shifts, shifts, pltpu.VMEM((tt, d), F32), pltpu.VMEM((HALO_B, d), F32),
                        pltpu.VMEM((CONV_B, 8, d), F32), pltpu.VMEM((3, 8, d), F32)],
        compiler_params=_params("arbitrary"), name="branch_b_bwd")(z, z, z, z, z, dyb, wb, bb, lg, lb)


def _branch_a_bwd(z, hl, dya, wa, ba, wr, br, wi, bi, lam):
    t = z.shape[0]
    d = wa.shape[1]
    tt = _pick(t, ROW_TILE_EW, HALO_B)
    hb = tt // HALO_A
    nt = t // tt
    nh = d // HEAD_DIM

    def body(xa_ref, xah_ref, sa_ref, hl_ref, hlh_ref, dya_ref, wa_ref, ba_ref, wr_ref, br_ref, wi_ref, bi_ref,
             lam_ref, dxa_ref, dsa_ref, dwa_ref, dba_ref, dbr_ref, dbi_ref, dlam_ref, dwr_ref, dwi_ref,
             xext, hext, dext, gr_s, gi_s, a_s, lam_s, gcarry, dcarry):
        i = pl.program_id(0)
        first = i == nt - 1

        @pl.when(i == 0)
        def _():
            for ref in (dwa_ref, dba_ref, dbr_ref, dbi_ref, dlam_ref, dwr_ref, dwi_ref, gcarry, dcarry):
                ref[...] = jnp.zeros_like(ref)
        xext[0:HALO_A, :] = jnp.where(first, 0.0, xah_ref[...])
        xext[HALO_A:, :] = xa_ref[...]
        hext[0:HALO_A, :] = jnp.where(first, 0.0, hlh_ref[...])
        hext[HALO_A:, :] = hl_ref[...]
        ca = _conv_a(xext, wa_ref, ba_ref[...], tt)
        r, ig = _gates_a(ca, wr_ref, wi_ref, br_ref[...], bi_ref[...], gr_s, gi_s)
        lam = lam_ref[...]
        sp = _softplus(-lam)
        a, mult = _decay_a(r, sp)
        sa = sa_ref[...]
        ss = _sigmoid(sa)
        dya = dya_ref[...]
        dsa_ref[...] = (dya * hl_ref[...] * _dsilu(sa, ss)).astype(BF16)
        a_s[...] = a
        lam_s[...] = dya * (sa * ss)
        gcarry[...] = _scan_rev(a_s, lam_s, lam_s, gcarry[...], tt // 8)
        du = lam_s[...]
        da = du * hext[pl.ds(HALO_A - 1, tt), :]
        dig = du * mult * ca
        dca = du * mult * ig
        dlog_a = da * a - (du * ig * ca) * jnp.where(mult > 0.0, a * a / mult, 0.0)
        dgr = (dlog_a * (-LRU_C * sp)) * r * (1.0 - r)
        dgi = dig * ig * (1.0 - ig)
        dlam_ref[...] += jnp.sum(dlog_a * r, axis=0, keepdims=True) * (LRU_C * _sigmoid(-lam))
        dbr_ref[...] += jnp.sum(dgr, axis=0, keepdims=True)
        dbi_ref[...] += jnp.sum(dgi, axis=0, keepdims=True)
        cab = ca.astype(BF16)
        dgrb = dgr.astype(BF16)
        dgib = dgi.astype(BF16)
        for hh in range(nh):
            sl = slice(hh * HEAD_DIM, (hh + 1) * HEAD_DIM)
            dwr_ref[hh] += lax.dot_general(cab[:, sl], dgrb[:, sl], TN_DIMS, preferred_element_type=F32)
            dwi_ref[hh] += lax.dot_general(cab[:, sl], dgib[:, sl], TN_DIMS, preferred_element_type=F32)
            gr_s[:, sl] = (lax.dot_general(dgrb[:, sl], wr_ref[hh], NT_DIMS, preferred_element_type=F32)
                           + lax.dot_general(dgib[:, sl], wi_ref[hh], NT_DIMS, preferred_element_type=F32))
        dca = dca + gr_s[...]
        dba_ref[...] += jnp.sum(dca, axis=0, keepdims=True)
        dext[0:tt, :] = dca
        dext[tt:, :] = dcarry[...]
        dcarry[...] = dca[0:HALO_A, :]
        dxa = jnp.zeros_like(dca)
        for k in range(CONV_A):
            off = HALO_A - (CONV_A - 1) + k
            dwa_ref[k:k + 1, :] += jnp.sum(dext[0:tt, :] * xext[pl.ds(off, tt), :], axis=0, keepdims=True)
            dxa = dxa + wa_ref[k:k + 1, :] * dext[pl.ds(CONV_A - 1 - k, tt), :]
        dxa_ref[...] = dxa.astype(BF16)

    rev = lambda i: nt - 1 - i
    vec = _whole((1, d))
    hw = _whole((nh, HEAD_DIM, HEAD_DIM))
    tile = lambda s: pl.BlockSpec((tt, d), lambda i: (rev(i), s))
    halo = pl.BlockSpec((HALO_A, d), lambda i: (jnp.maximum(rev(i) * hb - 1, 0), 0))
    big = pltpu.VMEM((tt + HALO_A, d), F32)
    full = pltpu.VMEM((tt, d), F32)
    return pl.pallas_call(
        body, grid=(nt,),
        in_specs=[tile(0), halo, tile(1), tile(0), halo, tile(0), _whole((CONV_A, d)), vec, hw, vec, hw, vec, vec],
        out_specs=[tile(0), tile(0), _whole((CONV_A, d)), vec, vec, vec, vec, hw, hw],
        out_shape=[jax.ShapeDtypeStruct((t, d), BF16)] * 2 + [jax.ShapeDtypeStruct((CONV_A, d), F32)]
        + [jax.ShapeDtypeStruct((1, d), F32)] * 4 + [jax.ShapeDtypeStruct((nh, HEAD_DIM, HEAD_DIM), F32)] * 2,
        scratch_shapes=[big, big, big, full, full, full, full, pltpu.VMEM((1, d), F32), pltpu.VMEM((HALO_A, d), F32)],
        compiler_params=_params("arbitrary"), name="branch_a_bwd")(z, z, z, hl, hl, dya, wa, ba, wr, br, wi, bi, lam)


def _rows_block(rows, cols, n_arrays=1):
    cap = max(8, BLOCK_BYTES // (4 * cols * n_arrays))
    return rows if rows <= cap else _pick(rows, cap, 8)


def _sum_halves(g, xo, c_idx):
    nq, r, c = g.shape
    rh = r // 2
    tr = _rows_block(rh, c)
    nb = rh // tr

    def body(c_ref, g_ref, x_ref, o_ref):
        o_ref[...] = (g_ref[...] + x_ref[...]).astype(BF16)

    grid_spec = pltpu.PrefetchScalarGridSpec(
        num_scalar_prefetch=1, grid=(nq, nb),
        in_specs=[pl.BlockSpec((None, tr, c), lambda q, i, cr: (q, cr[0] * nb + i, 0)),
                  pl.BlockSpec((None, tr, c), lambda q, i, cr: (q, i, 0))],
        out_specs=pl.BlockSpec((None, tr, c), lambda q, i, cr: (q, i, 0)))
    return pl.pallas_call(
        body, grid_spec=grid_spec, out_shape=jax.ShapeDtypeStruct((nq, rh, c), BF16),
        compiler_params=_params("parallel", "parallel"), name="sum_halves")(c_idx, g, xo)


def _sum_slots(y, name):
    ns, r, c = y.shape
    tr = _rows_block(r, c, ns)

    def body(y_ref, o_ref):
        acc = y_ref[0].astype(F32)
        for s in range(1, ns):
            acc = acc + y_ref[s].astype(F32)
        o_ref[...] = acc

    return pl.pallas_call(
        body, grid=(r // tr,), in_specs=[pl.BlockSpec((ns, tr, c), lambda i: (0, i, 0))],
        out_specs=pl.BlockSpec((tr, c), lambda i: (i, 0)), out_shape=jax.ShapeDtypeStruct((r, c), F32),
        compiler_params=_params("parallel"), name=name)(y)


def _adamw(w, g, m, v, name):
    shape = w.shape
    c = shape[-1]
    r = w.size // c
    tr = _rows_block(r, c)
    c1 = 1.0 / (1.0 - ADAM_B1 ** ADAM_STEP)
    c2 = 1.0 / (1.0 - ADAM_B2 ** ADAM_STEP)

    def body(w_ref, g_ref, m_ref, v_ref, go_ref, d_ref, mo_ref, vo_ref):
        gv = g_ref[...]
        mn = ADAM_B1 * m_ref[...] + (1.0 - ADAM_B1) * gv
        vn = ADAM_B2 * v_ref[...] + (1.0 - ADAM_B2) * (gv * gv)
        go_ref[...] = gv
        mo_ref[...] = mn
        vo_ref[...] = vn
        d_ref[...] = -ADAM_LR * ((mn * c1) / (jnp.sqrt(vn * c2) + ADAM_EPS) + ADAM_WD * w_ref[...])

    blk = pl.BlockSpec((tr, c), lambda i: (i, 0))
    outs = pl.pallas_call(
        body, grid=(r // tr,), in_specs=[blk] * 4, out_specs=[blk] * 4,
        out_shape=[jax.ShapeDtypeStruct((r, c), F32)] * 4,
        compiler_params=_params("parallel"), name=name)(*(a.reshape(r, c) for a in (w, g, m, v)))
    return tuple(o.reshape(shape) for o in outs)


def _adamw_layers(w, gs, m, v, name):
    depth, r, c = w.shape
    tr = _rows_block(r, c)
    c1 = 1.0 / (1.0 - ADAM_B1 ** ADAM_STEP)
    c2 = 1.0 / (1.0 - ADAM_B2 ** ADAM_STEP)

    def body(w_ref, m_ref, v_ref, *rest):
        g_refs = rest[:depth]
        go_ref, d_ref, mo_ref, vo_ref = rest[depth:]
        layer = pl.program_id(0)
        for s in range(depth):
            @pl.when(layer == s)
            def _(s=s):
                gv = g_refs[s][...]
                mn = ADAM_B1 * m_ref[...] + (1.0 - ADAM_B1) * gv
                vn = ADAM_B2 * v_ref[...] + (1.0 - ADAM_B2) * (gv * gv)
                go_ref[...] = gv
                mo_ref[...] = mn
                vo_ref[...] = vn
                d_ref[...] = -ADAM_LR * ((mn * c1) / (jnp.sqrt(vn * c2) + ADAM_EPS) + ADAM_WD * w_ref[...])

    stacked = pl.BlockSpec((None, tr, c), lambda l, i: (l, i, 0))
    g_spec = lambda s: pl.BlockSpec((tr, c), lambda l, i: (jnp.where(l == s, i, 0), 0))
    return pl.pallas_call(
        body, grid=(depth, r // tr),
        in_specs=[stacked] * 3 + [g_spec(s) for s in range(depth)], out_specs=[stacked] * 4,
        out_shape=[jax.ShapeDtypeStruct(w.shape, F32)] * 4,
        compiler_params=_params("arbitrary", "arbitrary"), name=name)(w, m, v, *gs)


HBM_SPEC = pl.BlockSpec(memory_space=pltpu.HBM)


def _place():
    return lax.axis_index("x"), lax.axis_index("y"), lax.axis_index("c")


def _other_chips(x, y):
    return [(1 - x, y), (x, 1 - y), (1 - x, 1 - y)]


def _half(ref, which):
    rows = ref.shape[0] // 2
    return ref.at[pl.ds(which * rows, rows)]


def _gather_chips(arrays, name):
    n = len(arrays)

    def body(*refs):
        ins, outs = refs[:n], refs[n:2 * n]
        send1, recv1, send2, recv2, lsem = refs[2 * n:]
        x, y, c = _place()
        me = 2 * x + y
        chips = _other_chips(x, y)
        sibling = (x, y, 1 - c)
        local = [pltpu.make_async_copy(ins[a], outs[a].at[me], lsem.at[a]) for a in range(n)]
        for cp in local:
            cp.start()

        def over_ici(a, j):
            return pltpu.make_async_remote_copy(
                src_ref=_half(ins[a], c), dst_ref=_half(outs[a].at[me], c),
                send_sem=send1.at[a, j], recv_sem=recv1.at[a, j], device_id=(*chips[j], c), device_id_type=MESH)

        def landed(a, j, which):
            q = 2 * chips[j][0] + chips[j][1]
            return _half(outs[a].at[q], which)

        def to_sibling(a, j):
            return pltpu.make_async_remote_copy(
                src_ref=landed(a, j, c), dst_ref=landed(a, j, c),
                send_sem=send2.at[a, j], recv_sem=recv2.at[a, j], device_id=sibling, device_id_type=MESH)

        sent = [over_ici(a, j) for a in range(n) for j in range(3)]
        for cp in sent:
            cp.start()
        for a in range(n):
            for j in range(3):
                pltpu.make_async_remote_copy(
                    src_ref=_half(ins[a], c), dst_ref=landed(a, j, c), send_sem=send1.at[a, j],
                    recv_sem=recv1.at[a, j], device_id=(*chips[j], c), device_id_type=MESH).wait_recv()
                cp = to_sibling(a, j)
                cp.start()
                sent.append(cp)
        for a in range(n):
            for j in range(3):
                pltpu.make_async_remote_copy(
                    src_ref=landed(a, j, 1 - c), dst_ref=landed(a, j, 1 - c), send_sem=send2.at[a, j],
                    recv_sem=recv2.at[a, j], device_id=sibling, device_id_type=MESH).wait_recv()
        for cp in sent:
            cp.wait_send()
        for cp in local:
            cp.wait()

    sems = pltpu.SemaphoreType.DMA((n, 3))
    return pl.pallas_call(
        body, in_specs=[HBM_SPEC] * n, out_specs=[HBM_SPEC] * n,
        out_shape=[jax.ShapeDtypeStruct((N_CHIPS,) + a.shape, a.dtype) for a in arrays],
        scratch_shapes=[sems, sems, sems, sems, pltpu.SemaphoreType.DMA((n,))], name=name)(*arrays)


def _swap_halves(grads, name):
    n = len(grads)

    def body(*refs):
        ins, outs = refs[:n], refs[n:2 * n]
        send, recv = refs[2 * n:]
        x, y, c = _place()
        copies = []
        for a in range(n):
            rows = ins[a].shape[1] // 2
            copies.append(pltpu.make_async_remote_copy(
                src_ref=ins[a].at[:, pl.ds((1 - c) * rows, rows)], dst_ref=outs[a],
                send_sem=send.at[a], recv_sem=recv.at[a], device_id=(x, y, 1 - c), device_id_type=MESH))
        for cp in copies:
            cp.start()
        for cp in copies:
            cp.wait()

    sems = pltpu.SemaphoreType.DMA((n,))
    return pl.pallas_call(
        body, in_specs=[HBM_SPEC] * n, out_specs=[HBM_SPEC] * n,
        out_shape=[jax.ShapeDtypeStruct((g.shape[0], g.shape[1] // 2) + g.shape[2:], g.dtype) for g in grads],
        scratch_shapes=[sems, sems], name=name)(*grads)


def _scatter_chips(parts, name):
    n = len(parts)

    def body(*refs):
        ins, outs = refs[:n], refs[n:2 * n]
        send, recv, lsem = refs[2 * n:]
        x, y, c = _place()
        me = 2 * x + y
        chips = _other_chips(x, y)
        local = [pltpu.make_async_copy(ins[a].at[me], outs[a].at[me], lsem.at[a]) for a in range(n)]
        for cp in local:
            cp.start()
        copies = []
        for a in range(n):
            for j in range(3):
                q = 2 * chips[j][0] + chips[j][1]
                copies.append(pltpu.make_async_remote_copy(
                    src_ref=ins[a].at[q], dst_ref=outs[a].at[me], send_sem=send.at[a, j], recv_sem=recv.at[a, j],
                    device_id=(*chips[j], c), device_id_type=MESH))
        for cp in copies:
            cp.start()
        for a in range(n):
            for j in range(3):
                q = 2 * chips[j][0] + chips[j][1]
                pltpu.make_async_remote_copy(
                    src_ref=ins[a].at[q], dst_ref=outs[a].at[q], send_sem=send.at[a, j], recv_sem=recv.at[a, j],
                    device_id=(*chips[j], c), device_id_type=MESH).wait_recv()
        for cp in copies:
            cp.wait_send()
        for cp in local:
            cp.wait()

    sems = pltpu.SemaphoreType.DMA((n, 3))
    return pl.pallas_call(
        body, in_specs=[HBM_SPEC] * n, out_specs=[HBM_SPEC] * n,
        out_shape=[jax.ShapeDtypeStruct(p.shape, p.dtype) for p in parts],
        scratch_shapes=[sems, sems, pltpu.SemaphoreType.DMA((n,))], name=name)(*parts)


def _join_halves(halves, name):
    n = len(halves)

    def body(*refs):
        ins, outs = refs[:n], refs[n:2 * n]
        send, recv, lsem = refs[2 * n:]
        x, y, c = _place()
        local, copies = [], []
        for a in range(n):
            dst = _half(outs[a], c)
            local.append(pltpu.make_async_copy(ins[a], dst, lsem.at[a]))
            copies.append(pltpu.make_async_remote_copy(
                src_ref=ins[a], dst_ref=dst, send_sem=send.at[a], recv_sem=recv.at[a],
                device_id=(x, y, 1 - c), device_id_type=MESH))
        for cp in local + copies:
            cp.start()
        for a in range(n):
            pltpu.make_async_remote_copy(
                src_ref=ins[a], dst_ref=_half(outs[a], 1 - c), send_sem=send.at[a], recv_sem=recv.at[a],
                device_id=(x, y, 1 - c), device_id_type=MESH).wait_recv()
        for cp in copies:
            cp.wait_send()
        for cp in local:
            cp.wait()

    sems = pltpu.SemaphoreType.DMA((n,))
    return pl.pallas_call(
        body, in_specs=[HBM_SPEC] * n, out_specs=[HBM_SPEC] * n,
        out_shape=[jax.ShapeDtypeStruct((2 * h.shape[0],) + h.shape[1:], h.dtype) for h in halves],
        scratch_shapes=[sems, sems, sems], name=name)(*halves)


def _gather_all(arrays, name):
    n = len(arrays)

    def body(*refs):
        ins, outs = refs[:n], refs[n:2 * n]
        send, recv, lsem = refs[2 * n:]
        x, y, c = _place()
        chips = _other_chips(x, y)
        sibling = (x, y, 1 - c)

        def slot(a, px, py, pc):
            return outs[a].at[4 * px + 2 * py + pc]

        def copy(a, k, block, to, src=None):
            return pltpu.make_async_remote_copy(
                src_ref=slot(a, *block) if src is None else src, dst_ref=slot(a, *block),
                send_sem=send.at[a, k], recv_sem=recv.at[a, k], device_id=to, device_id_type=MESH)

        local = [pltpu.make_async_copy(ins[a], slot(a, x, y, c), lsem.at[a]) for a in range(n)]
        for cp in local:
            cp.start()
        sent = []
        for a in range(n):
            sent.append(copy(a, 0, (x, y, c), sibling, src=ins[a]))
            sent += [copy(a, 1 + j, (x, y, c), (*chip, c), src=ins[a]) for j, chip in enumerate(chips)]
        for cp in sent:
            cp.start()
        for a in range(n):
            for j, chip in enumerate(chips):
                copy(a, 1 + j, (*chip, c), (x, y, c)).wait_recv()
                cp = copy(a, 4 + j, (*chip, c), sibling)
                cp.start()
                sent.append(cp)
        for a in range(n):
            copy(a, 0, sibling, (x, y, c)).wait_recv()
            for j, chip in enumerate(chips):
                copy(a, 4 + j, (*chip, 1 - c), (x, y, c)).wait_recv()
        for cp in sent:
            cp.wait_send()
        for cp in local:
            cp.wait()

    sems = pltpu.SemaphoreType.DMA((n, 7))
    return pl.pallas_call(
        body, in_specs=[HBM_SPEC] * n, out_specs=[HBM_SPEC] * n,
        out_shape=[jax.ShapeDtypeStruct((N_DEV,) + a.shape, a.dtype) for a in arrays],
        scratch_shapes=[sems, sems, pltpu.SemaphoreType.DMA((n,))], name=name)(*arrays)


def _layer_fwd(h, p):
    z, hn = _in_proj_fwd(h, p["norm_g"], p["w_in"])
    ya, hl = _branch_a_fwd(z, p["conv_a_w"], p["conv_a_b"], p["w_rg"], p["b_rg"], p["w_ig"], p["b_ig"], p["lam"])
    yb = _branch_b_fwd(z, p["conv_b_w"], p["conv_b_b"], p["ln_b_g"], p["ln_b_b"])
    merged, pa, pb = _merge_fwd(ya, yb, z, p["w_pa"], p["w_pb"])
    return _out_fwd(h, merged, p["w_out"]), (h, hn, z, hl, ya, yb, pa, pb, merged)


def _layer_bwd(dho, p, saved):
    h, hn, z, hl, ya, yb, pa, pb, merged = saved
    dpa, dpb, dma, dmb = _dmerged_bwd(dho, p["w_out"], z, pa, pb)
    g = {"w_out": _matmul_tn(merged, dho, "dw_out"), "w_proj_a": _matmul_tn(ya, dpa, "dw_proj_a"),
         "w_proj_b": _matmul_tn(yb, dpb, "dw_proj_b")}
    dya = _matmul_nt(dpa, p["w_pa"], "dya")
    dyb = _matmul_nt(dpb, p["w_pb"], "dyb")
    (dxa, dsa, g["conv_a_w"], g["conv_a_b"], g["b_rg"], g["b_ig"], g["lru_lambda"], g["w_rg"], g["w_ig"]) = \
        _branch_a_bwd(z, hl, dya, p["conv_a_w"], p["conv_a_b"], p["w_rg"], p["b_rg"], p["w_ig"], p["b_ig"], p["lam"])
    dvb, dgb, dsb, g["conv_b_w"], g["conv_b_b"], g["ln_b_g"], g["ln_b_b"] = \
        _branch_b_bwd(z, dyb, p["conv_b_w"], p["conv_b_b"], p["ln_b_g"], p["ln_b_b"])
    dzs = (dxa, dsa, dvb, dgb, dsb, dma, dmb)
    g["w_in"] = _dwin_bwd(hn, dzs)
    dh, g["norm_g"] = _din_bwd(dzs, p["w_in"], h, p["norm_g"], dho)
    return dh, g


def _forward_backward(h, target, layers, final_g, on_layer):
    saved = []
    for p in layers:
        h, s = _layer_fwd(h, p)
        saved.append(s)
    target = jnp.concatenate([jnp.zeros((N_META, h.shape[1]), F32), target], axis=0)
    loss_lanes, dh, g_final = _loss_head(h, final_g, target)
    for l in reversed(range(len(layers))):
        dh, g = _layer_bwd(dh, layers[l], saved[l])
        on_layer(l, g)
    return loss_lanes, dh, g_final


BIG = ("w_in", "w_proj_a", "w_proj_b", "w_out")
VECS = ("norm_g", "conv_a_b", "b_rg", "b_ig", "lru_lambda", "conv_b_b", "ln_b_g", "ln_b_b")


def _reduce_big(g, c_idx):
    d = g["w_out"].shape[0]
    parts = [g["w_in"]] + [g[k].reshape(N_CHIPS, d // N_CHIPS, d) for k in BIG[1:]]
    theirs = _swap_halves(parts, "swap_halves")
    chip_sums = [_sum_halves(a, b, c_idx) for a, b in zip(parts, theirs)]
    landed = _scatter_chips(chip_sums, "scatter_chips")
    halves = [_sum_slots(y, "sum_chips") for y in landed]
    return _join_halves(halves, "join_halves")


def kernel(x, meta, norm_g, w_in, conv_a_w, conv_a_b, w_rg, b_rg, w_ig, b_ig, lru_lambda, conv_b_w, conv_b_b, ln_b_g, ln_b_b, w_proj_a, w_proj_b, w_out, final_g, loss_target, m_meta, m_norm_g, m_w_in, m_conv_a_w, m_conv_a_b, m_w_rg, m_b_rg, m_w_ig, m_b_ig, m_lru_lambda, m_conv_b_w, m_conv_b_b, m_ln_b_g, m_ln_b_b, m_w_proj_a, m_w_proj_b, m_w_out, m_final_g, v_meta, v_norm_g, v_w_in, v_conv_a_w, v_conv_a_b, v_w_rg, v_b_rg, v_w_ig, v_b_ig, v_lru_lambda, v_conv_b_w, v_conv_b_b, v_ln_b_g, v_ln_b_b, v_w_proj_a, v_w_proj_b, v_w_out, v_final_g):
    names = ("meta", "norm_g", "w_in", "conv_a_w", "conv_a_b", "w_rg", "b_rg", "w_ig", "b_ig", "lru_lambda",
             "conv_b_w", "conv_b_b", "ln_b_g", "ln_b_b", "w_proj_a", "w_proj_b", "w_out", "final_g")
    w = dict(zip(names, (meta, norm_g, w_in, conv_a_w, conv_a_b, w_rg, b_rg, w_ig, b_ig, lru_lambda, conv_b_w,
                         conv_b_b, ln_b_g, ln_b_b, w_proj_a, w_proj_b, w_out, final_g)))
    m = dict(zip(names, (m_meta, m_norm_g, m_w_in, m_conv_a_w, m_conv_a_b, m_w_rg, m_b_rg, m_w_ig, m_b_ig,
                         m_lru_lambda, m_conv_b_w, m_conv_b_b, m_ln_b_g, m_ln_b_b, m_w_proj_a, m_w_proj_b, m_w_out,
                         m_final_g)))
    v = dict(zip(names, (v_meta, v_norm_g, v_w_in, v_conv_a_w, v_conv_a_b, v_w_rg, v_b_rg, v_w_ig, v_b_ig,
                         v_lru_lambda, v_conv_b_w, v_conv_b_b, v_ln_b_g, v_ln_b_b, v_w_proj_a, v_w_proj_b, v_w_out,
                         v_final_g)))
    depth, d = norm_g.shape
    dq = d // N_CHIPS
    px, py, pc = _place()
    chip = 2 * px + py
    c_idx = pc.astype(jnp.int32).reshape(1)

    meta_s, conv_a_s, conv_b_s = _gather_chips([meta, conv_a_w, conv_b_w], "gather_small")
    unshard = lambda s: jnp.concatenate([s[q] for q in range(N_CHIPS)], axis=-1)
    meta_f, conv_a_f, conv_b_f = unshard(meta_s), unshard(conv_a_s), unshard(conv_b_s)
    layers = []
    for l in range(depth):
        win, wpa, wpb, wo = _gather_chips(
            [w_in[l].astype(BF16), w_proj_a[l].astype(BF16), w_proj_b[l].astype(BF16), w_out[l].astype(BF16)],
            "gather_weights")
        layers.append({
            "w_in": win, "w_pa": wpa.reshape(d, d), "w_pb": wpb.reshape(d, d), "w_out": wo.reshape(d, d),
            "norm_g": norm_g[l][None], "conv_a_w": conv_a_f[l], "conv_a_b": conv_a_b[l][None],
            "w_rg": w_rg[l].astype(BF16), "b_rg": b_rg[l][None], "w_ig": w_ig[l].astype(BF16), "b_ig": b_ig[l][None],
            "lam": lru_lambda[l][None], "conv_b_w": conv_b_f[l], "conv_b_b": conv_b_b[l][None],
            "ln_b_g": ln_b_g[l][None], "ln_b_b": ln_b_b[l][None]})

    reduced = [None] * depth
    small = [None] * depth

    def on_layer(l, g):
        reduced[l] = _reduce_big(g, c_idx)
        small[l] = g

    loss_lanes, dh, g_final = _forward_backward(jnp.concatenate([meta_f, x[0]], axis=0), loss_target[0], layers,
                                                final_g[None], on_layer)
    loss = lax.psum(jnp.sum(loss_lanes), ("x", "y", "c"))
    grads = {}

    vec_rows = [small[l][k] for k in VECS for l in range(depth)]
    vec_rows += [small[l]["conv_a_w"] for l in range(depth)] + [small[l]["conv_b_w"] for l in range(depth)]
    vec_rows += [dh[:N_META], g_final]
    packed_v = jnp.concatenate(vec_rows, axis=0)
    pad = (-packed_v.shape[0]) % 8
    packed_v = jnp.concatenate([packed_v, jnp.zeros((pad, d), F32)], axis=0)
    packed_g = jnp.concatenate([small[l][k].reshape(-1, HEAD_DIM) for k in ("w_rg", "w_ig") for l in range(depth)],
                               axis=0)
    all_v, all_g = _gather_all([packed_v, packed_g], "gather_small_grads")
    sum_v = _sum_slots(all_v, "sum_devices_v")
    sum_g = _sum_slots(all_g, "sum_devices_g")
    cols = lambda a: lax.dynamic_slice_in_dim(a, chip * dq, dq, axis=-1)
    for n_k, k in enumerate(VECS):
        grads[k] = sum_v[n_k * depth:(n_k + 1) * depth]
    o = len(VECS) * depth
    grads["conv_a_w"] = cols(sum_v[o:o + depth * CONV_A].reshape(depth, CONV_A, d))
    o += depth * CONV_A
    grads["conv_b_w"] = cols(sum_v[o:o + depth * CONV_B].reshape(depth, CONV_B, d))
    o += depth * CONV_B
    grads["meta"] = cols(sum_v[o:o + N_META])
    grads["final_g"] = sum_v[o + N_META]
    gate_rows = depth * (d // HEAD_DIM) * HEAD_DIM
    grads["w_rg"] = sum_g[:gate_rows].reshape(w_rg.shape)
    grads["w_ig"] = sum_g[gate_rows:].reshape(w_ig.shape)

    outs = {k: _adamw(w[k], grads[k], m[k], v[k], f"adamw_{k}") for k in names if k not in BIG}
    for n_k, k in enumerate(BIG):
        outs[k] = _adamw_layers(w[k], [reduced[l][n_k] for l in range(depth)], m[k], v[k], f"adamw_{k}")
    dx = dh[N_META:][None]
    return (loss, dx, *[outs[k][0] for k in names], *[outs[k][1] for k in names],
            *[outs[k][2] for k in names], *[outs[k][3] for k in names])
```

```python
import functools

import jax
import jax.numpy as jnp
from jax import lax
from jax.experimental import pallas as pl
from jax.experimental.pallas import tpu as pltpu

F32 = jnp.float32
BF16 = jnp.bfloat16
MESH = pl.DeviceIdType.MESH

EPS = 1e-6
N_META = 16
HEAD_DIM = 128
CONV_A = 4
CONV_B = 31
LRU_C = 8.0
N_SPLIT = 7
N_CHIPS = 4
N_DEV = 8

ADAM_LR = 0.001
ADAM_B1 = 0.9
ADAM_B2 = 0.999
ADAM_EPS = 1e-08
ADAM_WD = 0.01
ADAM_STEP = 10

HALO_A = 16
HALO_B = 48
ROW_TILE_MM = 912
ROW_TILE_MM_BWD = 432
ROW_TILE_EW = 144
ROW_BLOCK = 16
LANE_CHUNK = 256
LOCAL_CHUNKS = 2
COL_TILE = 512
VMEM_LIMIT = 56 * 1024 * 1024
BLOCK_BYTES = 2 * 1024 * 1024

NT_DIMS = (((1,), (1,)), ((), ()))
TN_DIMS = (((0,), (0,)), ((), ()))


def _pick(n, cap, mult):
    best = None
    for d in range(mult, min(n, cap) + 1, mult):
        if n % d == 0:
            best = d
    if best is None:
        raise ValueError(f"no tile for {n} (cap {cap}, multiple of {mult})")
    return best


def _params(*semantics):
    return pltpu.CompilerParams(dimension_semantics=semantics, vmem_limit_bytes=VMEM_LIMIT)


def _whole(shape):
    return pl.BlockSpec(shape, lambda *_: (0,) * len(shape))


def _sigmoid(v):
    return 0.5 * jnp.tanh(0.5 * v) + 0.5


def _dsilu(v, s):
    return s * (1.0 + v * (1.0 - s))


def _log1p(e):
    w = 1.0 + e
    return jnp.where(w == 1.0, e, e * jnp.log(w) / (w - 1.0))


def _softplus(v):
    return jnp.maximum(v, 0.0) + _log1p(jnp.exp(-jnp.abs(v)))


def _neg_expm1_twice(v):
    t = jnp.tanh(v)
    return -2.0 * t / (1.0 - t)


def _row_iota(shape):
    return lax.broadcasted_iota(jnp.int32, shape, 0)


def _call(body, rider, operands, *, grid, in_specs, out_specs, out_shape, scratch_shapes, semantics, name):
    if rider is None:
        return pl.pallas_call(
            body, grid=grid, in_specs=in_specs, out_specs=out_specs, out_shape=out_shape,
            scratch_shapes=scratch_shapes, compiler_params=_params(*semantics), name=name)(*operands), None
    return _hosted(body, rider, grid=grid, in_specs=in_specs, out_specs=out_specs, out_shape=out_shape,
                   scratch_shapes=scratch_shapes, name=name)(*operands)


def _in_proj_fwd(h, g, w, rider=None):
    t, d = h.shape
    nq = w.shape[2]
    tm = _pick(t, ROW_TILE_MM, 16)
    tn = _pick(nq, COL_TILE, 128)
    bps = nq // tn

    def body(h_ref, g_ref, w_ref, z_ref, hn_ref):
        @pl.when(pl.program_id(1) == 0)
        def _():
            def norm(r0):
                rows = _rows(r0)
                v = h_ref[rows, :]
                rinv = lax.rsqrt(jnp.mean(v * v, axis=-1, keepdims=True) + EPS)
                hn_ref[rows, :] = (v * rinv * g_ref[...]).astype(BF16)
            _row_blocks(tm, norm)
        z_ref[...] = jnp.dot(hn_ref[...], w_ref[...], preferred_element_type=F32)

    return _call(
        body, rider, (h, g, w), grid=(t // tm, N_CHIPS * bps),
        in_specs=[pl.BlockSpec((tm, d), lambda i, j: (i, 0)), _whole((1, d)),
                  pl.BlockSpec((None, d, tn), lambda i, j: (j // bps, 0, j % bps))],
        out_specs=[pl.BlockSpec((tm, tn), lambda i, j: (i, j)), pl.BlockSpec((tm, d), lambda i, j: (i, 0))],
        out_shape=[jax.ShapeDtypeStruct((t, N_CHIPS * nq), F32), jax.ShapeDtypeStruct((t, d), BF16)],
        scratch_shapes=[], semantics=("parallel", "arbitrary"), name="in_proj_fwd")


def _conv_a(xext, wa_ref, ba, tt):
    acc = ba
    for k in range(CONV_A):
        acc = acc + wa_ref[k:k + 1, :] * xext[pl.ds(HALO_A - (CONV_A - 1) + k, tt), :]
    return acc


def _gates_a(ca, wr_ref, wi_ref, br, bi, gr_s, gi_s):
    cab = ca.astype(BF16)
    for hh in range(ca.shape[1] // HEAD_DIM):
        sl = slice(hh * HEAD_DIM, (hh + 1) * HEAD_DIM)
        gr_s[:, sl] = jnp.dot(cab[:, sl], wr_ref[hh], preferred_element_type=F32)
        gi_s[:, sl] = jnp.dot(cab[:, sl], wi_ref[hh], preferred_element_type=F32)
    return _sigmoid(gr_s[...] + br), _sigmoid(gi_s[...] + bi)


def _decay_a(r, sp):
    log_a = -LRU_C * r * sp
    a = jnp.exp(log_a)
    mult = jnp.sqrt(jnp.maximum(_neg_expm1_twice(log_a), 0.0))
    return a, mult


def _scan_fwd(a_ref, u_ref, out_ref, h0, nblk):
    def blk(b, hprev):
        rows = pl.ds(pl.multiple_of(b * 8, 8), 8)
        ca, cb = a_ref[rows, :], u_ref[rows, :]
        row = _row_iota(ca.shape)
        for dist in (1, 2, 4):
            m = row >= dist
            cb = jnp.where(m, ca * pltpu.roll(cb, dist, 0) + cb, cb)
            ca = jnp.where(m, ca * pltpu.roll(ca, dist, 0), ca)
        hb = ca * hprev + cb
        out_ref[rows, :] = hb
        return hb[7:8, :]
    return lax.fori_loop(0, nblk, blk, h0)


def _scan_rev(a_ref, d_ref, out_ref, g0, nblk):
    def blk(k, g):
        rows = pl.ds(pl.multiple_of((nblk - 1 - k) * 8, 8), 8)
        a, cb = a_ref[rows, :], d_ref[rows, :]
        row = _row_iota(a.shape)
        ca = jnp.where(row == 7, 1.0, pltpu.roll(a, 7, 0))
        for dist in (1, 2, 4):
            m = row < 8 - dist
            cb = jnp.where(m, cb + ca * pltpu.roll(cb, 8 - dist, 0), cb)
            ca = jnp.where(m, ca * pltpu.roll(ca, 8 - dist, 0), ca)
        lam = cb + ca * g
        out_ref[rows, :] = lam
        return a[0:1, :] * lam[0:1, :]
    return lax.fori_loop(0, nblk, blk, g0)


def _branch_a_fwd(z, wa, ba, wr, br, wi, bi, lam):
    t = z.shape[0]
    d = wa.shape[1]
    tt = _pick(t, ROW_TILE_EW, HALO_B)
    hb = tt // HALO_A
    nh = d // HEAD_DIM

    def body(xa_ref, halo_ref, sa_ref, wa_ref, ba_ref, wr_ref, br_ref, wi_ref, bi_ref, lam_ref,
             ya_ref, hl_ref, xext, gr_s, gi_s, a_s, carry):
        i = pl.program_id(0)

        @pl.when(i == 0)
        def _():
            carry[...] = jnp.zeros_like(carry)
        xext[0:HALO_A, :] = jnp.where(i == 0, 0.0, halo_ref[...])
        xext[HALO_A:, :] = xa_ref[...]
        ca = _conv_a(xext, wa_ref, ba_ref[...], tt)
        r, ig = _gates_a(ca, wr_ref, wi_ref, br_ref[...], bi_ref[...], gr_s, gi_s)
        a, mult = _decay_a(r, _softplus(-lam_ref[...]))
        a_s[...] = a
        hl_ref[...] = mult * (ig * ca)
        carry[...] = _scan_fwd(a_s, hl_ref, hl_ref, carry[...], tt // 8)
        sa = sa_ref[...]
        ya_ref[...] = (hl_ref[...] * (sa * _sigmoid(sa))).astype(BF16)

    vec = _whole((1, d))
    return pl.pallas_call(
        body, grid=(t // tt,),
        in_specs=[pl.BlockSpec((tt, d), lambda i: (i, 0)),
                  pl.BlockSpec((HALO_A, d), lambda i: (jnp.maximum(i * hb - 1, 0), 0)),
                  pl.BlockSpec((tt, d), lambda i: (i, 1)),
                  _whole((CONV_A, d)), vec, _whole((nh, HEAD_DIM, HEAD_DIM)), vec,
                  _whole((nh, HEAD_DIM, HEAD_DIM)), vec, vec],
        out_specs=[pl.BlockSpec((tt, d), lambda i: (i, 0)), pl.BlockSpec((tt, d), lambda i: (i, 0))],
        out_shape=[jax.ShapeDtypeStruct((t, d), BF16), jax.ShapeDtypeStruct((t, d), F32)],
        scratch_shapes=[pltpu.VMEM((tt + HALO_A, d), F32), pltpu.VMEM((tt, d), F32), pltpu.VMEM((tt, d), F32),
                        pltpu.VMEM((tt, d), F32), pltpu.VMEM((1, d), F32)],
        compiler_params=_params("arbitrary"), name="branch_a_fwd")(z, z, z, wa, ba, wr, br, wi, bi, lam)


def _row_blocks(n_rows, body):
    def step(b, carry):
        body(pl.multiple_of(b * ROW_BLOCK, ROW_BLOCK))
        return carry
    lax.fori_loop(0, n_rows // ROW_BLOCK, step, 0)


def _rows(r0, offset=0):
    return pl.ds(pl.multiple_of(r0 + offset, ROW_BLOCK), ROW_BLOCK)


def _glu_ext(gext, vb_ref, gb_ref, vbh_ref, gbh_ref, first, tt):
    def halo(r0):
        rows = _rows(r0)
        gext[rows, :] = jnp.where(first, 0.0, vbh_ref[rows, :] * _sigmoid(gbh_ref[rows, :]))
    _row_blocks(HALO_B, halo)

    def tile(r0):
        rows = _rows(r0)
        gext[_rows(r0, HALO_B), :] = vb_ref[rows, :] * _sigmoid(gb_ref[rows, :])
    _row_blocks(tt, tile)


def _shifted(src, dst, cs, n_rows):
    for b in range(1, 8):
        dst[b - 1, 0:n_rows, :] = src[pl.ds(b, n_rows), cs]


def _tap(src, shifted, cs, off, r0, n):
    b = off % 8
    if b == 0:
        return src[pl.ds(off + r0, n), cs]
    return shifted[b - 1, pl.ds(off - b + r0, n), :]


def _conv_rows(tt):
    return tt // 2 if tt % 16 == 0 else tt


def _conv_b_fwd(gext, gs, wb_ref, bb_ref, cb_s, tt, d):
    rc = _conv_rows(tt)
    for c in range(d // LANE_CHUNK):
        cs = slice(c * LANE_CHUNK, (c + 1) * LANE_CHUNK)
        _shifted(gext, gs, cs, tt + HALO_B - 8)
        for r0 in range(0, tt, rc):
            acc = jnp.broadcast_to(bb_ref[:, cs], (rc, LANE_CHUNK))
            for k in range(CONV_B):
                acc = acc + wb_ref[k:k + 1, cs] * _tap(gext, gs, cs, HALO_B - (CONV_B - 1) + k, r0, rc)
            cb_s[r0:r0 + rc, cs] = acc


def _layer_norm_stats(cb):
    mu = jnp.mean(cb, axis=-1, keepdims=True)
    xc = cb - mu
    rstd = lax.rsqrt(jnp.mean(xc * xc, axis=-1, keepdims=True) + EPS)
    return xc * rstd, rstd


def _branch_b_fwd(z, wb, bb, lg, lb, rider=None):
    t = z.shape[0]
    d = wb.shape[1]
    tt = _pick(t, ROW_TILE_EW, HALO_B)
    hb = tt // HALO_B

    def body(vb_ref, gb_ref, vbh_ref, gbh_ref, sb_ref, wb_ref, bb_ref, lg_ref, lb_ref, yb_ref, cb_s, gext, gs):
        _glu_ext(gext, vb_ref, gb_ref, vbh_ref, gbh_ref, pl.program_id(0) == 0, tt)
        _conv_b_fwd(gext, gs, wb_ref, bb_ref, cb_s, tt, d)

        def finish(r0):
            rows = _rows(r0)
            xhat, _ = _layer_norm_stats(cb_s[rows, :])
            ln = xhat * lg_ref[...] + lb_ref[...]
            sb = sb_ref[rows, :]
            yb_ref[rows, :] = (ln * _sigmoid(ln) * (sb * _sigmoid(sb))).astype(BF16)
        _row_blocks(tt, finish)

    vec = _whole((1, d))
    tile = lambda s: pl.BlockSpec((tt, d), lambda i: (i, s))
    halo = lambda s: pl.BlockSpec((HALO_B, d), lambda i: (jnp.maximum(i * hb - 1, 0), s))
    return _call(
        body, rider, (z, z, z, z, z, wb, bb, lg, lb), grid=(t // tt,),
        in_specs=[tile(2), tile(3), halo(2), halo(3), tile(4), _whole((CONV_B, d)), vec, vec, vec],
        out_specs=[pl.BlockSpec((tt, d), lambda i: (i, 0))] * 2,
        out_shape=[jax.ShapeDtypeStruct((t, d), BF16), jax.ShapeDtypeStruct((t, d), F32)],
        scratch_shapes=[pltpu.VMEM((tt + HALO_B, d), F32), pltpu.VMEM((7, tt + HALO_B - 8, LANE_CHUNK), F32)],
        semantics=("parallel",), name="branch_b_fwd")


def _merge_fwd(ya, yb, z, wpa, wpb):
    t, d = ya.shape
    tm = _pick(t, ROW_TILE_MM, 16)
    tn = _pick(d, COL_TILE, 128)
    nb = d // tn

    def body(ya_ref, yb_ref, ma_ref, mb_ref, wpa_ref, wpb_ref, mg_ref, pa_ref, pb_ref):
        pa = jnp.dot(ya_ref[...], wpa_ref[...], preferred_element_type=F32)
        pb = jnp.dot(yb_ref[...], wpb_ref[...], preferred_element_type=F32)
        mg_ref[...] = (_sigmoid(ma_ref[...]) * pa + _sigmoid(mb_ref[...]) * pb).astype(BF16)
        pa_ref[...] = pa.astype(BF16)
        pb_ref[...] = pb.astype(BF16)

    rows = pl.BlockSpec((tm, d), lambda i, j: (i, 0))
    wcol = pl.BlockSpec((d, tn), lambda i, j: (0, j))
    outb = pl.BlockSpec((tm, tn), lambda i, j: (i, j))
    return pl.pallas_call(
        body, grid=(t // tm, nb),
        in_specs=[rows, rows, pl.BlockSpec((tm, tn), lambda i, j: (i, 5 * nb + j)),
                  pl.BlockSpec((tm, tn), lambda i, j: (i, 6 * nb + j)), wcol, wcol],
        out_specs=[outb, outb, outb],
        out_shape=[jax.ShapeDtypeStruct((t, d), BF16)] * 3,
        compiler_params=_params("parallel", "arbitrary"), name="merge_fwd")(ya, yb, z, z, wpa, wpb)


def _out_fwd(h, merged, wout):
    t, d = h.shape
    tm = _pick(t, ROW_TILE_MM, 16)
    tn = _pick(d, COL_TILE, 128)

    def body(h_ref, mg_ref, w_ref, o_ref):
        o_ref[...] = h_ref[...] + jnp.dot(mg_ref[...], w_ref[...], preferred_element_type=F32)

    return pl.pallas_call(
        body, grid=(t // tm, d // tn),
        in_specs=[pl.BlockSpec((tm, tn), lambda i, j: (i, j)), pl.BlockSpec((tm, d), lambda i, j: (i, 0)),
                  pl.BlockSpec((d, tn), lambda i, j: (0, j))],
        out_specs=pl.BlockSpec((tm, tn), lambda i, j: (i, j)),
        out_shape=jax.ShapeDtypeStruct((t, d), F32),
        compiler_params=_params("parallel", "arbitrary"), name="out_fwd")(h, merged, wout)


def _loss_head(h, g, target):
    t, d = h.shape
    tt = _pick(t, ROW_TILE_EW, 16)

    def body(h_ref, g_ref, tg_ref, loss_ref, dh_ref, dg_ref):
        i = pl.program_id(0)

        @pl.when(i == 0)
        def _():
            loss_ref[...] = jnp.zeros_like(loss_ref)
            dg_ref[...] = jnp.zeros_like(dg_ref)
        v = h_ref[...]
        g = g_ref[...]
        rinv = lax.rsqrt(jnp.mean(v * v, axis=-1, keepdims=True) + EPS)
        xh = v * rinv
        valid = (i * tt + _row_iota(v.shape)) >= N_META
        diff = jnp.where(valid, xh * g - tg_ref[...], 0.0)
        loss_ref[...] += (0.5 / d) * jnp.sum(diff * diff, axis=0, keepdims=True)
        dy = diff * (1.0 / d)
        dg_ref[...] += jnp.sum(dy * xh, axis=0, keepdims=True)
        dxh = dy * g
        dh_ref[...] = rinv * (dxh - xh * jnp.mean(dxh * xh, axis=-1, keepdims=True))

    tile = pl.BlockSpec((tt, d), lambda i: (i, 0))
    vec = _whole((1, d))
    return pl.pallas_call(
        body, grid=(t // tt,), in_specs=[tile, vec, tile], out_specs=[vec, tile, vec],
        out_shape=[jax.ShapeDtypeStruct((1, d), F32), jax.ShapeDtypeStruct((t, d), F32),
                   jax.ShapeDtypeStruct((1, d), F32)],
        compiler_params=_params("arbitrary"), name="loss_head")(h, g, target)


def _dmerged_bwd(dho, wout, z, pa, pb):
    t, d = dho.shape
    tm = _pick(t, ROW_TILE_MM, 16)
    tn = _pick(d, COL_TILE, 128)
    nb = d // tn

    def body(dho_ref, w_ref, ma_ref, mb_ref, pa_ref, pb_ref, dpa_ref, dpb_ref, dma_ref, dmb_ref, dho_s):
        @pl.when(pl.program_id(1) == 0)
        def _():
            dho_s[...] = dho_ref[...].astype(BF16)
        dm = lax.dot_general(dho_s[...], w_ref[...], NT_DIMS, preferred_element_type=F32)
        sa = _sigmoid(ma_ref[...])
        sb = _sigmoid(mb_ref[...])
        dpa_ref[...] = (dm * sa).astype(BF16)
        dpb_ref[...] = (dm * sb).astype(BF16)
        dma_ref[...] = (dm * pa_ref[...].astype(F32) * (sa * (1.0 - sa))).astype(BF16)
        dmb_ref[...] = (dm * pb_ref[...].astype(F32) * (sb * (1.0 - sb))).astype(BF16)

    blk = pl.BlockSpec((tm, tn), lambda i, j: (i, j))
    return pl.pallas_call(
        body, grid=(t // tm, nb),
        in_specs=[pl.BlockSpec((tm, d), lambda i, j: (i, 0)), pl.BlockSpec((tn, d), lambda i, j: (j, 0)),
                  pl.BlockSpec((tm, tn), lambda i, j: (i, 5 * nb + j)),
                  pl.BlockSpec((tm, tn), lambda i, j: (i, 6 * nb + j)), blk, blk],
        out_specs=[blk, blk, blk, blk],
        out_shape=[jax.ShapeDtypeStruct((t, d), BF16)] * 4,
        scratch_shapes=[pltpu.VMEM((tm, d), BF16)],
        compiler_params=_params("parallel", "arbitrary"), name="dmerged_bwd")(dho, wout, z, z, pa, pb)


def _matmul_nt(a, w, name):
    t, k = a.shape
    n = w.shape[0]
    tm = _pick(t, ROW_TILE_MM, 16)
    tn = _pick(n, COL_TILE, 128)

    def body(a_ref, w_ref, o_ref):
        o_ref[...] = lax.dot_general(a_ref[...], w_ref[...], NT_DIMS, preferred_element_type=F32)

    return pl.pallas_call(
        body, grid=(t // tm, n // tn),
        in_specs=[pl.BlockSpec((tm, k), lambda i, j: (i, 0)), pl.BlockSpec((tn, k), lambda i, j: (j, 0))],
        out_specs=pl.BlockSpec((tm, tn), lambda i, j: (i, j)),
        out_shape=jax.ShapeDtypeStruct((t, n), F32),
        compiler_params=_params("parallel", "arbitrary"), name=name)(a, w)


def _matmul_tn(a, b, name):
    t, m = a.shape
    n = b.shape[1]
    tk = _pick(t, ROW_TILE_MM, 16)
    tn = _pick(n, COL_TILE, 128)

    def body(a_ref, b_ref, o_ref):
        @pl.when(pl.program_id(1) == 0)
        def _():
            o_ref[...] = jnp.zeros_like(o_ref)
        o_ref[...] += lax.dot_general(a_ref[...].astype(BF16), b_ref[...].astype(BF16), TN_DIMS,
                                      preferred_element_type=F32)

    return pl.pallas_call(
        body, grid=(n // tn, t // tk),
        in_specs=[pl.BlockSpec((tk, m), lambda j, s: (s, 0)), pl.BlockSpec((tk, tn), lambda j, s: (s, j))],
        out_specs=pl.BlockSpec((m, tn), lambda j, s: (0, j)),
        out_shape=jax.ShapeDtypeStruct((m, n), F32),
        compiler_params=_params("parallel", "arbitrary"), name=name)(a, b)


def _dwin_bwd(hn, dzs, rider=None):
    t, d = hn.shape
    nq = N_SPLIT * d // N_CHIPS
    tk = _pick(t, ROW_TILE_MM, 16)
    tn = _pick(nq, COL_TILE, 128)
    bps = nq // tn
    bpz = d // tn

    def body(hn_ref, *rest):
        dz_refs, o_ref = rest[:N_SPLIT], rest[N_SPLIT]
        j = pl.program_id(0)

        @pl.when(pl.program_id(1) == 0)
        def _():
            o_ref[...] = jnp.zeros_like(o_ref)
        for s in range(N_SPLIT):
            @pl.when(j // bpz == s)
            def _(s=s):
                o_ref[...] += lax.dot_general(hn_ref[...], dz_refs[s][...], TN_DIMS, preferred_element_type=F32)

    def dz_spec(s):
        def imap(j, r):
            on = (j // bpz) == s
            return (jnp.where(on, r, 0), jnp.where(on, j % bpz, 0))
        return pl.BlockSpec((tk, tn), imap)

    return _call(
        body, rider, (hn, *dzs), grid=(N_SPLIT * bpz, t // tk),
        in_specs=[pl.BlockSpec((tk, d), lambda j, r: (r, 0))] + [dz_spec(s) for s in range(N_SPLIT)],
        out_specs=[pl.BlockSpec((None, d, tn), lambda j, r: (j // bps, 0, j % bps))],
        out_shape=[jax.ShapeDtypeStruct((N_CHIPS, d, nq), F32)],
        scratch_shapes=[], semantics=("parallel", "arbitrary"), name="dwin_bwd")


def _din_bwd(dzs, w, h, g, dho):
    t, d = h.shape
    nq = w.shape[2]
    tm = _pick(t, ROW_TILE_MM, 16)
    tk = _pick(nq, COL_TILE, 128)
    bps = nq // tk
    bpz = d // tk
    nk = N_SPLIT * bpz

    def body(*refs):
        dz_refs = refs[:N_SPLIT]
        w_ref, h_ref, g_ref, dho_ref, dh_ref, dg_ref, dg8 = refs[N_SPLIT:]
        i, k = pl.program_id(0), pl.program_id(1)

        @pl.when(k == 0)
        def _():
            dh_ref[...] = jnp.zeros_like(dh_ref)

        @pl.when((i == 0) & (k == 0))
        def _():
            dg8[...] = jnp.zeros_like(dg8)
        for s in range(N_SPLIT):
            @pl.when(k // bpz == s)
            def _(s=s):
                dh_ref[...] += lax.dot_general(dz_refs[s][...], w_ref[...], NT_DIMS, preferred_element_type=F32)

        @pl.when(k == nk - 1)
        def _():
            def norm_bwd(r0):
                rows = _rows(r0)
                v = h_ref[rows, :]
                rinv = lax.rsqrt(jnp.mean(v * v, axis=-1, keepdims=True) + EPS)
                xh = v * rinv
                dhn = dh_ref[rows, :]
                part = dhn * xh
                dg8[...] += part[0:8, :] + part[8:16, :]
                dxh = dhn * g_ref[...]
                dh_ref[rows, :] = dho_ref[rows, :] + rinv * (dxh - xh * jnp.mean(dxh * xh, axis=-1, keepdims=True))
            _row_blocks(tm, norm_bwd)
            dg_ref[...] = jnp.sum(dg8[...], axis=0, keepdims=True)

    def dz_spec(s):
        return pl.BlockSpec((tm, tk), lambda i, k: (i, jnp.clip(k - s * bpz, 0, bpz - 1)))

    rows = pl.BlockSpec((tm, d), lambda i, k: (i, 0))
    once = pl.BlockSpec((tm, d), lambda i, k: (i, 0), pipeline_mode=pl.Buffered(1))
    vec = _whole((1, d))
    return pl.pallas_call(
        body, grid=(t // tm, nk),
        in_specs=[dz_spec(s) for s in range(N_SPLIT)]
        + [pl.BlockSpec((None, d, tk), lambda i, k: (k // bps, 0, k % bps)), once, vec, once],
        out_specs=[rows, vec],
        out_shape=[jax.ShapeDtypeStruct((t, d), F32), jax.ShapeDtypeStruct((1, d), F32)],
        scratch_shapes=[pltpu.VMEM((8, d), F32)],
        compiler_params=_params("arbitrary", "arbitrary"), name="din_bwd")(*dzs, w, h, g, dho)


def _branch_b_bwd(z, cb, dyb, wb, lg, lb):
    t = z.shape[0]
    d = wb.shape[1]
    tt = _pick(t, ROW_TILE_EW, HALO_B)
    hb = tt // HALO_B
    nt = t // tt
    rc = _conv_rows(tt)

    def body(vb_ref, gb_ref, vbh_ref, gbh_ref, sb_ref, cb_ref, dyb_ref, wb_ref, lg_ref, lb_ref,
             dvb_ref, dgb_ref, dsb_ref, dwb_ref, dbb_ref, dlg_ref, dlb_ref,
             gext, dext, gs, ds, cb_s, carry, dwb8, vec8):
        i = pl.program_id(0)

        @pl.when(i == 0)
        def _():
            for ref in (carry, dwb8, vec8):
                ref[...] = jnp.zeros_like(ref)
        _glu_ext(gext, vb_ref, gb_ref, vbh_ref, gbh_ref, i == nt - 1, tt)

        def norm_bwd(r0):
            rows = _rows(r0)
            xhat, rstd = _layer_norm_stats(cb_ref[rows, :])
            lg = lg_ref[...]
            ln = xhat * lg + lb_ref[...]
            sl = _sigmoid(ln)
            sb = sb_ref[rows, :]
            ss = _sigmoid(sb)
            dyb = dyb_ref[rows, :]
            dln = dyb * (sb * ss) * _dsilu(ln, sl)
            dsb_ref[rows, :] = (dyb * (ln * sl) * _dsilu(sb, ss)).astype(BF16)
            dxh = dln * lg
            dcb = rstd * (dxh - jnp.mean(dxh, axis=-1, keepdims=True)
                          - xhat * jnp.mean(dxh * xhat, axis=-1, keepdims=True))
            dext[rows, :] = dcb
            for n, val in enumerate((dln * xhat, dln, dcb)):
                vec8[n] += val[0:8, :] + val[8:16, :]
        _row_blocks(tt, norm_bwd)
        dext[tt:, :] = carry[...]
        carry[...] = dext[0:HALO_B, :]
        for c in range(d // LANE_CHUNK):
            cs = slice(c * LANE_CHUNK, (c + 1) * LANE_CHUNK)
            _shifted(gext, gs, cs, tt + HALO_B - 8)
            _shifted(dext, ds, cs, tt + HALO_B - 8)
            for r0 in range(0, tt, rc):
                dcb = dext[r0:r0 + rc, cs]
                acc = jnp.zeros((rc, LANE_CHUNK), F32)
                for k in range(CONV_B):
                    acc = acc + wb_ref[k:k + 1, cs] * _tap(dext, ds, cs, CONV_B - 1 - k, r0, rc)
                    prod = dcb * _tap(gext, gs, cs, HALO_B - (CONV_B - 1) + k, r0, rc)
                    part = prod[0:8, :]
                    for j in range(8, rc, 8):
                        part = part + prod[j:j + 8, :]
                    dwb8[k, :, cs] += part
                cb_s[r0:r0 + rc, cs] = acc

        def glu_bwd(r0):
            rows = _rows(r0)
            dglu = cb_s[rows, :]
            sg = _sigmoid(gb_ref[rows, :])
            dvb_ref[rows, :] = (dglu * sg).astype(BF16)
            dgb_ref[rows, :] = (dglu * vb_ref[rows, :] * (sg * (1.0 - sg))).astype(BF16)
        _row_blocks(tt, glu_bwd)

        @pl.when(i == nt - 1)
        def _():
            dwb_ref[...] = jnp.sum(dwb8[...], axis=1)
            dlg_ref[...] = jnp.sum(vec8[0], axis=0, keepdims=True)
            dlb_ref[...] = jnp.sum(vec8[1], axis=0, keepdims=True)
            dbb_ref[...] = jnp.sum(vec8[2], axis=0, keepdims=True)

    rev = lambda i: nt - 1 - i
    vec = _whole((1, d))
    tile = lambda s: pl.BlockSpec((tt, d), lambda i: (rev(i), s))
    halo = lambda s: pl.BlockSpec((HALO_B, d), lambda i: (jnp.maximum(rev(i) * hb - 1, 0), s))
    otile = pl.BlockSpec((tt, d), lambda i: (rev(i), 0))
    ext = pltpu.VMEM((tt + HALO_B, d), F32)
    shifts = pltpu.VMEM((7, tt + HALO_B - 8, LANE_CHUNK), F32)
    return pl.pallas_call(
        body, grid=(nt,),
        in_specs=[tile(2), tile(3), halo(2), halo(3), tile(4), otile, otile, _whole((CONV_B, d)), vec, vec],
        out_specs=[otile, otile, otile, _whole((CONV_B, d)), vec, vec, vec],
        out_shape=[jax.ShapeDtypeStruct((t, d), BF16)] * 3
        + [jax.ShapeDtypeStruct((CONV_B, d), F32)] + [jax.ShapeDtypeStruct((1, d), F32)] * 3,
        scratch_shapes=[ext, ext, shifts, shifts, pltpu.VMEM((tt, d), F32), pltpu.VMEM((HALO_B, d), F32),
                        pltpu.VMEM((CONV_B, 8, d), F32), pltpu.VMEM((3, 8, d), F32)],
        compiler_params=_params("arbitrary"), name="branch_b_bwd")(z, z, z, z, z, cb, dyb, wb, lg, lb)


def _branch_a_bwd(z, hl, dya, wa, ba, wr, br, wi, bi, lam):
    t = z.shape[0]
    d = wa.shape[1]
    tt = _pick(t, ROW_TILE_EW, HALO_B)
    hb = tt // HALO_A
    nt = t // tt
    nh = d // HEAD_DIM

    def body(xa_ref, xah_ref, sa_ref, hl_ref, hlh_ref, dya_ref, wa_ref, ba_ref, wr_ref, br_ref, wi_ref, bi_ref,
             lam_ref, dxa_ref, dsa_ref, dwa_ref, dba_ref, dbr_ref, dbi_ref, dlam_ref, dwr_ref, dwi_ref,
             xext, hext, dext, gr_s, gi_s, a_s, lam_s, gcarry, dcarry):
        i = pl.program_id(0)
        first = i == nt - 1

        @pl.when(i == 0)
        def _():
            for ref in (dwa_ref, dba_ref, dbr_ref, dbi_ref, dlam_ref, dwr_ref, dwi_ref, gcarry, dcarry):
                ref[...] = jnp.zeros_like(ref)
        xext[0:HALO_A, :] = jnp.where(first, 0.0, xah_ref[...])
        xext[HALO_A:, :] = xa_ref[...]
        hext[0:HALO_A, :] = jnp.where(first, 0.0, hlh_ref[...])
        hext[HALO_A:, :] = hl_ref[...]
        ca = _conv_a(xext, wa_ref, ba_ref[...], tt)
        r, ig = _gates_a(ca, wr_ref, wi_ref, br_ref[...], bi_ref[...], gr_s, gi_s)
        lam = lam_ref[...]
        sp = _softplus(-lam)
        a, mult = _decay_a(r, sp)
        sa = sa_ref[...]
        ss = _sigmoid(sa)
        dya = dya_ref[...]
        dsa_ref[...] = (dya * hl_ref[...] * _dsilu(sa, ss)).astype(BF16)
        a_s[...] = a
        lam_s[...] = dya * (sa * ss)
        gcarry[...] = _scan_rev(a_s, lam_s, lam_s, gcarry[...], tt // 8)
        du = lam_s[...]
        da = du * hext[pl.ds(HALO_A - 1, tt), :]
        dig = du * mult * ca
        dca = du * mult * ig
        dlog_a = da * a - (du * ig * ca) * jnp.where(mult > 0.0, a * a / mult, 0.0)
        dgr = (dlog_a * (-LRU_C * sp)) * r * (1.0 - r)
        dgi = dig * ig * (1.0 - ig)
        dlam_ref[...] += jnp.sum(dlog_a * r, axis=0, keepdims=True) * (LRU_C * _sigmoid(-lam))
        dbr_ref[...] += jnp.sum(dgr, axis=0, keepdims=True)
        dbi_ref[...] += jnp.sum(dgi, axis=0, keepdims=True)
        cab = ca.astype(BF16)
        dgrb = dgr.astype(BF16)
        dgib = dgi.astype(BF16)
        for hh in range(nh):
            sl = slice(hh * HEAD_DIM, (hh + 1) * HEAD_DIM)
            dwr_ref[hh] += lax.dot_general(cab[:, sl], dgrb[:, sl], TN_DIMS, preferred_element_type=F32)
            dwi_ref[hh] += lax.dot_general(cab[:, sl], dgib[:, sl], TN_DIMS, preferred_element_type=F32)
            gr_s[:, sl] = (lax.dot_general(dgrb[:, sl], wr_ref[hh], NT_DIMS, preferred_element_type=F32)
                           + lax.dot_general(dgib[:, sl], wi_ref[hh], NT_DIMS, preferred_element_type=F32))
        dca = dca + gr_s[...]
        dba_ref[...] += jnp.sum(dca, axis=0, keepdims=True)
        dext[0:tt, :] = dca
        dext[tt:, :] = dcarry[...]
        dcarry[...] = dca[0:HALO_A, :]
        dxa = jnp.zeros_like(dca)
        for k in range(CONV_A):
            off = HALO_A - (CONV_A - 1) + k
            dwa_ref[k:k + 1, :] += jnp.sum(dext[0:tt, :] * xext[pl.ds(off, tt), :], axis=0, keepdims=True)
            dxa = dxa + wa_ref[k:k + 1, :] * dext[pl.ds(CONV_A - 1 - k, tt), :]
        dxa_ref[...] = dxa.astype(BF16)

    rev = lambda i: nt - 1 - i
    vec = _whole((1, d))
    hw = _whole((nh, HEAD_DIM, HEAD_DIM))
    tile = lambda s: pl.BlockSpec((tt, d), lambda i: (rev(i), s))
    halo = pl.BlockSpec((HALO_A, d), lambda i: (jnp.maximum(rev(i) * hb - 1, 0), 0))
    big = pltpu.VMEM((tt + HALO_A, d), F32)
    full = pltpu.VMEM((tt, d), F32)
    return pl.pallas_call(
        body, grid=(nt,),
        in_specs=[tile(0), halo, tile(1), tile(0), halo, tile(0), _whole((CONV_A, d)), vec, hw, vec, hw, vec, vec],
        out_specs=[tile(0), tile(0), _whole((CONV_A, d)), vec, vec, vec, vec, hw, hw],
        out_shape=[jax.ShapeDtypeStruct((t, d), BF16)] * 2 + [jax.ShapeDtypeStruct((CONV_A, d), F32)]
        + [jax.ShapeDtypeStruct((1, d), F32)] * 4 + [jax.ShapeDtypeStruct((nh, HEAD_DIM, HEAD_DIM), F32)] * 2,
        scratch_shapes=[big, big, big, full, full, full, full, pltpu.VMEM((1, d), F32), pltpu.VMEM((HALO_A, d), F32)],
        compiler_params=_params("arbitrary"), name="branch_a_bwd")(z, z, z, hl, hl, dya, wa, ba, wr, br, wi, bi, lam)


def _rows_block(rows, cols, n_arrays=1):
    cap = max(8, BLOCK_BYTES // (4 * cols * n_arrays))
    return rows if rows <= cap else _pick(rows, cap, 8)


def _sum_halves(g, xo, c_idx):
    nq, r, c = g.shape
    rh = r // 2
    tr = _rows_block(rh, c)
    nb = rh // tr

    def body(c_ref, g_ref, x_ref, o_ref):
        o_ref[...] = (g_ref[...] + x_ref[...]).astype(BF16)

    grid_spec = pltpu.PrefetchScalarGridSpec(
        num_scalar_prefetch=1, grid=(nq, nb),
        in_specs=[pl.BlockSpec((None, tr, c), lambda q, i, cr: (q, cr[0] * nb + i, 0)),
                  pl.BlockSpec((None, tr, c), lambda q, i, cr: (q, i, 0))],
        out_specs=pl.BlockSpec((None, tr, c), lambda q, i, cr: (q, i, 0)))
    return pl.pallas_call(
        body, grid_spec=grid_spec, out_shape=jax.ShapeDtypeStruct((nq, rh, c), BF16),
        compiler_params=_params("parallel", "parallel"), name="sum_halves")(c_idx, g, xo)


def _sum_slots(y, name):
    ns, r, c = y.shape
    tr = _rows_block(r, c, ns)

    def body(y_ref, o_ref):
        acc = y_ref[0].astype(F32)
        for s in range(1, ns):
            acc = acc + y_ref[s].astype(F32)
        o_ref[...] = acc

    return pl.pallas_call(
        body, grid=(r // tr,), in_specs=[pl.BlockSpec((ns, tr, c), lambda i: (0, i, 0))],
        out_specs=pl.BlockSpec((tr, c), lambda i: (i, 0)), out_shape=jax.ShapeDtypeStruct((r, c), F32),
        compiler_params=_params("parallel"), name=name)(y)


def _adamw(w, g, m, v, name):
    shape = w.shape
    c = shape[-1]
    r = w.size // c
    tr = _rows_block(r, c)
    c1 = 1.0 / (1.0 - ADAM_B1 ** ADAM_STEP)
    c2 = 1.0 / (1.0 - ADAM_B2 ** ADAM_STEP)

    def body(w_ref, g_ref, m_ref, v_ref, go_ref, d_ref, mo_ref, vo_ref):
        gv = g_ref[...]
        mn = ADAM_B1 * m_ref[...] + (1.0 - ADAM_B1) * gv
        vn = ADAM_B2 * v_ref[...] + (1.0 - ADAM_B2) * (gv * gv)
        go_ref[...] = gv
        mo_ref[...] = mn
        vo_ref[...] = vn
        d_ref[...] = -ADAM_LR * ((mn * c1) / (jnp.sqrt(vn * c2) + ADAM_EPS) + ADAM_WD * w_ref[...])

    blk = pl.BlockSpec((tr, c), lambda i: (i, 0))
    outs = pl.pallas_call(
        body, grid=(r // tr,), in_specs=[blk] * 4, out_specs=[blk] * 4,
        out_shape=[jax.ShapeDtypeStruct((r, c), F32)] * 4,
        compiler_params=_params("parallel"), name=name)(*(a.reshape(r, c) for a in (w, g, m, v)))
    return tuple(o.reshape(shape) for o in outs)


def _adamw_layers(w, mine, theirs, m, v, c_idx, name):
    depth, r, c = w.shape
    tr = _rows_block(r // 2, c, 2)
    nbh = r // 2 // tr
    c1 = 1.0 / (1.0 - ADAM_B1 ** ADAM_STEP)
    c2 = 1.0 / (1.0 - ADAM_B2 ** ADAM_STEP)

    def body(c_ref, w_ref, m_ref, v_ref, *rest):
        g_refs = rest[:2 * depth]
        go_ref, d_ref, mo_ref, vo_ref = rest[2 * depth:]
        layer = pl.program_id(0)
        own = (pl.program_id(1) // nbh) == c_ref[0]

        def update(g_ref):
            gv = g_ref[...]
            mn = ADAM_B1 * m_ref[...] + (1.0 - ADAM_B1) * gv
            vn = ADAM_B2 * v_ref[...] + (1.0 - ADAM_B2) * (gv * gv)
            go_ref[...] = gv
            mo_ref[...] = mn
            vo_ref[...] = vn
            d_ref[...] = -ADAM_LR * ((mn * c1) / (jnp.sqrt(vn * c2) + ADAM_EPS) + ADAM_WD * w_ref[...])

        for s in range(depth):
            @pl.when((layer == s) & own)
            def _(s=s):
                update(g_refs[2 * s])

            @pl.when((layer == s) & jnp.logical_not(own))
            def _(s=s):
                update(g_refs[2 * s + 1])

    stacked = pl.BlockSpec((None, tr, c), lambda l, i, cr: (l, i, 0))

    def g_spec(s, is_mine):
        def imap(l, i, cr):
            own = (i // nbh) == cr[0]
            on = (l == s) & (own if is_mine else jnp.logical_not(own))
            return (jnp.where(on, i % nbh, 0), 0)
        return pl.BlockSpec((tr, c), imap)

    g_specs = [g_spec(s, is_mine) for s in range(depth) for is_mine in (True, False)]
    gs = [a for s in range(depth) for a in (mine[s], theirs[s])]
    grid_spec = pltpu.PrefetchScalarGridSpec(
        num_scalar_prefetch=1, grid=(depth, 2 * nbh), in_specs=[stacked] * 3 + g_specs, out_specs=[stacked] * 4)
    return pl.pallas_call(
        body, grid_spec=grid_spec, out_shape=[jax.ShapeDtypeStruct(w.shape, F32)] * 4,
        compiler_params=_params("arbitrary", "arbitrary"), name=name)(c_idx, w, m, v, *gs)


HBM_SPEC = pl.BlockSpec(memory_space=pltpu.HBM)


def _place():
    return lax.axis_index("x"), lax.axis_index("y"), lax.axis_index("c")


def _other_chips(x, y):
    return [(1 - x, y), (x, 1 - y), (1 - x, 1 - y)]


def _half(ref, which):
    rows = ref.shape[0] // 2
    return ref.at[pl.ds(which * rows, rows)]


def _gather_chips(arrays, name):
    partial = _run_rider(_gather_ici_rider(arrays), name + "_ici")
    return _run_rider(_gather_d2d_rider(partial), name + "_d2d")


class _Rider:
    def __init__(self, operands, out_shape, scratch, start, finish, aliases=None):
        self.operands, self.out_shape, self.scratch = list(operands), list(out_shape), list(scratch)
        self.start, self.finish, self.aliases = start, finish, dict(aliases or {})


def _run_rider(rider, name):
    n_in, n_out = len(rider.operands), len(rider.out_shape)

    def body(*refs):
        ins, outs, sems = refs[:n_in], refs[n_in:n_in + n_out], refs[n_in + n_out:]
        rider.start(ins, outs, sems)
        rider.finish(ins, outs, sems)

    return pl.pallas_call(
        body, in_specs=[HBM_SPEC] * n_in, out_specs=[HBM_SPEC] * n_out, out_shape=rider.out_shape,
        scratch_shapes=rider.scratch, input_output_aliases=rider.aliases, name=name)(*rider.operands)


def _hosted(body, rider, *, grid, in_specs, out_specs, out_shape, scratch_shapes, name):
    n_in, n_out, n_scr = len(in_specs), len(out_specs), len(scratch_shapes)
    r_in, r_out = len(rider.operands), len(rider.out_shape)

    def wrapped(*refs):
        ins, refs = refs[:n_in], refs[n_in:]
        r_ins, refs = refs[:r_in], refs[r_in:]
        outs, refs = refs[:n_out], refs[n_out:]
        r_outs, refs = refs[:r_out], refs[r_out:]
        scr, sems = refs[:n_scr], refs[n_scr:]
        first = functools.reduce(jnp.logical_and, [pl.program_id(a) == 0 for a in range(len(grid))])
        last = functools.reduce(jnp.logical_and, [pl.program_id(a) == g - 1 for a, g in enumerate(grid)])

        @pl.when(first)
        def _():
            rider.start(r_ins, r_outs, sems)
        body(*ins, *outs, *scr)

        @pl.when(last)
        def _():
            rider.finish(r_ins, r_outs, sems)

    res = pl.pallas_call(
        wrapped, grid=grid, in_specs=list(in_specs) + [HBM_SPEC] * r_in, out_specs=list(out_specs) + [HBM_SPEC] * r_out,
        out_shape=list(out_shape) + rider.out_shape, scratch_shapes=list(scratch_shapes) + rider.scratch,
        input_output_aliases={n_in + i: n_out + o for i, o in rider.aliases.items()},
        compiler_params=_params(*(["arbitrary"] * len(grid))), name=name)
    return lambda *operands: (lambda r: (r[:n_out], r[n_out:]))(res(*operands, *rider.operands))


def _gather_ici_rider(arrays):
    n = len(arrays)

    def copies(ins, outs, sems):
        send, recv, lsem = sems
        x, y, c = _place()
        me = 2 * x + y
        chips = _other_chips(x, y)
        local, sends, arrivals = [], [], []
        for a in range(n):
            rows = ins[a].shape[0] // LOCAL_CHUNKS
            for k in range(LOCAL_CHUNKS):
                part = pl.ds(k * rows, rows)
                local.append(pltpu.make_async_copy(ins[a].at[part], outs[a].at[me].at[part], lsem.at[a, k]))
            for j in range(3):
                q = 2 * chips[j][0] + chips[j][1]
                for dst, group in ((_half(outs[a].at[me], c), sends), (_half(outs[a].at[q], c), arrivals)):
                    group.append(pltpu.make_async_remote_copy(
                        src_ref=_half(ins[a], c), dst_ref=dst, send_sem=send.at[a, j], recv_sem=recv.at[a, j],
                        device_id=(*chips[j], c), device_id_type=MESH))
        return local, sends, arrivals

    def start(ins, outs, sems):
        local, sends, _ = copies(ins, outs, sems)
        for cp in local + sends:
            cp.start()

    def finish(ins, outs, sems):
        local, sends, arrivals = copies(ins, outs, sems)
        for cp in arrivals:
            cp.wait_recv()
        for cp in sends:
            cp.wait_send()
        for cp in local:
            cp.wait()

    sems = pltpu.SemaphoreType.DMA((n, 3))
    return _Rider(arrays, [jax.ShapeDtypeStruct((N_CHIPS,) + a.shape, a.dtype) for a in arrays],
                  [sems, sems, pltpu.SemaphoreType.DMA((n, LOCAL_CHUNKS))], start, finish)


def _gather_d2d_rider(partial):
    n = len(partial)

    def copies(outs, sems):
        send, recv = sems
        x, y, c = _place()
        chips = _other_chips(x, y)
        sends, arrivals = [], []
        for a in range(n):
            for j in range(3):
                q = 2 * chips[j][0] + chips[j][1]
                for which, group in ((c, sends), (1 - c, arrivals)):
                    ref = _half(outs[a].at[q], which)
                    group.append(pltpu.make_async_remote_copy(
                        src_ref=ref, dst_ref=ref, send_sem=send.at[a, j], recv_sem=recv.at[a, j],
                        device_id=(x, y, 1 - c), device_id_type=MESH))
        return sends, arrivals

    def start(ins, outs, sems):
        for cp in copies(outs, sems)[0]:
            cp.start()

    def finish(ins, outs, sems):
        sends, arrivals = copies(outs, sems)
        for cp in arrivals:
            cp.wait_recv()
        for cp in sends:
            cp.wait_send()

    sems = pltpu.SemaphoreType.DMA((n, 3))
    return _Rider(partial, [jax.ShapeDtypeStruct(a.shape, a.dtype) for a in partial], [sems, sems], start, finish,
                  aliases={k: k for k in range(n)})


def _swap_halves(grads, name):
    n = len(grads)

    def body(*refs):
        ins, outs = refs[:n], refs[n:2 * n]
        send, recv = refs[2 * n:]
        x, y, c = _place()
        copies = []
        for a in range(n):
            rows = ins[a].shape[1] // 2
            copies.append(pltpu.make_async_remote_copy(
                src_ref=ins[a].at[:, pl.ds((1 - c) * rows, rows)], dst_ref=outs[a],
                send_sem=send.at[a], recv_sem=recv.at[a], device_id=(x, y, 1 - c), device_id_type=MESH))
        for cp in copies:
            cp.start()
        for cp in copies:
            cp.wait()

    sems = pltpu.SemaphoreType.DMA((n,))
    return pl.pallas_call(
        body, in_specs=[HBM_SPEC] * n, out_specs=[HBM_SPEC] * n,
        out_shape=[jax.ShapeDtypeStruct((g.shape[0], g.shape[1] // 2) + g.shape[2:], g.dtype) for g in grads],
        scratch_shapes=[sems, sems], name=name)(*grads)


def _scatter_rider(parts):
    n = len(parts)

    def copies(ins, outs, sems):
        send, recv, lsem = sems
        x, y, c = _place()
        me = 2 * x + y
        chips = _other_chips(x, y)
        local = [pltpu.make_async_copy(ins[a].at[me], outs[a].at[me], lsem.at[a]) for a in range(n)]
        sends, arrivals = [], []
        for a in range(n):
            for j in range(3):
                q = 2 * chips[j][0] + chips[j][1]
                for dst, group in ((outs[a].at[me], sends), (outs[a].at[q], arrivals)):
                    group.append(pltpu.make_async_remote_copy(
                        src_ref=ins[a].at[q], dst_ref=dst, send_sem=send.at[a, j], recv_sem=recv.at[a, j],
                        device_id=(*chips[j], c), device_id_type=MESH))
        return local, sends, arrivals

    def start(ins, outs, sems):
        local, sends, _ = copies(ins, outs, sems)
        for cp in local + sends:
            cp.start()

    def finish(ins, outs, sems):
        local, sends, arrivals = copies(ins, outs, sems)
        for cp in arrivals:
            cp.wait_recv()
        for cp in sends:
            cp.wait_send()
        for cp in local:
            cp.wait()

    sems = pltpu.SemaphoreType.DMA((n, 3))
    return _Rider(parts, [jax.ShapeDtypeStruct(p.shape, p.dtype) for p in parts],
                  [sems, sems, pltpu.SemaphoreType.DMA((n,))], start, finish)


def _swap_sibling(arrays, name):
    n = len(arrays)

    def body(*refs):
        ins, outs = refs[:n], refs[n:2 * n]
        send, recv = refs[2 * n:]
        x, y, c = _place()
        copies = [pltpu.make_async_remote_copy(
            src_ref=ins[a], dst_ref=outs[a], send_sem=send.at[a], recv_sem=recv.at[a],
            device_id=(x, y, 1 - c), device_id_type=MESH) for a in range(n)]
        for cp in copies:
            cp.start()
        for cp in copies:
            cp.wait()

    sems = pltpu.SemaphoreType.DMA((n,))
    return pl.pallas_call(
        body, in_specs=[HBM_SPEC] * n, out_specs=[HBM_SPEC] * n,
        out_shape=[jax.ShapeDtypeStruct(a.shape, a.dtype) for a in arrays],
        scratch_shapes=[sems, sems], name=name)(*arrays)


def _gather_all(arrays, name):
    n = len(arrays)

    def body(*refs):
        ins, outs = refs[:n], refs[n:2 * n]
        send, recv, lsem = refs[2 * n:]
        x, y, c = _place()
        chips = _other_chips(x, y)
        sibling = (x, y, 1 - c)

        def slot(a, px, py, pc):
            return outs[a].at[4 * px + 2 * py + pc]

        def copy(a, k, block, to, src=None):
            return pltpu.make_async_remote_copy(
                src_ref=slot(a, *block) if src is None else src, dst_ref=slot(a, *block),
                send_sem=send.at[a, k], recv_sem=recv.at[a, k], device_id=to, device_id_type=MESH)

        local = [pltpu.make_async_copy(ins[a], slot(a, x, y, c), lsem.at[a]) for a in range(n)]
        for cp in local:
            cp.start()
        sent = []
        for a in range(n):
            sent.append(copy(a, 0, (x, y, c), sibling, src=ins[a]))
            sent += [copy(a, 1 + j, (x, y, c), (*chip, c), src=ins[a]) for j, chip in enumerate(chips)]
        for cp in sent:
            cp.start()
        for a in range(n):
            for j, chip in enumerate(chips):
                copy(a, 1 + j, (*chip, c), (x, y, c)).wait_recv()
                cp = copy(a, 4 + j, (*chip, c), sibling)
                cp.start()
                sent.append(cp)
        for a in range(n):
            copy(a, 0, sibling, (x, y, c)).wait_recv()
            for j, chip in enumerate(chips):
                copy(a, 4 + j, (*chip, 1 - c), (x, y, c)).wait_recv()
        for cp in sent:
            cp.wait_send()
        for cp in local:
            cp.wait()

    sems = pltpu.SemaphoreType.DMA((n, 7))
    return pl.pallas_call(
        body, in_specs=[HBM_SPEC] * n, out_specs=[HBM_SPEC] * n,
        out_shape=[jax.ShapeDtypeStruct((N_DEV,) + a.shape, a.dtype) for a in arrays],
        scratch_shapes=[sems, sems, pltpu.SemaphoreType.DMA((n,))], name=name)(*arrays)


def _layer_fwd(h, p, next_shards):
    ride = None if next_shards is None else _gather_ici_rider(next_shards)
    (z, hn), partial = _in_proj_fwd(h, p["norm_g"], p["w_in"], ride)
    ya, hl = _branch_a_fwd(z, p["conv_a_w"], p["conv_a_b"], p["w_rg"], p["b_rg"], p["w_ig"], p["b_ig"], p["lam"])
    ride = None if partial is None else _gather_d2d_rider(partial)
    (yb, cb), gathered = _branch_b_fwd(z, p["conv_b_w"], p["conv_b_b"], p["ln_b_g"], p["ln_b_b"], ride)
    merged, pa, pb = _merge_fwd(ya, yb, z, p["w_pa"], p["w_pb"])
    return _out_fwd(h, merged, p["w_out"]), (h, hn, z, hl, cb, ya, yb, pa, pb, merged), gathered


def _layer_bwd(dho, p, saved, chip_sums):
    h, hn, z, hl, cb, ya, yb, pa, pb, merged = saved
    dpa, dpb, dma, dmb = _dmerged_bwd(dho, p["w_out"], z, pa, pb)
    g = {"w_out": _matmul_tn(merged, dho, "dw_out"), "w_proj_a": _matmul_tn(ya, dpa, "dw_proj_a"),
         "w_proj_b": _matmul_tn(yb, dpb, "dw_proj_b")}
    dya = _matmul_nt(dpa, p["w_pa"], "dya")
    dyb = _matmul_nt(dpb, p["w_pb"], "dyb")
    (dxa, dsa, g["conv_a_w"], g["conv_a_b"], g["b_rg"], g["b_ig"], g["lru_lambda"], g["w_rg"], g["w_ig"]) = \
        _branch_a_bwd(z, hl, dya, p["conv_a_w"], p["conv_a_b"], p["w_rg"], p["b_rg"], p["w_ig"], p["b_ig"], p["lam"])
    dvb, dgb, dsb, g["conv_b_w"], g["conv_b_b"], g["ln_b_g"], g["ln_b_b"] = \
        _branch_b_bwd(z, cb, dyb, p["conv_b_w"], p["ln_b_g"], p["ln_b_b"])
    dzs = (dxa, dsa, dvb, dgb, dsb, dma, dmb)
    (g["w_in"],), landed = _dwin_bwd(hn, dzs, None if chip_sums is None else _scatter_rider(chip_sums))
    dh, g["norm_g"] = _din_bwd(dzs, p["w_in"], h, p["norm_g"], dho)
    return dh, g, landed


BIG = ("w_in", "w_proj_a", "w_proj_b", "w_out")
VECS = ("norm_g", "conv_a_b", "b_rg", "b_ig", "lru_lambda", "conv_b_b", "ln_b_g", "ln_b_b")


def _reduce_start(g, c_idx):
    d = g["w_out"].shape[0]
    parts = [g["w_in"]] + [g[k].reshape(N_CHIPS, d // N_CHIPS, d) for k in BIG[1:]]
    other = _swap_halves(parts, "swap_halves")
    return [_sum_halves(a, b, c_idx) for a, b in zip(parts, other)]


def _reduce_finish(landed):
    mine = [_sum_slots(y, "sum_chips") for y in landed]
    return mine, _swap_sibling(mine, "swap_sibling")


def _forward_backward(h, target, layers, shards, final_g, c_idx):
    depth = len(layers)
    d = h.shape[1]
    gathered = _gather_chips(shards[0], "gather_weights")
    params, saved = [], []
    for l in range(depth):
        win, wpa, wpb, wo = gathered
        params.append({**layers[l], "w_in": win, "w_pa": wpa.reshape(d, d), "w_pb": wpb.reshape(d, d),
                       "w_out": wo.reshape(d, d)})
        h, s, gathered = _layer_fwd(h, params[l], shards[l + 1] if l + 1 < depth else None)
        saved.append(s)
    target = jnp.concatenate([jnp.zeros((N_META, d), F32), target], axis=0)
    loss_lanes, dh, g_final = _loss_head(h, final_g, target)
    grads, reduced, chip_sums = [None] * depth, [None] * depth, None
    for l in reversed(range(depth)):
        dh, grads[l], landed = _layer_bwd(dh, params[l], saved[l], chip_sums)
        if landed is not None:
            reduced[l + 1] = _reduce_finish(landed)
        chip_sums = _reduce_start(grads[l], c_idx)
    reduced[0] = _reduce_finish(_run_rider(_scatter_rider(chip_sums), "scatter_chips"))
    return loss_lanes, dh, g_final, grads, reduced


def kernel(x, meta, norm_g, w_in, conv_a_w, conv_a_b, w_rg, b_rg, w_ig, b_ig, lru_lambda, conv_b_w, conv_b_b, ln_b_g, ln_b_b, w_proj_a, w_proj_b, w_out, final_g, loss_target, m_meta, m_norm_g, m_w_in, m_conv_a_w, m_conv_a_b, m_w_rg, m_b_rg, m_w_ig, m_b_ig, m_lru_lambda, m_conv_b_w, m_conv_b_b, m_ln_b_g, m_ln_b_b, m_w_proj_a, m_w_proj_b, m_w_out, m_final_g, v_meta, v_norm_g, v_w_in, v_conv_a_w, v_conv_a_b, v_w_rg, v_b_rg, v_w_ig, v_b_ig, v_lru_lambda, v_conv_b_w, v_conv_b_b, v_ln_b_g, v_ln_b_b, v_w_proj_a, v_w_proj_b, v_w_out, v_final_g):
    names = ("meta", "norm_g", "w_in", "conv_a_w", "conv_a_b", "w_rg", "b_rg", "w_ig", "b_ig", "lru_lambda",
             "conv_b_w", "conv_b_b", "ln_b_g", "ln_b_b", "w_proj_a", "w_proj_b", "w_out", "final_g")
    w = dict(zip(names, (meta, norm_g, w_in, conv_a_w, conv_a_b, w_rg, b_rg, w_ig, b_ig, lru_lambda, conv_b_w,
                         conv_b_b, ln_b_g, ln_b_b, w_proj_a, w_proj_b, w_out, final_g)))
    m = dict(zip(names, (m_meta, m_norm_g, m_w_in, m_conv_a_w, m_conv_a_b, m_w_rg, m_b_rg, m_w_ig, m_b_ig,
                         m_lru_lambda, m_conv_b_w, m_conv_b_b, m_ln_b_g, m_ln_b_b, m_w_proj_a, m_w_proj_b, m_w_out,
                         m_final_g)))
    v = dict(zip(names, (v_meta, v_norm_g, v_w_in, v_conv_a_w, v_conv_a_b, v_w_rg, v_b_rg, v_w_ig, v_b_ig,
                         v_lru_lambda, v_conv_b_w, v_conv_b_b, v_ln_b_g, v_ln_b_b, v_w_proj_a, v_w_proj_b, v_w_out,
                         v_final_g)))
    depth, d = norm_g.shape
    dq = d // N_CHIPS
    px, py, pc = _place()
    chip = 2 * px + py
    c_idx = pc.astype(jnp.int32).reshape(1)

    meta_s, conv_a_s, conv_b_s = _gather_chips([meta, conv_a_w, conv_b_w], "gather_small")
    unshard = lambda s: jnp.concatenate([s[q] for q in range(N_CHIPS)], axis=-1)
    meta_f, conv_a_f, conv_b_f = unshard(meta_s), unshard(conv_a_s), unshard(conv_b_s)
    layers, shards = [], []
    for l in range(depth):
        shards.append([w_in[l].astype(BF16), w_proj_a[l].astype(BF16), w_proj_b[l].astype(BF16),
                       w_out[l].astype(BF16)])
        layers.append({
            "norm_g": norm_g[l][None], "conv_a_w": conv_a_f[l], "conv_a_b": conv_a_b[l][None],
            "w_rg": w_rg[l].astype(BF16), "b_rg": b_rg[l][None], "w_ig": w_ig[l].astype(BF16), "b_ig": b_ig[l][None],
            "lam": lru_lambda[l][None], "conv_b_w": conv_b_f[l], "conv_b_b": conv_b_b[l][None],
            "ln_b_g": ln_b_g[l][None], "ln_b_b": ln_b_b[l][None]})

    loss_lanes, dh, g_final, small, reduced = _forward_backward(
        jnp.concatenate([meta_f, x[0]], axis=0), loss_target[0], layers, shards, final_g[None], c_idx)
    loss = lax.psum(jnp.sum(loss_lanes), ("x", "y", "c"))
    grads = {}

    vec_rows = [small[l][k] for k in VECS for l in range(depth)]
    vec_rows += [small[l]["conv_a_w"] for l in range(depth)] + [small[l]["conv_b_w"] for l in range(depth)]
    vec_rows += [dh[:N_META], g_final]
    packed_v = jnp.concatenate(vec_rows, axis=0)
    pad = (-packed_v.shape[0]) % 8
    packed_v = jnp.concatenate([packed_v, jnp.zeros((pad, d), F32)], axis=0)
    packed_g = jnp.concatenate([small[l][k].reshape(-1, HEAD_DIM) for k in ("w_rg", "w_ig") for l in range(depth)],
                               axis=0)
    all_v, all_g = _gather_all([packed_v, packed_g], "gather_small_grads")
    sum_v = _sum_slots(all_v, "sum_devices_v")
    sum_g = _sum_slots(all_g, "sum_devices_g")
    cols = lambda a: lax.dynamic_slice_in_dim(a, chip * dq, dq, axis=-1)
    for n_k, k in enumerate(VECS):
        grads[k] = sum_v[n_k * depth:(n_k + 1) * depth]
    o = len(VECS) * depth
    grads["conv_a_w"] = cols(sum_v[o:o + depth * CONV_A].reshape(depth, CONV_A, d))
    o += depth * CONV_A
    grads["conv_b_w"] = cols(sum_v[o:o + depth * CONV_B].reshape(depth, CONV_B, d))
    o += depth * CONV_B
    grads["meta"] = cols(sum_v[o:o + N_META])
    grads["final_g"] = sum_v[o + N_META]
    gate_rows = depth * (d // HEAD_DIM) * HEAD_DIM
    grads["w_rg"] = sum_g[:gate_rows].reshape(w_rg.shape)
    grads["w_ig"] = sum_g[gate_rows:].reshape(w_ig.shape)

    outs = {k: _adamw(w[k], grads[k], m[k], v[k], f"adamw_{k}") for k in names if k not in BIG}
    for n_k, k in enumerate(BIG):
        outs[k] = _adamw_layers(w[k], [reduced[l][0][n_k] for l in range(depth)],
                                [reduced[l][1][n_k] for l in range(depth)], m[k], v[k], c_idx, f"adamw_{k}")
    dx = dh[N_META:][None]
    return (loss, dx, *[outs[k][0] for k in names], *[outs[k][1] for k in names],
            *[outs[k][2] for k in names], *[outs[k][3] for k in names])
```

```python
import functools

import jax
import jax.numpy as jnp
from jax import lax
from jax.experimental import pallas as pl
from jax.experimental.pallas import tpu as pltpu

F32 = jnp.float32
BF16 = jnp.bfloat16
MESH = pl.DeviceIdType.MESH

EPS = 1e-6
N_META = 16
HEAD_DIM = 128
CONV_A = 4
CONV_B = 31
LRU_C = 8.0
N_SPLIT = 7
N_CHIPS = 4
N_DEV = 8

ADAM_LR = 0.001
ADAM_B1 = 0.9
ADAM_B2 = 0.999
ADAM_EPS = 1e-08
ADAM_WD = 0.01
ADAM_STEP = 10

HALO_A = 16
HALO_B = 48
ROW_TILE_MM = 912
ROW_TILE_EW = 144
ROW_BLOCK = 16
LANE_CHUNK = 256
COL_TILE = 512
COL_TILE_WIDE = 1024
COL_TILE_IN = 1792
VMEM_LIMIT = 56 * 1024 * 1024
BLOCK_BYTES = 2 * 1024 * 1024

NT_DIMS = (((1,), (1,)), ((), ()))
TN_DIMS = (((0,), (0,)), ((), ()))


def _pick(n, cap, mult):
    best = None
    for d in range(mult, min(n, cap) + 1, mult):
        if n % d == 0:
            best = d
    if best is None:
        raise ValueError(f"no tile for {n} (cap {cap}, multiple of {mult})")
    return best


def _params(*semantics):
    return pltpu.CompilerParams(dimension_semantics=semantics, vmem_limit_bytes=VMEM_LIMIT)


def _whole(shape):
    return pl.BlockSpec(shape, lambda *_: (0,) * len(shape))


def _sigmoid(v):
    return 0.5 * jnp.tanh(0.5 * v) + 0.5


def _dsilu(v, s):
    return s * (1.0 + v * (1.0 - s))


def _log1p(e):
    w = 1.0 + e
    return jnp.where(w == 1.0, e, e * jnp.log(w) / (w - 1.0))


def _softplus(v):
    return jnp.maximum(v, 0.0) + _log1p(jnp.exp(-jnp.abs(v)))


def _neg_expm1_twice(v):
    t = jnp.tanh(v)
    return -2.0 * t / (1.0 - t)


def _row_iota(shape):
    return lax.broadcasted_iota(jnp.int32, shape, 0)


def _call(body, rider, operands, *, grid, in_specs, out_specs, out_shape, scratch_shapes, semantics, name):
    if rider is None:
        return pl.pallas_call(
            body, grid=grid, in_specs=in_specs, out_specs=out_specs, out_shape=out_shape,
            scratch_shapes=scratch_shapes, compiler_params=_params(*semantics), name=name)(*operands), None
    return _hosted(body, rider, grid=grid, in_specs=in_specs, out_specs=out_specs, out_shape=out_shape,
                   scratch_shapes=scratch_shapes, name=name)(*operands)


def _in_proj_fwd(h, g, w, rider=None):
    t, d = h.shape
    nq = w.shape[2]
    tm = _pick(t, ROW_TILE_MM, 16)
    tn = _pick(nq, COL_TILE_IN, 128)
    bps = nq // tn

    def body(h_ref, g_ref, w_ref, z_ref, hn_ref):
        @pl.when(pl.program_id(1) == 0)
        def _():
            def norm(r0):
                rows = _rows(r0)
                v = h_ref[rows, :]
                rinv = lax.rsqrt(jnp.mean(v * v, axis=-1, keepdims=True) + EPS)
                hn_ref[rows, :] = (v * rinv * g_ref[...]).astype(BF16)
            _row_blocks(tm, norm)
        z_ref[...] = jnp.dot(hn_ref[...], w_ref[...], preferred_element_type=F32)

    return _call(
        body, rider, (h, g, w), grid=(t // tm, N_CHIPS * bps),
        in_specs=[pl.BlockSpec((tm, d), lambda i, j: (i, 0), pipeline_mode=pl.Buffered(1)), _whole((1, d)),
                  pl.BlockSpec((None, d, tn), lambda i, j: (j // bps, 0, j % bps))],
        out_specs=[pl.BlockSpec((tm, tn), lambda i, j: (i, j)), pl.BlockSpec((tm, d), lambda i, j: (i, 0))],
        out_shape=[jax.ShapeDtypeStruct((t, N_CHIPS * nq), F32), jax.ShapeDtypeStruct((t, d), BF16)],
        scratch_shapes=[], semantics=("parallel", "arbitrary"), name="in_proj_fwd")


def _conv_a(xext, wa_ref, ba, tt):
    acc = ba
    for k in range(CONV_A):
        acc = acc + wa_ref[k:k + 1, :] * xext[pl.ds(HALO_A - (CONV_A - 1) + k, tt), :]
    return acc


def _gates_a(ca, wr_ref, wi_ref, br, bi, gr_s, gi_s):
    cab = ca.astype(BF16)
    for hh in range(ca.shape[1] // HEAD_DIM):
        sl = slice(hh * HEAD_DIM, (hh + 1) * HEAD_DIM)
        gr_s[:, sl] = jnp.dot(cab[:, sl], wr_ref[hh], preferred_element_type=F32)
        gi_s[:, sl] = jnp.dot(cab[:, sl], wi_ref[hh], preferred_element_type=F32)
    return _sigmoid(gr_s[...] + br), _sigmoid(gi_s[...] + bi)


def _decay_a(r, sp):
    log_a = -LRU_C * r * sp
    a = jnp.exp(log_a)
    mult = jnp.sqrt(jnp.maximum(_neg_expm1_twice(log_a), 0.0))
    return a, mult


def _scan_fwd(a_ref, u_ref, out_ref, h0, nblk):
    def blk(b, hprev):
        rows = pl.ds(pl.multiple_of(b * 8, 8), 8)
        ca, cb = a_ref[rows, :], u_ref[rows, :]
        row = _row_iota(ca.shape)
        for dist in (1, 2, 4):
            m = row >= dist
            cb = jnp.where(m, ca * pltpu.roll(cb, dist, 0) + cb, cb)
            ca = jnp.where(m, ca * pltpu.roll(ca, dist, 0), ca)
        hb = ca * hprev + cb
        out_ref[rows, :] = hb
        return hb[7:8, :]
    return lax.fori_loop(0, nblk, blk, h0)


def _scan_rev(a_ref, d_ref, out_ref, g0, nblk):
    def blk(k, g):
        rows = pl.ds(pl.multiple_of((nblk - 1 - k) * 8, 8), 8)
        a, cb = a_ref[rows, :], d_ref[rows, :]
        row = _row_iota(a.shape)
        ca = jnp.where(row == 7, 1.0, pltpu.roll(a, 7, 0))
        for dist in (1, 2, 4):
            m = row < 8 - dist
            cb = jnp.where(m, cb + ca * pltpu.roll(cb, 8 - dist, 0), cb)
            ca = jnp.where(m, ca * pltpu.roll(ca, 8 - dist, 0), ca)
        lam = cb + ca * g
        out_ref[rows, :] = lam
        return a[0:1, :] * lam[0:1, :]
    return lax.fori_loop(0, nblk, blk, g0)


def _branch_a_fwd(z, wa, ba, wr, br, wi, bi, lam):
    t = z.shape[0]
    d = wa.shape[1]
    tt = _pick(t, ROW_TILE_EW, HALO_B)
    hb = tt // HALO_A
    nh = d // HEAD_DIM

    def body(xa_ref, halo_ref, sa_ref, wa_ref, ba_ref, wr_ref, br_ref, wi_ref, bi_ref, lam_ref,
             ya_ref, hl_ref, xext, gr_s, gi_s, a_s, carry):
        i = pl.program_id(0)

        @pl.when(i == 0)
        def _():
            carry[...] = jnp.zeros_like(carry)
        xext[0:HALO_A, :] = jnp.where(i == 0, 0.0, halo_ref[...])
        xext[HALO_A:, :] = xa_ref[...]
        ca = _conv_a(xext, wa_ref, ba_ref[...], tt)
        r, ig = _gates_a(ca, wr_ref, wi_ref, br_ref[...], bi_ref[...], gr_s, gi_s)
        a, mult = _decay_a(r, _softplus(-lam_ref[...]))
        a_s[...] = a
        hl_ref[...] = mult * (ig * ca)
        carry[...] = _scan_fwd(a_s, hl_ref, hl_ref, carry[...], tt // 8)
        sa = sa_ref[...]
        ya_ref[...] = (hl_ref[...] * (sa * _sigmoid(sa))).astype(BF16)

    vec = _whole((1, d))
    return pl.pallas_call(
        body, grid=(t // tt,),
        in_specs=[pl.BlockSpec((tt, d), lambda i: (i, 0)),
                  pl.BlockSpec((HALO_A, d), lambda i: (jnp.maximum(i * hb - 1, 0), 0)),
                  pl.BlockSpec((tt, d), lambda i: (i, 1)),
                  _whole((CONV_A, d)), vec, _whole((nh, HEAD_DIM, HEAD_DIM)), vec,
                  _whole((nh, HEAD_DIM, HEAD_DIM)), vec, vec],
        out_specs=[pl.BlockSpec((tt, d), lambda i: (i, 0)), pl.BlockSpec((tt, d), lambda i: (i, 0))],
        out_shape=[jax.ShapeDtypeStruct((t, d), BF16), jax.ShapeDtypeStruct((t, d), F32)],
        scratch_shapes=[pltpu.VMEM((tt + HALO_A, d), F32), pltpu.VMEM((tt, d), F32), pltpu.VMEM((tt, d), F32),
                        pltpu.VMEM((tt, d), F32), pltpu.VMEM((1, d), F32)],
        compiler_params=_params("arbitrary"), name="branch_a_fwd")(z, z, z, wa, ba, wr, br, wi, bi, lam)


def _row_blocks(n_rows, body):
    n = n_rows // ROW_BLOCK
    unroll = 3 if n % 3 == 0 else 1

    def step(b, carry):
        for u in range(unroll):
            body(pl.multiple_of((b * unroll + u) * ROW_BLOCK, ROW_BLOCK))
        return carry
    lax.fori_loop(0, n // unroll, step, 0)


def _rows(r0, offset=0):
    return pl.ds(pl.multiple_of(r0 + offset, ROW_BLOCK), ROW_BLOCK)


def _glu_ext(gext, vb_ref, gb_ref, vbh_ref, gbh_ref, first, tt):
    def halo(r0):
        rows = _rows(r0)
        gext[rows, :] = jnp.where(first, 0.0, vbh_ref[rows, :] * _sigmoid(gbh_ref[rows, :]))
    _row_blocks(HALO_B, halo)

    def tile(r0):
        rows = _rows(r0)
        gext[_rows(r0, HALO_B), :] = vb_ref[rows, :] * _sigmoid(gb_ref[rows, :])
    _row_blocks(tt, tile)


def _shifted(src, dst, cs, n_rows):
    for b in range(1, 8):
        dst[b - 1, 0:n_rows, :] = src[pl.ds(b, n_rows), cs]


def _tap(src, shifted, cs, off, r0, n):
    b = off % 8
    if b == 0:
        return src[pl.ds(off + r0, n), cs]
    return shifted[b - 1, pl.ds(off - b + r0, n), :]


def _conv_rows(tt):
    return tt // 2 if tt % 16 == 0 else tt


def _conv_b_fwd(gext, gs, wb_ref, bb_ref, cb_s, tt, d):
    rc = _conv_rows(tt)
    for c in range(d // LANE_CHUNK):
        cs = slice(c * LANE_CHUNK, (c + 1) * LANE_CHUNK)
        _shifted(gext, gs, cs, tt + HALO_B - 8)
        for r0 in range(0, tt, rc):
            acc = jnp.broadcast_to(bb_ref[:, cs], (rc, LANE_CHUNK))
            for k in range(CONV_B):
                acc = acc + wb_ref[k:k + 1, cs] * _tap(gext, gs, cs, HALO_B - (CONV_B - 1) + k, r0, rc)
            cb_s[r0:r0 + rc, cs] = acc


def _layer_norm_stats(cb):
    mu = jnp.mean(cb, axis=-1, keepdims=True)
    xc = cb - mu
    rstd = lax.rsqrt(jnp.mean(xc * xc, axis=-1, keepdims=True) + EPS)
    return xc * rstd, rstd


def _branch_b_fwd(z, wb, bb, lg, lb, rider=None):
    t = z.shape[0]
    d = wb.shape[1]
    tt = _pick(t, ROW_TILE_EW, HALO_B)
    hb = tt // HALO_B

    def body(vb_ref, gb_ref, vbh_ref, gbh_ref, sb_ref, wb_ref, bb_ref, lg_ref, lb_ref, yb_ref, cb_s, gext, gs):
        _glu_ext(gext, vb_ref, gb_ref, vbh_ref, gbh_ref, pl.program_id(0) == 0, tt)
        _conv_b_fwd(gext, gs, wb_ref, bb_ref, cb_s, tt, d)

        def finish(r0):
            rows = _rows(r0)
            xhat, _ = _layer_norm_stats(cb_s[rows, :])
            ln = xhat * lg_ref[...] + lb_ref[...]
            sb = sb_ref[rows, :]
            yb_ref[rows, :] = (ln * _sigmoid(ln) * (sb * _sigmoid(sb))).astype(BF16)
        _row_blocks(tt, finish)

    vec = _whole((1, d))
    tile = lambda s: pl.BlockSpec((tt, d), lambda i: (i, s))
    halo = lambda s: pl.BlockSpec((HALO_B, d), lambda i: (jnp.maximum(i * hb - 1, 0), s))
    return _call(
        body, rider, (z, z, z, z, z, wb, bb, lg, lb), grid=(t // tt,),
        in_specs=[tile(2), tile(3), halo(2), halo(3), tile(4), _whole((CONV_B, d)), vec, vec, vec],
        out_specs=[pl.BlockSpec((tt, d), lambda i: (i, 0))] * 2,
        out_shape=[jax.ShapeDtypeStruct((t, d), BF16), jax.ShapeDtypeStruct((t, d), F32)],
        scratch_shapes=[pltpu.VMEM((tt + HALO_B, d), F32), pltpu.VMEM((7, tt + HALO_B - 8, LANE_CHUNK), F32)],
        semantics=("parallel",), name="branch_b_fwd")


def _merge_fwd(ya, yb, z, wpa, wpb):
    t, d = ya.shape
    tm = _pick(t, ROW_TILE_MM, 16)
    tn = _pick(d, COL_TILE, 128)
    nb = d // tn

    def body(ya_ref, yb_ref, ma_ref, mb_ref, wpa_ref, wpb_ref, mg_ref, pa_ref, pb_ref):
        pa = jnp.dot(ya_ref[...], wpa_ref[...], preferred_element_type=F32)
        pb = jnp.dot(yb_ref[...], wpb_ref[...], preferred_element_type=F32)
        mg_ref[...] = (_sigmoid(ma_ref[...]) * pa + _sigmoid(mb_ref[...]) * pb).astype(BF16)
        pa_ref[...] = pa.astype(BF16)
        pb_ref[...] = pb.astype(BF16)

    rows = pl.BlockSpec((tm, d), lambda i, j: (i, 0))
    wcol = pl.BlockSpec((d, tn), lambda i, j: (0, j))
    outb = pl.BlockSpec((tm, tn), lambda i, j: (i, j))
    return pl.pallas_call(
        body, grid=(t // tm, nb),
        in_specs=[rows, rows, pl.BlockSpec((tm, tn), lambda i, j: (i, 5 * nb + j)),
                  pl.BlockSpec((tm, tn), lambda i, j: (i, 6 * nb + j)), wcol, wcol],
        out_specs=[outb, outb, outb],
        out_shape=[jax.ShapeDtypeStruct((t, d), BF16)] * 3,
        compiler_params=_params("parallel", "arbitrary"), name="merge_fwd")(ya, yb, z, z, wpa, wpb)


def _out_fwd(h, merged, wout):
    t, d = h.shape
    tm = _pick(t, ROW_TILE_MM, 16)
    tn = _pick(d, COL_TILE_WIDE, 128)

    def body(h_ref, mg_ref, w_ref, o_ref):
        o_ref[...] = h_ref[...] + jnp.dot(mg_ref[...], w_ref[...], preferred_element_type=F32)

    return pl.pallas_call(
        body, grid=(t // tm, d // tn),
        in_specs=[pl.BlockSpec((tm, tn), lambda i, j: (i, j)), pl.BlockSpec((tm, d), lambda i, j: (i, 0)),
                  pl.BlockSpec((d, tn), lambda i, j: (0, j))],
        out_specs=pl.BlockSpec((tm, tn), lambda i, j: (i, j)),
        out_shape=jax.ShapeDtypeStruct((t, d), F32),
        compiler_params=_params("parallel", "arbitrary"), name="out_fwd")(h, merged, wout)


def _loss_head(h, g, target):
    t, d = h.shape
    tt = _pick(t, ROW_TILE_EW, 16)

    def body(h_ref, g_ref, tg_ref, loss_ref, dh_ref, dg_ref):
        i = pl.program_id(0)

        @pl.when(i == 0)
        def _():
            loss_ref[...] = jnp.zeros_like(loss_ref)
            dg_ref[...] = jnp.zeros_like(dg_ref)
        v = h_ref[...]
        g = g_ref[...]
        rinv = lax.rsqrt(jnp.mean(v * v, axis=-1, keepdims=True) + EPS)
        xh = v * rinv
        valid = (i * tt + _row_iota(v.shape)) >= N_META
        diff = jnp.where(valid, xh * g - tg_ref[...], 0.0)
        loss_ref[...] += (0.5 / d) * jnp.sum(diff * diff, axis=0, keepdims=True)
        dy = diff * (1.0 / d)
        dg_ref[...] += jnp.sum(dy * xh, axis=0, keepdims=True)
        dxh = dy * g
        dh_ref[...] = rinv * (dxh - xh * jnp.mean(dxh * xh, axis=-1, keepdims=True))

    tile = pl.BlockSpec((tt, d), lambda i: (i, 0))
    vec = _whole((1, d))
    return pl.pallas_call(
        body, grid=(t // tt,), in_specs=[tile, vec, tile], out_specs=[vec, tile, vec],
        out_shape=[jax.ShapeDtypeStruct((1, d), F32), jax.ShapeDtypeStruct((t, d), F32),
                   jax.ShapeDtypeStruct((1, d), F32)],
        compiler_params=_params("arbitrary"), name="loss_head")(h, g, target)


def _dmerged_bwd(dho, wout, z, pa, pb):
    t, d = dho.shape
    tm = _pick(t, ROW_TILE_MM, 16)
    tn = _pick(d, COL_TILE, 128)
    nb = d // tn

    def body(dho_ref, w_ref, ma_ref, mb_ref, pa_ref, pb_ref, dpa_ref, dpb_ref, dma_ref, dmb_ref, dho_s):
        @pl.when(pl.program_id(1) == 0)
        def _():
            dho_s[...] = dho_ref[...].astype(BF16)
        dm = lax.dot_general(dho_s[...], w_ref[...], NT_DIMS, preferred_element_type=F32)
        sa = _sigmoid(ma_ref[...])
        sb = _sigmoid(mb_ref[...])
        dpa_ref[...] = (dm * sa).astype(BF16)
        dpb_ref[...] = (dm * sb).astype(BF16)
        dma_ref[...] = (dm * pa_ref[...].astype(F32) * (sa * (1.0 - sa))).astype(BF16)
        dmb_ref[...] = (dm * pb_ref[...].astype(F32) * (sb * (1.0 - sb))).astype(BF16)

    blk = pl.BlockSpec((tm, tn), lambda i, j: (i, j))
    return pl.pallas_call(
        body, grid=(t // tm, nb),
        in_specs=[pl.BlockSpec((tm, d), lambda i, j: (i, 0)), pl.BlockSpec((tn, d), lambda i, j: (j, 0)),
                  pl.BlockSpec((tm, tn), lambda i, j: (i, 5 * nb + j)),
                  pl.BlockSpec((tm, tn), lambda i, j: (i, 6 * nb + j)), blk, blk],
        out_specs=[blk, blk, blk, blk],
        out_shape=[jax.ShapeDtypeStruct((t, d), BF16)] * 4,
        scratch_shapes=[pltpu.VMEM((tm, d), BF16)],
        compiler_params=_params("parallel", "arbitrary"), name="dmerged_bwd")(dho, wout, z, z, pa, pb)


def _matmul_nt(a, w, name):
    t, k = a.shape
    n = w.shape[0]
    tm = _pick(t, ROW_TILE_MM, 16)
    tn = _pick(n, COL_TILE_WIDE, 128)

    def body(a_ref, w_ref, o_ref):
        o_ref[...] = lax.dot_general(a_ref[...], w_ref[...], NT_DIMS, preferred_element_type=F32)

    return pl.pallas_call(
        body, grid=(t // tm, n // tn),
        in_specs=[pl.BlockSpec((tm, k), lambda i, j: (i, 0)), pl.BlockSpec((tn, k), lambda i, j: (j, 0))],
        out_specs=pl.BlockSpec((tm, tn), lambda i, j: (i, j)),
        out_shape=jax.ShapeDtypeStruct((t, n), F32),
        compiler_params=_params("parallel", "arbitrary"), name=name)(a, w)


def _matmul_tn(a, b, name):
    t, m = a.shape
    n = b.shape[1]
    tk = _pick(t, ROW_TILE_MM, 16)
    tn = _pick(n, COL_TILE_WIDE, 128)

    def body(a_ref, b_ref, o_ref):
        @pl.when(pl.program_id(1) == 0)
        def _():
            o_ref[...] = jnp.zeros_like(o_ref)
        o_ref[...] += lax.dot_general(a_ref[...].astype(BF16), b_ref[...].astype(BF16), TN_DIMS,
                                      preferred_element_type=F32)

    return pl.pallas_call(
        body, grid=(n // tn, t // tk),
        in_specs=[pl.BlockSpec((tk, m), lambda j, s: (s, 0)), pl.BlockSpec((tk, tn), lambda j, s: (s, j))],
        out_specs=pl.BlockSpec((m, tn), lambda j, s: (0, j)),
        out_shape=jax.ShapeDtypeStruct((m, n), F32),
        compiler_params=_params("parallel", "arbitrary"), name=name)(a, b)


def _dwin_bwd(hn, dzs, rider=None):
    t, d = hn.shape
    nq = N_SPLIT * d // N_CHIPS
    tk = _pick(t, ROW_TILE_MM, 16)
    tn = _pick(nq, COL_TILE, 128)
    bps = nq // tn
    bpz = d // tn

    def body(hn_ref, *rest):
        dz_refs, o_ref = rest[:N_SPLIT], rest[N_SPLIT]
        j = pl.program_id(0)

        @pl.when(pl.program_id(1) == 0)
        def _():
            o_ref[...] = jnp.zeros_like(o_ref)
        for s in range(N_SPLIT):
            @pl.when(j // bpz == s)
            def _(s=s):
                o_ref[...] += lax.dot_general(hn_ref[...], dz_refs[s][...], TN_DIMS, preferred_element_type=F32)

    def dz_spec(s):
        def imap(j, r):
            on = (j // bpz) == s
            return (jnp.where(on, r, 0), jnp.where(on, j % bpz, 0))
        return pl.BlockSpec((tk, tn), imap)

    return _call(
        body, rider, (hn, *dzs), grid=(N_SPLIT * bpz, t // tk),
        in_specs=[pl.BlockSpec((tk, d), lambda j, r: (r, 0))] + [dz_spec(s) for s in range(N_SPLIT)],
        out_specs=[pl.BlockSpec((None, d, tn), lambda j, r: (j // bps, 0, j % bps))],
        out_shape=[jax.ShapeDtypeStruct((N_CHIPS, d, nq), F32)],
        scratch_shapes=[], semantics=("parallel", "arbitrary"), name="dwin_bwd")


def _din_bwd(dzs, w, h, g, dho, rider=None):
    t, d = h.shape
    nq = w.shape[2]
    tm = _pick(t, ROW_TILE_MM, 16)
    tk = _pick(nq, COL_TILE, 128)
    bps = nq // tk
    bpz = d // tk
    nk = N_SPLIT * bpz

    def body(*refs):
        dz_refs = refs[:N_SPLIT]
        w_ref, h_ref, g_ref, dho_ref, dh_ref, dg_ref, dg8 = refs[N_SPLIT:]
        i, k = pl.program_id(0), pl.program_id(1)

        @pl.when(k == 0)
        def _():
            dh_ref[...] = jnp.zeros_like(dh_ref)

        @pl.when((i == 0) & (k == 0))
        def _():
            dg8[...] = jnp.zeros_like(dg8)
        for s in range(N_SPLIT):
            @pl.when(k // bpz == s)
            def _(s=s):
                dh_ref[...] += lax.dot_general(dz_refs[s][...], w_ref[...], NT_DIMS, preferred_element_type=F32)

        @pl.when(k == nk - 1)
        def _():
            def norm_bwd(r0):
                rows = _rows(r0)
                v = h_ref[rows, :]
                rinv = lax.rsqrt(jnp.mean(v * v, axis=-1, keepdims=True) + EPS)
                xh = v * rinv
                dhn = dh_ref[rows, :]
                part = dhn * xh
                dg8[...] += part[0:8, :] + part[8:16, :]
                dxh = dhn * g_ref[...]
                dh_ref[rows, :] = dho_ref[rows, :] + rinv * (dxh - xh * jnp.mean(dxh * xh, axis=-1, keepdims=True))
            _row_blocks(tm, norm_bwd)
            dg_ref[...] = jnp.sum(dg8[...], axis=0, keepdims=True)

    def dz_spec(s):
        return pl.BlockSpec((tm, tk), lambda i, k: (i, jnp.clip(k - s * bpz, 0, bpz - 1)))

    rows = pl.BlockSpec((tm, d), lambda i, k: (i, 0))
    once = pl.BlockSpec((tm, d), lambda i, k: (i, 0), pipeline_mode=pl.Buffered(1))
    vec = _whole((1, d))
    return _call(
        body, rider, (*dzs, w, h, g, dho), grid=(t // tm, nk),
        in_specs=[dz_spec(s) for s in range(N_SPLIT)]
        + [pl.BlockSpec((None, d, tk), lambda i, k: (k // bps, 0, k % bps)), once, vec, once],
        out_specs=[rows, vec],
        out_shape=[jax.ShapeDtypeStruct((t, d), F32), jax.ShapeDtypeStruct((1, d), F32)],
        scratch_shapes=[pltpu.VMEM((8, d), F32)], semantics=("arbitrary", "arbitrary"), name="din_bwd")


def _branch_b_bwd(z, cb, dyb, wb, lg, lb):
    t = z.shape[0]
    d = wb.shape[1]
    tt = _pick(t, ROW_TILE_EW, HALO_B)
    hb = tt // HALO_B
    nt = t // tt
    rc = _conv_rows(tt)

    def body(vb_ref, gb_ref, vbh_ref, gbh_ref, sb_ref, cb_ref, dyb_ref, wb_ref, lg_ref, lb_ref,
             dvb_ref, dgb_ref, dsb_ref, dwb_ref, dbb_ref, dlg_ref, dlb_ref,
             gext, dext, gs, ds, cb_s, carry, dwb8, vec8):
        i = pl.program_id(0)

        @pl.when(i == 0)
        def _():
            for ref in (carry, dwb8, vec8):
                ref[...] = jnp.zeros_like(ref)
        _glu_ext(gext, vb_ref, gb_ref, vbh_ref, gbh_ref, i == nt - 1, tt)

        def norm_bwd(r0):
            rows = _rows(r0)
            xhat, rstd = _layer_norm_stats(cb_ref[rows, :])
            lg = lg_ref[...]
            ln = xhat * lg + lb_ref[...]
            sl = _sigmoid(ln)
            sb = sb_ref[rows, :]
            ss = _sigmoid(sb)
            dyb = dyb_ref[rows, :]
            dln = dyb * (sb * ss) * _dsilu(ln, sl)
            dsb_ref[rows, :] = (dyb * (ln * sl) * _dsilu(sb, ss)).astype(BF16)
            dxh = dln * lg
            dcb = rstd * (dxh - jnp.mean(dxh, axis=-1, keepdims=True)
                          - xhat * jnp.mean(dxh * xhat, axis=-1, keepdims=True))
            dext[rows, :] = dcb
            for n, val in enumerate((dln * xhat, dln, dcb)):
                vec8[n] += val[0:8, :] + val[8:16, :]
        _row_blocks(tt, norm_bwd)
        dext[tt:, :] = carry[...]
        carry[...] = dext[0:HALO_B, :]
        for c in range(d // LANE_CHUNK):
            cs = slice(c * LANE_CHUNK, (c + 1) * LANE_CHUNK)
            _shifted(gext, gs, cs, tt + HALO_B - 8)
            _shifted(dext, ds, cs, tt + HALO_B - 8)
            for r0 in range(0, tt, rc):
                dcb = dext[r0:r0 + rc, cs]
                acc = jnp.zeros((rc, LANE_CHUNK), F32)
                for k in range(CONV_B):
                    acc = acc + wb_ref[k:k + 1, cs] * _tap(dext, ds, cs, CONV_B - 1 - k, r0, rc)
                    prod = dcb * _tap(gext, gs, cs, HALO_B - (CONV_B - 1) + k, r0, rc)
                    part = prod[0:8, :]
                    for j in range(8, rc, 8):
                        part = part + prod[j:j + 8, :]
                    dwb8[k, :, cs] += part
                cb_s[r0:r0 + rc, cs] = acc

        def glu_bwd(r0):
            rows = _rows(r0)
            dglu = cb_s[rows, :]
            sg = _sigmoid(gb_ref[rows, :])
            dvb_ref[rows, :] = (dglu * sg).astype(BF16)
            dgb_ref[rows, :] = (dglu * vb_ref[rows, :] * (sg * (1.0 - sg))).astype(BF16)
        _row_blocks(tt, glu_bwd)

        @pl.when(i == nt - 1)
        def _():
            dwb_ref[...] = jnp.sum(dwb8[...], axis=1)
            dlg_ref[...] = jnp.sum(vec8[0], axis=0, keepdims=True)
            dlb_ref[...] = jnp.sum(vec8[1], axis=0, keepdims=True)
            dbb_ref[...] = jnp.sum(vec8[2], axis=0, keepdims=True)

    rev = lambda i: nt - 1 - i
    vec = _whole((1, d))
    tile = lambda s: pl.BlockSpec((tt, d), lambda i: (rev(i), s))
    halo = lambda s: pl.BlockSpec((HALO_B, d), lambda i: (jnp.maximum(rev(i) * hb - 1, 0), s))
    otile = pl.BlockSpec((tt, d), lambda i: (rev(i), 0))
    ext = pltpu.VMEM((tt + HALO_B, d), F32)
    shifts = pltpu.VMEM((7, tt + HALO_B - 8, LANE_CHUNK), F32)
    return pl.pallas_call(
        body, grid=(nt,),
        in_specs=[tile(2), tile(3), halo(2), halo(3), tile(4), otile, otile, _whole((CONV_B, d)), vec, vec],
        out_specs=[otile, otile, otile, _whole((CONV_B, d)), vec, vec, vec],
        out_shape=[jax.ShapeDtypeStruct((t, d), BF16)] * 3
        + [jax.ShapeDtypeStruct((CONV_B, d), F32)] + [jax.ShapeDtypeStruct((1, d), F32)] * 3,
        scratch_shapes=[ext, ext, shifts, shifts, pltpu.VMEM((tt, d), F32), pltpu.VMEM((HALO_B, d), F32),
                        pltpu.VMEM((CONV_B, 8, d), F32), pltpu.VMEM((3, 8, d), F32)],
        compiler_params=_params("arbitrary"), name="branch_b_bwd")(z, z, z, z, z, cb, dyb, wb, lg, lb)


def _branch_a_bwd(z, hl, dya, wa, ba, wr, br, wi, bi, lam):
    t = z.shape[0]
    d = wa.shape[1]
    tt = _pick(t, ROW_TILE_EW, HALO_B)
    hb = tt // HALO_A
    nt = t // tt
    nh = d // HEAD_DIM

    def body(xa_ref, xah_ref, sa_ref, hl_ref, hlh_ref, dya_ref, wa_ref, ba_ref, wr_ref, br_ref, wi_ref, bi_ref,
             lam_ref, dxa_ref, dsa_ref, dwa_ref, dba_ref, dbr_ref, dbi_ref, dlam_ref, dwr_ref, dwi_ref,
             xext, hext, dext, gr_s, gi_s, a_s, lam_s, gcarry, dcarry):
        i = pl.program_id(0)
        first = i == nt - 1

        @pl.when(i == 0)
        def _():
            for ref in (dwa_ref, dba_ref, dbr_ref, dbi_ref, dlam_ref, dwr_ref, dwi_ref, gcarry, dcarry):
                ref[...] = jnp.zeros_like(ref)
        xext[0:HALO_A, :] = jnp.where(first, 0.0, xah_ref[...])
        xext[HALO_A:, :] = xa_ref[...]
        hext[0:HALO_A, :] = jnp.where(first, 0.0, hlh_ref[...])
        hext[HALO_A:, :] = hl_ref[...]
        ca = _conv_a(xext, wa_ref, ba_ref[...], tt)
        r, ig = _gates_a(ca, wr_ref, wi_ref, br_ref[...], bi_ref[...], gr_s, gi_s)
        lam = lam_ref[...]
        sp = _softplus(-lam)
        a, mult = _decay_a(r, sp)
        sa = sa_ref[...]
        ss = _sigmoid(sa)
        dya = dya_ref[...]
        dsa_ref[...] = (dya * hl_ref[...] * _dsilu(sa, ss)).astype(BF16)
        a_s[...] = a
        lam_s[...] = dya * (sa * ss)
        gcarry[...] = _scan_rev(a_s, lam_s, lam_s, gcarry[...], tt // 8)
        du = lam_s[...]
        da = du * hext[pl.ds(HALO_A - 1, tt), :]
        dig = du * mult * ca
        dca = du * mult * ig
        dlog_a = da * a - (du * ig * ca) * jnp.where(mult > 0.0, a * a / mult, 0.0)
        dgr = (dlog_a * (-LRU_C * sp)) * r * (1.0 - r)
        dgi = dig * ig * (1.0 - ig)
        dlam_ref[...] += jnp.sum(dlog_a * r, axis=0, keepdims=True) * (LRU_C * _sigmoid(-lam))
        dbr_ref[...] += jnp.sum(dgr, axis=0, keepdims=True)
        dbi_ref[...] += jnp.sum(dgi, axis=0, keepdims=True)
        cab = ca.astype(BF16)
        dgrb = dgr.astype(BF16)
        dgib = dgi.astype(BF16)
        for hh in range(nh):
            sl = slice(hh * HEAD_DIM, (hh + 1) * HEAD_DIM)
            dwr_ref[hh] += lax.dot_general(cab[:, sl], dgrb[:, sl], TN_DIMS, preferred_element_type=F32)
            dwi_ref[hh] += lax.dot_general(cab[:, sl], dgib[:, sl], TN_DIMS, preferred_element_type=F32)
            gr_s[:, sl] = (lax.dot_general(dgrb[:, sl], wr_ref[hh], NT_DIMS, preferred_element_type=F32)
                           + lax.dot_general(dgib[:, sl], wi_ref[hh], NT_DIMS, preferred_element_type=F32))
        dca = dca + gr_s[...]
        dba_ref[...] += jnp.sum(dca, axis=0, keepdims=True)
        dext[0:tt, :] = dca
        dext[tt:, :] = dcarry[...]
        dcarry[...] = dca[0:HALO_A, :]
        dxa = jnp.zeros_like(dca)
        for k in range(CONV_A):
            off = HALO_A - (CONV_A - 1) + k
            dwa_ref[k:k + 1, :] += jnp.sum(dext[0:tt, :] * xext[pl.ds(off, tt), :], axis=0, keepdims=True)
            dxa = dxa + wa_ref[k:k + 1, :] * dext[pl.ds(CONV_A - 1 - k, tt), :]
        dxa_ref[...] = dxa.astype(BF16)

    rev = lambda i: nt - 1 - i
    vec = _whole((1, d))
    hw = _whole((nh, HEAD_DIM, HEAD_DIM))
    tile = lambda s: pl.BlockSpec((tt, d), lambda i: (rev(i), s))
    halo = pl.BlockSpec((HALO_A, d), lambda i: (jnp.maximum(rev(i) * hb - 1, 0), 0))
    big = pltpu.VMEM((tt + HALO_A, d), F32)
    full = pltpu.VMEM((tt, d), F32)
    return pl.pallas_call(
        body, grid=(nt,),
        in_specs=[tile(0), halo, tile(1), tile(0), halo, tile(0), _whole((CONV_A, d)), vec, hw, vec, hw, vec, vec],
        out_specs=[tile(0), tile(0), _whole((CONV_A, d)), vec, vec, vec, vec, hw, hw],
        out_shape=[jax.ShapeDtypeStruct((t, d), BF16)] * 2 + [jax.ShapeDtypeStruct((CONV_A, d), F32)]
        + [jax.ShapeDtypeStruct((1, d), F32)] * 4 + [jax.ShapeDtypeStruct((nh, HEAD_DIM, HEAD_DIM), F32)] * 2,
        scratch_shapes=[big, big, big, full, full, full, full, pltpu.VMEM((1, d), F32), pltpu.VMEM((HALO_A, d), F32)],
        compiler_params=_params("arbitrary"), name="branch_a_bwd")(z, z, z, hl, hl, dya, wa, ba, wr, br, wi, bi, lam)


def _rows_block(rows, cols, n_arrays=1):
    cap = max(8, BLOCK_BYTES // (4 * cols * n_arrays))
    return rows if rows <= cap else _pick(rows, cap, 8)


def _sum_halves(g, xo, c_idx):
    nq, r, c = g.shape
    rh = r // 2
    tr = _rows_block(rh, c)
    nb = rh // tr

    def body(c_ref, g_ref, x_ref, o_ref):
        o_ref[...] = (g_ref[...] + x_ref[...]).astype(BF16)

    grid_spec = pltpu.PrefetchScalarGridSpec(
        num_scalar_prefetch=1, grid=(nq, nb),
        in_specs=[pl.BlockSpec((None, tr, c), lambda q, i, cr: (q, cr[0] * nb + i, 0)),
                  pl.BlockSpec((None, tr, c), lambda q, i, cr: (q, i, 0))],
        out_specs=pl.BlockSpec((None, tr, c), lambda q, i, cr: (q, i, 0)))
    return pl.pallas_call(
        body, grid_spec=grid_spec, out_shape=jax.ShapeDtypeStruct((nq, rh, c), BF16),
        compiler_params=_params("parallel", "parallel"), name="sum_halves")(c_idx, g, xo)


def _sum_slots(y, name):
    ns, r, c = y.shape
    tr = _rows_block(r, c, ns)

    def body(y_ref, o_ref):
        acc = y_ref[0].astype(F32)
        for s in range(1, ns):
            acc = acc + y_ref[s].astype(F32)
        o_ref[...] = acc

    return pl.pallas_call(
        body, grid=(r // tr,), in_specs=[pl.BlockSpec((ns, tr, c), lambda i: (0, i, 0))],
        out_specs=pl.BlockSpec((tr, c), lambda i: (i, 0)), out_shape=jax.ShapeDtypeStruct((r, c), F32),
        compiler_params=_params("parallel"), name=name)(y)


def _sum_devices(own, gathered, me_idx, name):
    ns, r, c = gathered.shape
    tr = _rows_block(r, c, ns)

    def body(me_ref, own_ref, y_ref, o_ref):
        for me in range(ns):
            @pl.when(me_ref[0] == me)
            def _(me=me):
                acc = own_ref[...] if me == 0 else y_ref[0]
                for s in range(1, ns):
                    acc = acc + (own_ref[...] if s == me else y_ref[s])
                o_ref[...] = acc

    grid_spec = pltpu.PrefetchScalarGridSpec(
        num_scalar_prefetch=1, grid=(r // tr,),
        in_specs=[pl.BlockSpec((tr, c), lambda i, me: (i, 0)), pl.BlockSpec((ns, tr, c), lambda i, me: (0, i, 0))],
        out_specs=pl.BlockSpec((tr, c), lambda i, me: (i, 0)))
    return pl.pallas_call(
        body, grid_spec=grid_spec, out_shape=jax.ShapeDtypeStruct((r, c), F32),
        compiler_params=_params("parallel"), name=name)(me_idx, own, gathered)


def _adamw(w, g, m, v, name):
    shape = w.shape
    c = shape[-1]
    r = w.size // c
    tr = _rows_block(r, c)
    c1 = 1.0 / (1.0 - ADAM_B1 ** ADAM_STEP)
    c2 = 1.0 / (1.0 - ADAM_B2 ** ADAM_STEP)

    def body(w_ref, g_ref, m_ref, v_ref, go_ref, d_ref, mo_ref, vo_ref):
        gv = g_ref[...]
        mn = ADAM_B1 * m_ref[...] + (1.0 - ADAM_B1) * gv
        vn = ADAM_B2 * v_ref[...] + (1.0 - ADAM_B2) * (gv * gv)
        go_ref[...] = gv
        mo_ref[...] = mn
        vo_ref[...] = vn
        d_ref[...] = -ADAM_LR * ((mn * c1) / (jnp.sqrt(vn * c2) + ADAM_EPS) + ADAM_WD * w_ref[...])

    blk = pl.BlockSpec((tr, c), lambda i: (i, 0))
    outs = pl.pallas_call(
        body, grid=(r // tr,), in_specs=[blk] * 4, out_specs=[blk] * 4,
        out_shape=[jax.ShapeDtypeStruct((r, c), F32)] * 4,
        compiler_params=_params("parallel"), name=name)(*(a.reshape(r, c) for a in (w, g, m, v)))
    return tuple(o.reshape(shape) for o in outs)


def _adamw_layers(w, mine, theirs, m, v, c_idx, name):
    depth, r, c = w.shape
    tr = _rows_block(r // 2, c, 2)
    nbh = r // 2 // tr
    c1 = 1.0 / (1.0 - ADAM_B1 ** ADAM_STEP)
    c2 = 1.0 / (1.0 - ADAM_B2 ** ADAM_STEP)

    def body(c_ref, w_ref, m_ref, v_ref, *rest):
        g_refs = rest[:2 * depth]
        go_ref, d_ref, mo_ref, vo_ref = rest[2 * depth:]
        layer = pl.program_id(0)
        own = (pl.program_id(1) // nbh) == c_ref[0]

        def update(g_ref):
            gv = g_ref[...]
            mn = ADAM_B1 * m_ref[...] + (1.0 - ADAM_B1) * gv
            vn = ADAM_B2 * v_ref[...] + (1.0 - ADAM_B2) * (gv * gv)
            go_ref[...] = gv
            mo_ref[...] = mn
            vo_ref[...] = vn
            d_ref[...] = -ADAM_LR * ((mn * c1) / (jnp.sqrt(vn * c2) + ADAM_EPS) + ADAM_WD * w_ref[...])

        for s in range(depth):
            @pl.when((layer == s) & own)
            def _(s=s):
                update(g_refs[2 * s])

            @pl.when((layer == s) & jnp.logical_not(own))
            def _(s=s):
                update(g_refs[2 * s + 1])

    stacked = pl.BlockSpec((None, tr, c), lambda l, i, cr: (l, i, 0))

    def g_spec(s, is_mine):
        def imap(l, i, cr):
            own = (i // nbh) == cr[0]
            on = (l == s) & (own if is_mine else jnp.logical_not(own))
            return (jnp.where(on, i % nbh, 0), 0)
        return pl.BlockSpec((tr, c), imap)

    g_specs = [g_spec(s, is_mine) for s in range(depth) for is_mine in (True, False)]
    gs = [a for s in range(depth) for a in (mine[s], theirs[s])]
    grid_spec = pltpu.PrefetchScalarGridSpec(
        num_scalar_prefetch=1, grid=(depth, 2 * nbh), in_specs=[stacked] * 3 + g_specs, out_specs=[stacked] * 4)
    return pl.pallas_call(
        body, grid_spec=grid_spec, out_shape=[jax.ShapeDtypeStruct(w.shape, F32)] * 4,
        compiler_params=_params("arbitrary", "arbitrary"), name=name)(c_idx, w, m, v, *gs)


HBM_SPEC = pl.BlockSpec(memory_space=pltpu.HBM)


def _place():
    return lax.axis_index("x"), lax.axis_index("y"), lax.axis_index("c")


def _other_chips(x, y):
    return [(1 - x, y), (x, 1 - y), (1 - x, 1 - y)]


def _half(ref, which):
    rows = ref.shape[0] // 2
    return ref.at[pl.ds(which * rows, rows)]


def _gather_chips(arrays, name):
    partial = _run_rider(_gather_ici_rider(arrays), name + "_ici")
    return _run_rider(_gather_d2d_rider(partial), name + "_d2d")


class _Rider:
    def __init__(self, operands, out_shape, scratch, start, finish, aliases=None):
        self.operands, self.out_shape, self.scratch = list(operands), list(out_shape), list(scratch)
        self.start, self.finish, self.aliases = start, finish, dict(aliases or {})


def _run_rider(rider, name):
    n_in, n_out = len(rider.operands), len(rider.out_shape)

    def body(*refs):
        ins, outs, sems = refs[:n_in], refs[n_in:n_in + n_out], refs[n_in + n_out:]
        rider.start(ins, outs, sems)
        rider.finish(ins, outs, sems)

    return pl.pallas_call(
        body, in_specs=[HBM_SPEC] * n_in, out_specs=[HBM_SPEC] * n_out, out_shape=rider.out_shape,
        scratch_shapes=rider.scratch, input_output_aliases=rider.aliases, name=name)(*rider.operands)


def _hosted(body, rider, *, grid, in_specs, out_specs, out_shape, scratch_shapes, name):
    n_in, n_out, n_scr = len(in_specs), len(out_specs), len(scratch_shapes)
    r_in, r_out = len(rider.operands), len(rider.out_shape)

    def wrapped(*refs):
        ins, refs = refs[:n_in], refs[n_in:]
        r_ins, refs = refs[:r_in], refs[r_in:]
        outs, refs = refs[:n_out], refs[n_out:]
        r_outs, refs = refs[:r_out], refs[r_out:]
        scr, sems = refs[:n_scr], refs[n_scr:]
        first = functools.reduce(jnp.logical_and, [pl.program_id(a) == 0 for a in range(len(grid))])
        last = functools.reduce(jnp.logical_and, [pl.program_id(a) == g - 1 for a, g in enumerate(grid)])

        @pl.when(first)
        def _():
            rider.start(r_ins, r_outs, sems)
        body(*ins, *outs, *scr)

        @pl.when(last)
        def _():
            rider.finish(r_ins, r_outs, sems)

    res = pl.pallas_call(
        wrapped, grid=grid, in_specs=list(in_specs) + [HBM_SPEC] * r_in, out_specs=list(out_specs) + [HBM_SPEC] * r_out,
        out_shape=list(out_shape) + rider.out_shape, scratch_shapes=list(scratch_shapes) + rider.scratch,
        input_output_aliases={n_in + i: n_out + o for i, o in rider.aliases.items()},
        compiler_params=_params(*(["arbitrary"] * len(grid))), name=name)
    return lambda *operands: (lambda r: (r[:n_out], r[n_out:]))(res(*operands, *rider.operands))


def _gather_ici_rider(arrays):
    n = len(arrays)

    def copies(ins, outs, sems):
        send, recv = sems
        x, y, c = _place()
        me = 2 * x + y
        chips = _other_chips(x, y)
        sends, arrivals = [], []
        for a in range(n):
            for group in (sends, arrivals):
                group.append(pltpu.make_async_remote_copy(
                    src_ref=ins[a], dst_ref=outs[a].at[me], send_sem=send.at[a, 3], recv_sem=recv.at[a, 3],
                    device_id=(x, y, 1 - c), device_id_type=MESH))
            for j in range(3):
                q = 2 * chips[j][0] + chips[j][1]
                for dst, group in ((_half(outs[a].at[me], c), sends), (_half(outs[a].at[q], c), arrivals)):
                    group.append(pltpu.make_async_remote_copy(
                        src_ref=_half(ins[a], c), dst_ref=dst, send_sem=send.at[a, j], recv_sem=recv.at[a, j],
                        device_id=(*chips[j], c), device_id_type=MESH))
        return sends, arrivals

    def start(ins, outs, sems):
        for cp in copies(ins, outs, sems)[0]:
            cp.start()

    def finish(ins, outs, sems):
        sends, arrivals = copies(ins, outs, sems)
        for cp in arrivals:
            cp.wait_recv()
        for cp in sends:
            cp.wait_send()

    sems = pltpu.SemaphoreType.DMA((n, 4))
    return _Rider(arrays, [jax.ShapeDtypeStruct((N_CHIPS,) + a.shape, a.dtype) for a in arrays],
                  [sems, sems], start, finish)


def _gather_d2d_rider(partial):
    n = len(partial)

    def copies(outs, sems):
        send, recv = sems
        x, y, c = _place()
        chips = _other_chips(x, y)
        sends, arrivals = [], []
        for a in range(n):
            for j in range(3):
                q = 2 * chips[j][0] + chips[j][1]
                for which, group in ((c, sends), (1 - c, arrivals)):
                    ref = _half(outs[a].at[q], which)
                    group.append(pltpu.make_async_remote_copy(
                        src_ref=ref, dst_ref=ref, send_sem=send.at[a, j], recv_sem=recv.at[a, j],
                        device_id=(x, y, 1 - c), device_id_type=MESH))
        return sends, arrivals

    def start(ins, outs, sems):
        for cp in copies(outs, sems)[0]:
            cp.start()

    def finish(ins, outs, sems):
        sends, arrivals = copies(outs, sems)
        for cp in arrivals:
            cp.wait_recv()
        for cp in sends:
            cp.wait_send()

    sems = pltpu.SemaphoreType.DMA((n, 3))
    return _Rider(partial, [jax.ShapeDtypeStruct(a.shape, a.dtype) for a in partial], [sems, sems], start, finish,
                  aliases={k: k for k in range(n)})


def _swap_halves_rider(grads):
    n = len(grads)

    def copies(ins, outs, sems):
        send, recv = sems
        x, y, c = _place()
        made = []
        for a in range(n):
            rows = ins[a].shape[1] // 2
            made.append(pltpu.make_async_remote_copy(
                src_ref=ins[a].at[:, pl.ds((1 - c) * rows, rows)], dst_ref=outs[a],
                send_sem=send.at[a], recv_sem=recv.at[a], device_id=(x, y, 1 - c), device_id_type=MESH))
        return made

    def start(ins, outs, sems):
        for cp in copies(ins, outs, sems):
            cp.start()

    def finish(ins, outs, sems):
        for cp in copies(ins, outs, sems):
            cp.wait()

    sems = pltpu.SemaphoreType.DMA((n,))
    return _Rider(grads, [jax.ShapeDtypeStruct((g.shape[0], g.shape[1] // 2) + g.shape[2:], g.dtype) for g in grads],
                  [sems, sems], start, finish)


def _scatter_rider(parts):
    n = len(parts)

    def copies(ins, outs, sems):
        send, recv, lsem = sems
        x, y, c = _place()
        me = 2 * x + y
        chips = _other_chips(x, y)
        local = [pltpu.make_async_copy(ins[a].at[me], outs[a].at[me], lsem.at[a]) for a in range(n)]
        sends, arrivals = [], []
        for a in range(n):
            for j in range(3):
                q = 2 * chips[j][0] + chips[j][1]
                for dst, group in ((outs[a].at[me], sends), (outs[a].at[q], arrivals)):
                    group.append(pltpu.make_async_remote_copy(
                        src_ref=ins[a].at[q], dst_ref=dst, send_sem=send.at[a, j], recv_sem=recv.at[a, j],
                        device_id=(*chips[j], c), device_id_type=MESH))
        return local, sends, arrivals

    def start(ins, outs, sems):
        local, sends, _ = copies(ins, outs, sems)
        for cp in local + sends:
            cp.start()

    def finish(ins, outs, sems):
        local, sends, arrivals = copies(ins, outs, sems)
        for cp in arrivals:
            cp.wait_recv()
        for cp in sends:
            cp.wait_send()
        for cp in local:
            cp.wait()

    sems = pltpu.SemaphoreType.DMA((n, 3))
    return _Rider(parts, [jax.ShapeDtypeStruct(p.shape, p.dtype) for p in parts],
                  [sems, sems, pltpu.SemaphoreType.DMA((n,))], start, finish)


def _swap_sibling(arrays, name):
    n = len(arrays)

    def body(*refs):
        ins, outs = refs[:n], refs[n:2 * n]
        send, recv = refs[2 * n:]
        x, y, c = _place()
        copies = [pltpu.make_async_remote_copy(
            src_ref=ins[a], dst_ref=outs[a], send_sem=send.at[a], recv_sem=recv.at[a],
            device_id=(x, y, 1 - c), device_id_type=MESH) for a in range(n)]
        for cp in copies:
            cp.start()
        for cp in copies:
            cp.wait()

    sems = pltpu.SemaphoreType.DMA((n,))
    return pl.pallas_call(
        body, in_specs=[HBM_SPEC] * n, out_specs=[HBM_SPEC] * n,
        out_shape=[jax.ShapeDtypeStruct(a.shape, a.dtype) for a in arrays],
        scratch_shapes=[sems, sems], name=name)(*arrays)


def _gather_all(arrays, name):
    n = len(arrays)

    def body(*refs):
        ins, outs = refs[:n], refs[n:2 * n]
        send, recv = refs[2 * n:]
        x, y, c = _place()
        chips = _other_chips(x, y)
        sibling = (x, y, 1 - c)

        def slot(a, px, py, pc):
            return outs[a].at[4 * px + 2 * py + pc]

        def copy(a, k, block, to, src=None):
            return pltpu.make_async_remote_copy(
                src_ref=slot(a, *block) if src is None else src, dst_ref=slot(a, *block),
                send_sem=send.at[a, k], recv_sem=recv.at[a, k], device_id=to, device_id_type=MESH)

        sent = []
        for a in range(n):
            sent.append(copy(a, 0, (x, y, c), sibling, src=ins[a]))
            sent += [copy(a, 1 + j, (x, y, c), (*chip, c), src=ins[a]) for j, chip in enumerate(chips)]
        for cp in sent:
            cp.start()
        for a in range(n):
            for j, chip in enumerate(chips):
                copy(a, 1 + j, (*chip, c), (x, y, c)).wait_recv()
                cp = copy(a, 4 + j, (*chip, c), sibling)
                cp.start()
                sent.append(cp)
        for a in range(n):
            copy(a, 0, sibling, (x, y, c)).wait_recv()
            for j, chip in enumerate(chips):
                copy(a, 4 + j, (*chip, 1 - c), (x, y, c)).wait_recv()
        for cp in sent:
            cp.wait_send()

    sems = pltpu.SemaphoreType.DMA((n, 7))
    return pl.pallas_call(
        body, in_specs=[HBM_SPEC] * n, out_specs=[HBM_SPEC] * n,
        out_shape=[jax.ShapeDtypeStruct((N_DEV,) + a.shape, a.dtype) for a in arrays],
        scratch_shapes=[sems, sems], name=name)(*arrays)


def _layer_fwd(h, p, next_shards):
    ride = None if next_shards is None else _gather_ici_rider(next_shards)
    (z, hn), partial = _in_proj_fwd(h, p["norm_g"], p["w_in"], ride)
    ya, hl = _branch_a_fwd(z, p["conv_a_w"], p["conv_a_b"], p["w_rg"], p["b_rg"], p["w_ig"], p["b_ig"], p["lam"])
    ride = None if partial is None else _gather_d2d_rider(partial)
    (yb, cb), gathered = _branch_b_fwd(z, p["conv_b_w"], p["conv_b_b"], p["ln_b_g"], p["ln_b_b"], ride)
    merged, pa, pb = _merge_fwd(ya, yb, z, p["w_pa"], p["w_pb"])
    return _out_fwd(h, merged, p["w_out"]), (h, hn, z, hl, cb, ya, yb, pa, pb, merged), gathered


def _layer_bwd(dho, p, saved, chip_sums, c_idx):
    h, hn, z, hl, cb, ya, yb, pa, pb, merged = saved
    dpa, dpb, dma, dmb = _dmerged_bwd(dho, p["w_out"], z, pa, pb)
    g = {"w_out": _matmul_tn(merged, dho, "dw_out"), "w_proj_a": _matmul_tn(ya, dpa, "dw_proj_a"),
         "w_proj_b": _matmul_tn(yb, dpb, "dw_proj_b")}
    dya = _matmul_nt(dpa, p["w_pa"], "dya")
    dyb = _matmul_nt(dpb, p["w_pb"], "dyb")
    (dxa, dsa, g["conv_a_w"], g["conv_a_b"], g["b_rg"], g["b_ig"], g["lru_lambda"], g["w_rg"], g["w_ig"]) = \
        _branch_a_bwd(z, hl, dya, p["conv_a_w"], p["conv_a_b"], p["w_rg"], p["b_rg"], p["w_ig"], p["b_ig"], p["lam"])
    dvb, dgb, dsb, g["conv_b_w"], g["conv_b_b"], g["ln_b_g"], g["ln_b_b"] = \
        _branch_b_bwd(z, cb, dyb, p["conv_b_w"], p["ln_b_g"], p["ln_b_b"])
    dzs = (dxa, dsa, dvb, dgb, dsb, dma, dmb)
    (g["w_in"],), landed = _dwin_bwd(hn, dzs, None if chip_sums is None else _scatter_rider(chip_sums))
    d = h.shape[1]
    parts = [g["w_in"]] + [g[k].reshape(N_CHIPS, d // N_CHIPS, d) for k in BIG[1:]]
    (dh, g["norm_g"]), other = _din_bwd(dzs, p["w_in"], h, p["norm_g"], dho, _swap_halves_rider(parts))
    return dh, g, landed, [_sum_halves(a, b, c_idx) for a, b in zip(parts, other)]


BIG = ("w_in", "w_proj_a", "w_proj_b", "w_out")
VECS = ("norm_g", "conv_a_b", "b_rg", "b_ig", "lru_lambda", "conv_b_b", "ln_b_g", "ln_b_b")


def _reduce_finish(landed):
    mine = [_sum_slots(y, "sum_chips") for y in landed]
    return mine, _swap_sibling(mine, "swap_sibling")


def _forward_backward(h, target, layers, shards, final_g, c_idx):
    depth = len(layers)
    d = h.shape[1]
    gathered = _gather_chips(shards[0], "gather_weights")
    params, saved = [], []
    for l in range(depth):
        win, wpa, wpb, wo = gathered
        params.append({**layers[l], "w_in": win, "w_pa": wpa.reshape(d, d), "w_pb": wpb.reshape(d, d),
                       "w_out": wo.reshape(d, d)})
        h, s, gathered = _layer_fwd(h, params[l], shards[l + 1] if l + 1 < depth else None)
        saved.append(s)
    target = jnp.concatenate([jnp.zeros((N_META, d), F32), target], axis=0)
    loss_lanes, dh, g_final = _loss_head(h, final_g, target)
    grads, reduced, chip_sums = [None] * depth, [None] * depth, None
    for l in reversed(range(depth)):
        dh, grads[l], landed, chip_sums = _layer_bwd(dh, params[l], saved[l], chip_sums, c_idx)
        if landed is not None:
            reduced[l + 1] = _reduce_finish(landed)
    reduced[0] = _reduce_finish(_run_rider(_scatter_rider(chip_sums), "scatter_chips"))
    return loss_lanes, dh, g_final, grads, reduced


def kernel(x, meta, norm_g, w_in, conv_a_w, conv_a_b, w_rg, b_rg, w_ig, b_ig, lru_lambda, conv_b_w, conv_b_b, ln_b_g, ln_b_b, w_proj_a, w_proj_b, w_out, final_g, loss_target, m_meta, m_norm_g, m_w_in, m_conv_a_w, m_conv_a_b, m_w_rg, m_b_rg, m_w_ig, m_b_ig, m_lru_lambda, m_conv_b_w, m_conv_b_b, m_ln_b_g, m_ln_b_b, m_w_proj_a, m_w_proj_b, m_w_out, m_final_g, v_meta, v_norm_g, v_w_in, v_conv_a_w, v_conv_a_b, v_w_rg, v_b_rg, v_w_ig, v_b_ig, v_lru_lambda, v_conv_b_w, v_conv_b_b, v_ln_b_g, v_ln_b_b, v_w_proj_a, v_w_proj_b, v_w_out, v_final_g):
    names = ("meta", "norm_g", "w_in", "conv_a_w", "conv_a_b", "w_rg", "b_rg", "w_ig", "b_ig", "lru_lambda",
             "conv_b_w", "conv_b_b", "ln_b_g", "ln_b_b", "w_proj_a", "w_proj_b", "w_out", "final_g")
    w = dict(zip(names, (meta, norm_g, w_in, conv_a_w, conv_a_b, w_rg, b_rg, w_ig, b_ig, lru_lambda, conv_b_w,
                         conv_b_b, ln_b_g, ln_b_b, w_proj_a, w_proj_b, w_out, final_g)))
    m = dict(zip(names, (m_meta, m_norm_g, m_w_in, m_conv_a_w, m_conv_a_b, m_w_rg, m_b_rg, m_w_ig, m_b_ig,
                         m_lru_lambda, m_conv_b_w, m_conv_b_b, m_ln_b_g, m_ln_b_b, m_w_proj_a, m_w_proj_b, m_w_out,
                         m_final_g)))
    v = dict(zip(names, (v_meta, v_norm_g, v_w_in, v_conv_a_w, v_conv_a_b, v_w_rg, v_b_rg, v_w_ig, v_b_ig,
                         v_lru_lambda, v_conv_b_w, v_conv_b_b, v_ln_b_g, v_ln_b_b, v_w_proj_a, v_w_proj_b, v_w_out,
                         v_final_g)))
    depth, d = norm_g.shape
    dq = d // N_CHIPS
    px, py, pc = _place()
    chip = 2 * px + py
    c_idx = pc.astype(jnp.int32).reshape(1)

    meta_s, conv_a_s, conv_b_s = _gather_chips([meta, conv_a_w, conv_b_w], "gather_small")
    unshard = lambda s: jnp.concatenate([s[q] for q in range(N_CHIPS)], axis=-1)
    meta_f, conv_a_f, conv_b_f = unshard(meta_s), unshard(conv_a_s), unshard(conv_b_s)
    layers, shards = [], []
    for l in range(depth):
        shards.append([w_in[l].astype(BF16), w_proj_a[l].astype(BF16), w_proj_b[l].astype(BF16),
                       w_out[l].astype(BF16)])
        layers.append({
            "norm_g": norm_g[l][None], "conv_a_w": conv_a_f[l], "conv_a_b": conv_a_b[l][None],
            "w_rg": w_rg[l].astype(BF16), "b_rg": b_rg[l][None], "w_ig": w_ig[l].astype(BF16), "b_ig": b_ig[l][None],
            "lam": lru_lambda[l][None], "conv_b_w": conv_b_f[l], "conv_b_b": conv_b_b[l][None],
            "ln_b_g": ln_b_g[l][None], "ln_b_b": ln_b_b[l][None]})

    loss_lanes, dh, g_final, small, reduced = _forward_backward(
        jnp.concatenate([meta_f, x[0]], axis=0), loss_target[0], layers, shards, final_g[None], c_idx)
    loss = lax.psum(jnp.sum(loss_lanes), ("x", "y", "c"))
    grads = {}

    vec_rows = [small[l][k] for k in VECS for l in range(depth)]
    vec_rows += [small[l]["conv_a_w"] for l in range(depth)] + [small[l]["conv_b_w"] for l in range(depth)]
    vec_rows += [dh[:N_META], g_final]
    packed_v = jnp.concatenate(vec_rows, axis=0)
    pad = (-packed_v.shape[0]) % 8
    packed_v = jnp.concatenate([packed_v, jnp.zeros((pad, d), F32)], axis=0)
    packed_g = jnp.concatenate([small[l][k].reshape(-1, HEAD_DIM) for k in ("w_rg", "w_ig") for l in range(depth)],
                               axis=0)
    all_v, all_g = _gather_all([packed_v, packed_g], "gather_small_grads")
    me_idx = (4 * px + 2 * py + pc).astype(jnp.int32).reshape(1)
    sum_v = _sum_devices(packed_v, all_v, me_idx, "sum_devices_v")
    sum_g = _sum_devices(packed_g, all_g, me_idx, "sum_devices_g")
    cols = lambda a: lax.dynamic_slice_in_dim(a, chip * dq, dq, axis=-1)
    for n_k, k in enumerate(VECS):
        grads[k] = sum_v[n_k * depth:(n_k + 1) * depth]
    o = len(VECS) * depth
    grads["conv_a_w"] = cols(sum_v[o:o + depth * CONV_A].reshape(depth, CONV_A, d))
    o += depth * CONV_A
    grads["conv_b_w"] = cols(sum_v[o:o + depth * CONV_B].reshape(depth, CONV_B, d))
    o += depth * CONV_B
    grads["meta"] = cols(sum_v[o:o + N_META])
    grads["final_g"] = sum_v[o + N_META]
    gate_rows = depth * (d // HEAD_DIM) * HEAD_DIM
    grads["w_rg"] = sum_g[:gate_rows].reshape(w_rg.shape)
    grads["w_ig"] = sum_g[gate_rows:].reshape(w_ig.shape)

    outs = {k: _adamw(w[k], grads[k], m[k], v[k], f"adamw_{k}") for k in names if k not in BIG}
    for n_k, k in enumerate(BIG):
        outs[k] = _adamw_layers(w[k], [reduced[l][0][n_k] for l in range(depth)],
                                [reduced[l][1][n_k] for l in range(depth)], m[k], v[k], c_idx, f"adamw_{k}")
    dx = dh[N_META:][None]
    return (loss, dx, *[outs[k][0] for k in names], *[outs[k][1] for k in names],
            *[outs[k][2] for k in names], *[outs[k][3] for k in names])
```

```python
import functools

import jax
import jax.numpy as jnp
from jax import lax
from jax.experimental import pallas as pl
from jax.experimental.pallas import tpu as pltpu

F32 = jnp.float32
BF16 = jnp.bfloat16
MESH = pl.DeviceIdType.MESH

EPS = 1e-6
N_META = 16
HEAD_DIM = 128
CONV_A = 4
CONV_B = 31
LRU_C = 8.0
N_SPLIT = 7
N_CHIPS = 4

ADAM_LR = 0.001
ADAM_B1 = 0.9
ADAM_B2 = 0.999
ADAM_EPS = 1e-08
ADAM_WD = 0.01
ADAM_STEP = 10

HALO_A = 16
HALO_B = 48
ROW_TILE_MM = 912
ROW_TILE_EW = 144
ROW_BLOCK = 16
LANE_CHUNK = 256
COL_TILE = 512
COL_TILE_WIDE = 1024
COL_TILE_IN = 1792
VMEM_LIMIT = 56 * 1024 * 1024
BLOCK_BYTES = 2 * 1024 * 1024

NT_DIMS = (((1,), (1,)), ((), ()))
TN_DIMS = (((0,), (0,)), ((), ()))


def _pick(n, cap, mult):
    best = None
    for d in range(mult, min(n, cap) + 1, mult):
        if n % d == 0:
            best = d
    if best is None:
        raise ValueError(f"no tile for {n} (cap {cap}, multiple of {mult})")
    return best


def _params(*semantics):
    return pltpu.CompilerParams(dimension_semantics=semantics, vmem_limit_bytes=VMEM_LIMIT)


def _whole(shape):
    return pl.BlockSpec(shape, lambda *_: (0,) * len(shape))


def _sigmoid(v):
    return 0.5 * jnp.tanh(0.5 * v) + 0.5


def _dsilu(v, s):
    return s * (1.0 + v * (1.0 - s))


def _log1p(e):
    w = 1.0 + e
    return jnp.where(w == 1.0, e, e * jnp.log(w) / (w - 1.0))


def _softplus(v):
    return jnp.maximum(v, 0.0) + _log1p(jnp.exp(-jnp.abs(v)))


def _neg_expm1_twice(v):
    t = jnp.tanh(v)
    return -2.0 * t / (1.0 - t)


def _row_iota(shape):
    return lax.broadcasted_iota(jnp.int32, shape, 0)


def _call(body, rider, operands, *, grid, in_specs, out_specs, out_shape, scratch_shapes, semantics, name):
    if rider is None:
        return pl.pallas_call(
            body, grid=grid, in_specs=in_specs, out_specs=out_specs, out_shape=out_shape,
            scratch_shapes=scratch_shapes, compiler_params=_params(*semantics), name=name)(*operands), None
    return _hosted(body, rider, grid=grid, in_specs=in_specs, out_specs=out_specs, out_shape=out_shape,
                   scratch_shapes=scratch_shapes, name=name)(*operands)


def _in_proj_fwd(h, g, w, rider=None):
    t, d = h.shape
    nq = w.shape[2]
    tm = _pick(t, ROW_TILE_MM, 16)
    tn = _pick(nq, COL_TILE_IN, 128)
    bps = nq // tn

    def body(h_ref, g_ref, w_ref, z_ref, hn_ref):
        @pl.when(pl.program_id(1) == 0)
        def _():
            def norm(r0):
                rows = _rows(r0)
                v = h_ref[rows, :]
                rinv = lax.rsqrt(jnp.mean(v * v, axis=-1, keepdims=True) + EPS)
                hn_ref[rows, :] = (v * rinv * g_ref[...]).astype(BF16)
            _row_blocks(tm, norm)
        z_ref[...] = jnp.dot(hn_ref[...], w_ref[...], preferred_element_type=F32)

    return _call(
        body, rider, (h, g, w), grid=(t // tm, N_CHIPS * bps),
        in_specs=[pl.BlockSpec((tm, d), lambda i, j: (i, 0), pipeline_mode=pl.Buffered(1)), _whole((1, d)),
                  pl.BlockSpec((None, d, tn), lambda i, j: (j // bps, 0, j % bps))],
        out_specs=[pl.BlockSpec((tm, tn), lambda i, j: (i, j)), pl.BlockSpec((tm, d), lambda i, j: (i, 0))],
        out_shape=[jax.ShapeDtypeStruct((t, N_CHIPS * nq), F32), jax.ShapeDtypeStruct((t, d), BF16)],
        scratch_shapes=[], semantics=("parallel", "arbitrary"), name="in_proj_fwd")


def _conv_a(xext, wa_ref, ba, tt):
    acc = ba
    for k in range(CONV_A):
        acc = acc + wa_ref[k:k + 1, :] * xext[pl.ds(HALO_A - (CONV_A - 1) + k, tt), :]
    return acc


def _gates_a(ca, wr_ref, wi_ref, br, bi, gr_s, gi_s):
    cab = ca.astype(BF16)
    for hh in range(ca.shape[1] // HEAD_DIM):
        sl = slice(hh * HEAD_DIM, (hh + 1) * HEAD_DIM)
        gr_s[:, sl] = jnp.dot(cab[:, sl], wr_ref[hh], preferred_element_type=F32)
        gi_s[:, sl] = jnp.dot(cab[:, sl], wi_ref[hh], preferred_element_type=F32)
    return _sigmoid(gr_s[...] + br), _sigmoid(gi_s[...] + bi)


def _decay_a(r, sp):
    log_a = -LRU_C * r * sp
    a = jnp.exp(log_a)
    mult = jnp.sqrt(jnp.maximum(_neg_expm1_twice(log_a), 0.0))
    return a, mult


def _scan_fwd(a_ref, u_ref, out_ref, h0, nblk):
    def blk(b, hprev):
        rows = pl.ds(pl.multiple_of(b * 8, 8), 8)
        ca, cb = a_ref[rows, :], u_ref[rows, :]
        row = _row_iota(ca.shape)
        for dist in (1, 2, 4):
            m = row >= dist
            cb = jnp.where(m, ca * pltpu.roll(cb, dist, 0) + cb, cb)
            ca = jnp.where(m, ca * pltpu.roll(ca, dist, 0), ca)
        hb = ca * hprev + cb
        out_ref[rows, :] = hb
        return hb[7:8, :]
    return lax.fori_loop(0, nblk, blk, h0)


def _scan_rev(a_ref, d_ref, out_ref, g0, nblk):
    def blk(k, g):
        rows = pl.ds(pl.multiple_of((nblk - 1 - k) * 8, 8), 8)
        a, cb = a_ref[rows, :], d_ref[rows, :]
        row = _row_iota(a.shape)
        ca = jnp.where(row == 7, 1.0, pltpu.roll(a, 7, 0))
        for dist in (1, 2, 4):
            m = row < 8 - dist
            cb = jnp.where(m, cb + ca * pltpu.roll(cb, 8 - dist, 0), cb)
            ca = jnp.where(m, ca * pltpu.roll(ca, 8 - dist, 0), ca)
        lam = cb + ca * g
        out_ref[rows, :] = lam
        return a[0:1, :] * lam[0:1, :]
    return lax.fori_loop(0, nblk, blk, g0)


def _branch_a_fwd(z, wa, ba, wr, br, wi, bi, lam):
    t = z.shape[0]
    d = wa.shape[1]
    tt = _pick(t, ROW_TILE_EW, HALO_B)
    hb = tt // HALO_A
    nh = d // HEAD_DIM

    def body(xa_ref, halo_ref, sa_ref, wa_ref, ba_ref, wr_ref, br_ref, wi_ref, bi_ref, lam_ref,
             ya_ref, hl_ref, xext, gr_s, gi_s, a_s, carry):
        i = pl.program_id(0)

        @pl.when(i == 0)
        def _():
            carry[...] = jnp.zeros_like(carry)
        xext[0:HALO_A, :] = jnp.where(i == 0, 0.0, halo_ref[...])
        xext[HALO_A:, :] = xa_ref[...]
        ca = _conv_a(xext, wa_ref, ba_ref[...], tt)
        r, ig = _gates_a(ca, wr_ref, wi_ref, br_ref[...], bi_ref[...], gr_s, gi_s)
        a, mult = _decay_a(r, _softplus(-lam_ref[...]))
        a_s[...] = a
        hl_ref[...] = mult * (ig * ca)
        carry[...] = _scan_fwd(a_s, hl_ref, hl_ref, carry[...], tt // 8)
        sa = sa_ref[...]
        ya_ref[...] = (hl_ref[...] * (sa * _sigmoid(sa))).astype(BF16)

    vec = _whole((1, d))
    return pl.pallas_call(
        body, grid=(t // tt,),
        in_specs=[pl.BlockSpec((tt, d), lambda i: (i, 0)),
                  pl.BlockSpec((HALO_A, d), lambda i: (jnp.maximum(i * hb - 1, 0), 0)),
                  pl.BlockSpec((tt, d), lambda i: (i, 1)),
                  _whole((CONV_A, d)), vec, _whole((nh, HEAD_DIM, HEAD_DIM)), vec,
                  _whole((nh, HEAD_DIM, HEAD_DIM)), vec, vec],
        out_specs=[pl.BlockSpec((tt, d), lambda i: (i, 0)), pl.BlockSpec((tt, d), lambda i: (i, 0))],
        out_shape=[jax.ShapeDtypeStruct((t, d), BF16), jax.ShapeDtypeStruct((t, d), F32)],
        scratch_shapes=[pltpu.VMEM((tt + HALO_A, d), F32), pltpu.VMEM((tt, d), F32), pltpu.VMEM((tt, d), F32),
                        pltpu.VMEM((tt, d), F32), pltpu.VMEM((1, d), F32)],
        compiler_params=_params("arbitrary"), name="branch_a_fwd")(z, z, z, wa, ba, wr, br, wi, bi, lam)


def _row_blocks(n_rows, body):
    n = n_rows // ROW_BLOCK
    unroll = 3 if n % 3 == 0 else 1

    def step(b, carry):
        for u in range(unroll):
            body(pl.multiple_of((b * unroll + u) * ROW_BLOCK, ROW_BLOCK))
        return carry
    lax.fori_loop(0, n // unroll, step, 0)


def _rows(r0, offset=0):
    return pl.ds(pl.multiple_of(r0 + offset, ROW_BLOCK), ROW_BLOCK)


def _glu_ext(gext, vb_ref, gb_ref, vbh_ref, gbh_ref, first, tt):
    def halo(r0):
        rows = _rows(r0)
        gext[rows, :] = jnp.where(first, 0.0, vbh_ref[rows, :] * _sigmoid(gbh_ref[rows, :]))
    _row_blocks(HALO_B, halo)

    def tile(r0):
        rows = _rows(r0)
        gext[_rows(r0, HALO_B), :] = vb_ref[rows, :] * _sigmoid(gb_ref[rows, :])
    _row_blocks(tt, tile)


def _shifted(src, dst, cs, n_rows):
    for b in range(1, 8):
        dst[b - 1, 0:n_rows, :] = src[pl.ds(b, n_rows), cs]


def _tap(src, shifted, cs, off, r0, n):
    b = off % 8
    if b == 0:
        return src[pl.ds(off + r0, n), cs]
    return shifted[b - 1, pl.ds(off - b + r0, n), :]


def _conv_rows(tt):
    return tt // 2 if tt % 16 == 0 else tt


def _conv_b_fwd(gext, gs, wb_ref, bb_ref, cb_s, tt, d):
    rc = _conv_rows(tt)
    for c in range(d // LANE_CHUNK):
        cs = slice(c * LANE_CHUNK, (c + 1) * LANE_CHUNK)
        _shifted(gext, gs, cs, tt + HALO_B - 8)
        for r0 in range(0, tt, rc):
            acc = jnp.broadcast_to(bb_ref[:, cs], (rc, LANE_CHUNK))
            for k in range(CONV_B):
                acc = acc + wb_ref[k:k + 1, cs] * _tap(gext, gs, cs, HALO_B - (CONV_B - 1) + k, r0, rc)
            cb_s[r0:r0 + rc, cs] = acc


def _layer_norm_stats(cb):
    mu = jnp.mean(cb, axis=-1, keepdims=True)
    xc = cb - mu
    rstd = lax.rsqrt(jnp.mean(xc * xc, axis=-1, keepdims=True) + EPS)
    return xc * rstd, rstd


def _branch_b_fwd(z, wb, bb, lg, lb, rider=None):
    t = z.shape[0]
    d = wb.shape[1]
    tt = _pick(t, ROW_TILE_EW, HALO_B)
    hb = tt // HALO_B

    def body(vb_ref, gb_ref, vbh_ref, gbh_ref, sb_ref, wb_ref, bb_ref, lg_ref, lb_ref, yb_ref, cb_s, gext, gs):
        _glu_ext(gext, vb_ref, gb_ref, vbh_ref, gbh_ref, pl.program_id(0) == 0, tt)
        _conv_b_fwd(gext, gs, wb_ref, bb_ref, cb_s, tt, d)

        def finish(r0):
            rows = _rows(r0)
            xhat, _ = _layer_norm_stats(cb_s[rows, :])
            ln = xhat * lg_ref[...] + lb_ref[...]
            sb = sb_ref[rows, :]
            yb_ref[rows, :] = (ln * _sigmoid(ln) * (sb * _sigmoid(sb))).astype(BF16)
        _row_blocks(tt, finish)

    vec = _whole((1, d))
    tile = lambda s: pl.BlockSpec((tt, d), lambda i: (i, s))
    halo = lambda s: pl.BlockSpec((HALO_B, d), lambda i: (jnp.maximum(i * hb - 1, 0), s))
    return _call(
        body, rider, (z, z, z, z, z, wb, bb, lg, lb), grid=(t // tt,),
        in_specs=[tile(2), tile(3), halo(2), halo(3), tile(4), _whole((CONV_B, d)), vec, vec, vec],
        out_specs=[pl.BlockSpec((tt, d), lambda i: (i, 0))] * 2,
        out_shape=[jax.ShapeDtypeStruct((t, d), BF16), jax.ShapeDtypeStruct((t, d), F32)],
        scratch_shapes=[pltpu.VMEM((tt + HALO_B, d), F32), pltpu.VMEM((7, tt + HALO_B - 8, LANE_CHUNK), F32)],
        semantics=("parallel",), name="branch_b_fwd")


def _merge_fwd(ya, yb, z, wpa, wpb):
    t, d = ya.shape
    tm = _pick(t, ROW_TILE_MM, 16)
    tn = _pick(d, COL_TILE, 128)
    nb = d // tn

    def body(ya_ref, yb_ref, ma_ref, mb_ref, wpa_ref, wpb_ref, mg_ref, pa_ref, pb_ref):
        pa = jnp.dot(ya_ref[...], wpa_ref[...], preferred_element_type=F32)
        pb = jnp.dot(yb_ref[...], wpb_ref[...], preferred_element_type=F32)
        mg_ref[...] = (_sigmoid(ma_ref[...]) * pa + _sigmoid(mb_ref[...]) * pb).astype(BF16)
        pa_ref[...] = pa.astype(BF16)
        pb_ref[...] = pb.astype(BF16)

    rows = pl.BlockSpec((tm, d), lambda i, j: (i, 0))
    wcol = pl.BlockSpec((d, tn), lambda i, j: (0, j))
    outb = pl.BlockSpec((tm, tn), lambda i, j: (i, j))
    return pl.pallas_call(
        body, grid=(t // tm, nb),
        in_specs=[rows, rows, pl.BlockSpec((tm, tn), lambda i, j: (i, 5 * nb + j)),
                  pl.BlockSpec((tm, tn), lambda i, j: (i, 6 * nb + j)), wcol, wcol],
        out_specs=[outb, outb, outb],
        out_shape=[jax.ShapeDtypeStruct((t, d), BF16)] * 3,
        compiler_params=_params("parallel", "arbitrary"), name="merge_fwd")(ya, yb, z, z, wpa, wpb)


def _out_fwd(h, merged, wout):
    t, d = h.shape
    tm = _pick(t, ROW_TILE_MM, 16)
    tn = _pick(d, COL_TILE_WIDE, 128)

    def body(h_ref, mg_ref, w_ref, o_ref):
        o_ref[...] = h_ref[...] + jnp.dot(mg_ref[...], w_ref[...], preferred_element_type=F32)

    return pl.pallas_call(
        body, grid=(t // tm, d // tn),
        in_specs=[pl.BlockSpec((tm, tn), lambda i, j: (i, j)), pl.BlockSpec((tm, d), lambda i, j: (i, 0)),
                  pl.BlockSpec((d, tn), lambda i, j: (0, j))],
        out_specs=pl.BlockSpec((tm, tn), lambda i, j: (i, j)),
        out_shape=jax.ShapeDtypeStruct((t, d), F32),
        compiler_params=_params("parallel", "arbitrary"), name="out_fwd")(h, merged, wout)


def _loss_head(h, g, target):
    t, d = h.shape
    tt = _pick(t, ROW_TILE_EW, 16)

    def body(h_ref, g_ref, tg_ref, loss_ref, dh_ref, dg_ref):
        i = pl.program_id(0)

        @pl.when(i == 0)
        def _():
            loss_ref[...] = jnp.zeros_like(loss_ref)
            dg_ref[...] = jnp.zeros_like(dg_ref)
        v = h_ref[...]
        g = g_ref[...]
        rinv = lax.rsqrt(jnp.mean(v * v, axis=-1, keepdims=True) + EPS)
        xh = v * rinv
        valid = (i * tt + _row_iota(v.shape)) >= N_META
        diff = jnp.where(valid, xh * g - tg_ref[...], 0.0)
        loss_ref[...] += (0.5 / d) * jnp.sum(diff * diff, axis=0, keepdims=True)
        dy = diff * (1.0 / d)
        dg_ref[...] += jnp.sum(dy * xh, axis=0, keepdims=True)
        dxh = dy * g
        dh_ref[...] = rinv * (dxh - xh * jnp.mean(dxh * xh, axis=-1, keepdims=True))

    tile = pl.BlockSpec((tt, d), lambda i: (i, 0))
    vec = _whole((1, d))
    return pl.pallas_call(
        body, grid=(t // tt,), in_specs=[tile, vec, tile], out_specs=[vec, tile, vec],
        out_shape=[jax.ShapeDtypeStruct((1, d), F32), jax.ShapeDtypeStruct((t, d), F32),
                   jax.ShapeDtypeStruct((1, d), F32)],
        compiler_params=_params("arbitrary"), name="loss_head")(h, g, target)


def _dmerged_bwd(dho, wout, z, pa, pb):
    t, d = dho.shape
    tm = _pick(t, ROW_TILE_MM, 16)
    tn = _pick(d, COL_TILE, 128)
    nb = d // tn

    def body(dho_ref, w_ref, ma_ref, mb_ref, pa_ref, pb_ref, dpa_ref, dpb_ref, dma_ref, dmb_ref, dho_s, dm_s):
        @pl.when(pl.program_id(1) == 0)
        def _():
            dho_s[...] = dho_ref[...].astype(BF16)
        dm_s[...] = lax.dot_general(dho_s[...], w_ref[...], NT_DIMS, preferred_element_type=F32)

        def gates(r0):
            rows = _rows(r0)
            dm = dm_s[rows, :]
            sa = _sigmoid(ma_ref[rows, :])
            sb = _sigmoid(mb_ref[rows, :])
            dpa_ref[rows, :] = (dm * sa).astype(BF16)
            dpb_ref[rows, :] = (dm * sb).astype(BF16)
            dma_ref[rows, :] = (dm * pa_ref[rows, :].astype(F32) * (sa * (1.0 - sa))).astype(BF16)
            dmb_ref[rows, :] = (dm * pb_ref[rows, :].astype(F32) * (sb * (1.0 - sb))).astype(BF16)
        _row_blocks(tm, gates)

    blk = pl.BlockSpec((tm, tn), lambda i, j: (i, j))
    return pl.pallas_call(
        body, grid=(t // tm, nb),
        in_specs=[pl.BlockSpec((tm, d), lambda i, j: (i, 0)), pl.BlockSpec((tn, d), lambda i, j: (j, 0)),
                  pl.BlockSpec((tm, tn), lambda i, j: (i, 5 * nb + j)),
                  pl.BlockSpec((tm, tn), lambda i, j: (i, 6 * nb + j)), blk, blk],
        out_specs=[blk, blk, blk, blk],
        out_shape=[jax.ShapeDtypeStruct((t, d), BF16)] * 4,
        scratch_shapes=[pltpu.VMEM((tm, d), BF16), pltpu.VMEM((tm, tn), F32)],
        compiler_params=_params("parallel", "arbitrary"), name="dmerged_bwd")(dho, wout, z, z, pa, pb)


def _matmul_nt(a, w, name):
    t, k = a.shape
    n = w.shape[0]
    tm = _pick(t, ROW_TILE_MM, 16)
    tn = _pick(n, COL_TILE_WIDE, 128)

    def body(a_ref, w_ref, o_ref):
        o_ref[...] = lax.dot_general(a_ref[...], w_ref[...], NT_DIMS, preferred_element_type=F32)

    return pl.pallas_call(
        body, grid=(t // tm, n // tn),
        in_specs=[pl.BlockSpec((tm, k), lambda i, j: (i, 0)), pl.BlockSpec((tn, k), lambda i, j: (j, 0))],
        out_specs=pl.BlockSpec((tm, tn), lambda i, j: (i, j)),
        out_shape=jax.ShapeDtypeStruct((t, n), F32),
        compiler_params=_params("parallel", "arbitrary"), name=name)(a, w)


def _matmul_tn(a, b, name):
    t, m = a.shape
    n = b.shape[1]
    tk = _pick(t, ROW_TILE_MM, 16)
    tn = _pick(n, COL_TILE_WIDE, 128)

    def body(a_ref, b_ref, o_ref):
        @pl.when(pl.program_id(1) == 0)
        def _():
            o_ref[...] = jnp.zeros_like(o_ref)
        o_ref[...] += lax.dot_general(a_ref[...].astype(BF16), b_ref[...].astype(BF16), TN_DIMS,
                                      preferred_element_type=F32)

    return pl.pallas_call(
        body, grid=(n // tn, t // tk),
        in_specs=[pl.BlockSpec((tk, m), lambda j, s: (s, 0)), pl.BlockSpec((tk, tn), lambda j, s: (s, j))],
        out_specs=pl.BlockSpec((m, tn), lambda j, s: (0, j)),
        out_shape=jax.ShapeDtypeStruct((m, n), F32),
        compiler_params=_params("parallel", "arbitrary"), name=name)(a, b)


def _dwin_bwd(hn, dzs, rider=None):
    t, d = hn.shape
    nq = N_SPLIT * d // N_CHIPS
    tk = _pick(t, ROW_TILE_MM, 16)
    tn = _pick(nq, COL_TILE, 128)
    bps = nq // tn
    bpz = d // tn

    def body(hn_ref, *rest):
        dz_refs, o_ref = rest[:N_SPLIT], rest[N_SPLIT]
        j = pl.program_id(0)

        @pl.when(pl.program_id(1) == 0)
        def _():
            o_ref[...] = jnp.zeros_like(o_ref)
        for s in range(N_SPLIT):
            @pl.when(j // bpz == s)
            def _(s=s):
                o_ref[...] += lax.dot_general(hn_ref[...], dz_refs[s][...], TN_DIMS, preferred_element_type=F32)

    def dz_spec(s):
        def imap(j, r):
            on = (j // bpz) == s
            return (jnp.where(on, r, 0), jnp.where(on, j % bpz, 0))
        return pl.BlockSpec((tk, tn), imap)

    return _call(
        body, rider, (hn, *dzs), grid=(N_SPLIT * bpz, t // tk),
        in_specs=[pl.BlockSpec((tk, d), lambda j, r: (r, 0))] + [dz_spec(s) for s in range(N_SPLIT)],
        out_specs=[pl.BlockSpec((None, d, tn), lambda j, r: (j // bps, 0, j % bps))],
        out_shape=[jax.ShapeDtypeStruct((N_CHIPS, d, nq), F32)],
        scratch_shapes=[], semantics=("parallel", "arbitrary"), name="dwin_bwd")


def _din_bwd(dzs, w, h, g, dho, rider=None):
    t, d = h.shape
    nq = w.shape[2]
    tm = _pick(t, ROW_TILE_MM, 16)
    tk = _pick(nq, COL_TILE, 128)
    bps = nq // tk
    bpz = d // tk
    nk = N_SPLIT * bpz

    def body(*refs):
        dz_refs = refs[:N_SPLIT]
        w_ref, h_ref, g_ref, dho_ref, dh_ref, dg_ref, dg8 = refs[N_SPLIT:]
        i, k = pl.program_id(0), pl.program_id(1)

        @pl.when(k == 0)
        def _():
            dh_ref[...] = jnp.zeros_like(dh_ref)

        @pl.when((i == 0) & (k == 0))
        def _():
            dg8[...] = jnp.zeros_like(dg8)
        for s in range(N_SPLIT):
            @pl.when(k // bpz == s)
            def _(s=s):
                dh_ref[...] += lax.dot_general(dz_refs[s][...], w_ref[...], NT_DIMS, preferred_element_type=F32)

        @pl.when(k == nk - 1)
        def _():
            def norm_bwd(r0):
                rows = _rows(r0)
                v = h_ref[rows, :]
                rinv = lax.rsqrt(jnp.mean(v * v, axis=-1, keepdims=True) + EPS)
                xh = v * rinv
                dhn = dh_ref[rows, :]
                part = dhn * xh
                dg8[...] += part[0:8, :] + part[8:16, :]
                dxh = dhn * g_ref[...]
                dh_ref[rows, :] = dho_ref[rows, :] + rinv * (dxh - xh * jnp.mean(dxh * xh, axis=-1, keepdims=True))
            _row_blocks(tm, norm_bwd)
            dg_ref[...] = jnp.sum(dg8[...], axis=0, keepdims=True)

    def dz_spec(s):
        return pl.BlockSpec((tm, tk), lambda i, k: (i, jnp.clip(k - s * bpz, 0, bpz - 1)))

    rows = pl.BlockSpec((tm, d), lambda i, k: (i, 0))
    once = pl.BlockSpec((tm, d), lambda i, k: (i, 0), pipeline_mode=pl.Buffered(1))
    vec = _whole((1, d))
    return _call(
        body, rider, (*dzs, w, h, g, dho), grid=(t // tm, nk),
        in_specs=[dz_spec(s) for s in range(N_SPLIT)]
        + [pl.BlockSpec((None, d, tk), lambda i, k: (k // bps, 0, k % bps)), once, vec, once],
        out_specs=[rows, vec],
        out_shape=[jax.ShapeDtypeStruct((t, d), F32), jax.ShapeDtypeStruct((1, d), F32)],
        scratch_shapes=[pltpu.VMEM((8, d), F32)], semantics=("arbitrary", "arbitrary"), name="din_bwd")


def _branch_b_bwd(z, cb, dyb, wb, lg, lb):
    t = z.shape[0]
    d = wb.shape[1]
    tt = _pick(t, ROW_TILE_EW, HALO_B)
    hb = tt // HALO_B
    nt = t // tt
    rc = _conv_rows(tt)

    def body(vb_ref, gb_ref, vbh_ref, gbh_ref, sb_ref, cb_ref, dyb_ref, wb_ref, lg_ref, lb_ref,
             dvb_ref, dgb_ref, dsb_ref, dwb_ref, dbb_ref, dlg_ref, dlb_ref,
             gext, dext, gs, ds, cb_s, carry, dwb8, vec8):
        i = pl.program_id(0)

        @pl.when(i == 0)
        def _():
            for ref in (carry, dwb8, vec8):
                ref[...] = jnp.zeros_like(ref)
        _glu_ext(gext, vb_ref, gb_ref, vbh_ref, gbh_ref, i == nt - 1, tt)

        def norm_bwd(r0):
            rows = _rows(r0)
            xhat, rstd = _layer_norm_stats(cb_ref[rows, :])
            lg = lg_ref[...]
            ln = xhat * lg + lb_ref[...]
            sl = _sigmoid(ln)
            sb = sb_ref[rows, :]
            ss = _sigmoid(sb)
            dyb = dyb_ref[rows, :]
            dln = dyb * (sb * ss) * _dsilu(ln, sl)
            dsb_ref[rows, :] = (dyb * (ln * sl) * _dsilu(sb, ss)).astype(BF16)
            dxh = dln * lg
            dcb = rstd * (dxh - jnp.mean(dxh, axis=-1, keepdims=True)
                          - xhat * jnp.mean(dxh * xhat, axis=-1, keepdims=True))
            dext[rows, :] = dcb
            for n, val in enumerate((dln * xhat, dln, dcb)):
                vec8[n] += val[0:8, :] + val[8:16, :]
        _row_blocks(tt, norm_bwd)
        dext[tt:, :] = carry[...]
        carry[...] = dext[0:HALO_B, :]
        for c in range(d // LANE_CHUNK):
            cs = slice(c * LANE_CHUNK, (c + 1) * LANE_CHUNK)
            _shifted(gext, gs, cs, tt + HALO_B - 8)
            _shifted(dext, ds, cs, tt + HALO_B - 8)
            for r0 in range(0, tt, rc):
                dcb = dext[r0:r0 + rc, cs]
                acc = jnp.zeros((rc, LANE_CHUNK), F32)
                for k in range(CONV_B):
                    acc = acc + wb_ref[k:k + 1, cs] * _tap(dext, ds, cs, CONV_B - 1 - k, r0, rc)
                    prod = dcb * _tap(gext, gs, cs, HALO_B - (CONV_B - 1) + k, r0, rc)
                    part = prod[0:8, :]
                    for j in range(8, rc, 8):
                        part = part + prod[j:j + 8, :]
                    dwb8[k, :, cs] += part
                cb_s[r0:r0 + rc, cs] = acc

        def glu_bwd(r0):
            rows = _rows(r0)
            dglu = cb_s[rows, :]
            sg = _sigmoid(gb_ref[rows, :])
            dvb_ref[rows, :] = (dglu * sg).astype(BF16)
            dgb_ref[rows, :] = (dglu * vb_ref[rows, :] * (sg * (1.0 - sg))).astype(BF16)
        _row_blocks(tt, glu_bwd)

        @pl.when(i == nt - 1)
        def _():
            dwb_ref[...] = jnp.sum(dwb8[...], axis=1)
            dlg_ref[...] = jnp.sum(vec8[0], axis=0, keepdims=True)
            dlb_ref[...] = jnp.sum(vec8[1], axis=0, keepdims=True)
            dbb_ref[...] = jnp.sum(vec8[2], axis=0, keepdims=True)

    rev = lambda i: nt - 1 - i
    vec = _whole((1, d))
    tile = lambda s: pl.BlockSpec((tt, d), lambda i: (rev(i), s))
    halo = lambda s: pl.BlockSpec((HALO_B, d), lambda i: (jnp.maximum(rev(i) * hb - 1, 0), s))
    otile = pl.BlockSpec((tt, d), lambda i: (rev(i), 0))
    ext = pltpu.VMEM((tt + HALO_B, d), F32)
    shifts = pltpu.VMEM((7, tt + HALO_B - 8, LANE_CHUNK), F32)
    return pl.pallas_call(
        body, grid=(nt,),
        in_specs=[tile(2), tile(3), halo(2), halo(3), tile(4), otile, otile, _whole((CONV_B, d)), vec, vec],
        out_specs=[otile, otile, otile, _whole((CONV_B, d)), vec, vec, vec],
        out_shape=[jax.ShapeDtypeStruct((t, d), BF16)] * 3
        + [jax.ShapeDtypeStruct((CONV_B, d), F32)] + [jax.ShapeDtypeStruct((1, d), F32)] * 3,
        scratch_shapes=[ext, ext, shifts, shifts, pltpu.VMEM((tt, d), F32), pltpu.VMEM((HALO_B, d), F32),
                        pltpu.VMEM((CONV_B, 8, d), F32), pltpu.VMEM((3, 8, d), F32)],
        compiler_params=_params("arbitrary"), name="branch_b_bwd")(z, z, z, z, z, cb, dyb, wb, lg, lb)


def _branch_a_bwd(z, hl, dya, wa, ba, wr, br, wi, bi, lam):
    t = z.shape[0]
    d = wa.shape[1]
    tt = _pick(t, ROW_TILE_EW, HALO_B)
    hb = tt // HALO_A
    nt = t // tt
    nh = d // HEAD_DIM

    def body(xa_ref, xah_ref, sa_ref, hl_ref, hlh_ref, dya_ref, wa_ref, ba_ref, wr_ref, br_ref, wi_ref, bi_ref,
             lam_ref, dxa_ref, dsa_ref, dwa_ref, dba_ref, dbr_ref, dbi_ref, dlam_ref, dwr_ref, dwi_ref,
             xext, hext, dext, gr_s, gi_s, a_s, lam_s, gcarry, dcarry):
        i = pl.program_id(0)
        first = i == nt - 1

        @pl.when(i == 0)
        def _():
            for ref in (dwa_ref, dba_ref, dbr_ref, dbi_ref, dlam_ref, dwr_ref, dwi_ref, gcarry, dcarry):
                ref[...] = jnp.zeros_like(ref)
        xext[0:HALO_A, :] = jnp.where(first, 0.0, xah_ref[...])
        xext[HALO_A:, :] = xa_ref[...]
        hext[0:HALO_A, :] = jnp.where(first, 0.0, hlh_ref[...])
        hext[HALO_A:, :] = hl_ref[...]
        ca = _conv_a(xext, wa_ref, ba_ref[...], tt)
        r, ig = _gates_a(ca, wr_ref, wi_ref, br_ref[...], bi_ref[...], gr_s, gi_s)
        lam = lam_ref[...]
        sp = _softplus(-lam)
        a, mult = _decay_a(r, sp)
        sa = sa_ref[...]
        ss = _sigmoid(sa)
        dya = dya_ref[...]
        dsa_ref[...] = (dya * hl_ref[...] * _dsilu(sa, ss)).astype(BF16)
        a_s[...] = a
        lam_s[...] = dya * (sa * ss)
        gcarry[...] = _scan_rev(a_s, lam_s, lam_s, gcarry[...], tt // 8)
        du = lam_s[...]
        da = du * hext[pl.ds(HALO_A - 1, tt), :]
        dig = du * mult * ca
        dca = du * mult * ig
        dlog_a = da * a - (du * ig * ca) * jnp.where(mult > 0.0, a * a / mult, 0.0)
        dgr = (dlog_a * (-LRU_C * sp)) * r * (1.0 - r)
        dgi = dig * ig * (1.0 - ig)
        dlam_ref[...] += jnp.sum(dlog_a * r, axis=0, keepdims=True) * (LRU_C * _sigmoid(-lam))
        dbr_ref[...] += jnp.sum(dgr, axis=0, keepdims=True)
        dbi_ref[...] += jnp.sum(dgi, axis=0, keepdims=True)
        cab = ca.astype(BF16)
        dgrb = dgr.astype(BF16)
        dgib = dgi.astype(BF16)
        for hh in range(nh):
            sl = slice(hh * HEAD_DIM, (hh + 1) * HEAD_DIM)
            dwr_ref[hh] += lax.dot_general(cab[:, sl], dgrb[:, sl], TN_DIMS, preferred_element_type=F32)
            dwi_ref[hh] += lax.dot_general(cab[:, sl], dgib[:, sl], TN_DIMS, preferred_element_type=F32)
            gr_s[:, sl] = (lax.dot_general(dgrb[:, sl], wr_ref[hh], NT_DIMS, preferred_element_type=F32)
                           + lax.dot_general(dgib[:, sl], wi_ref[hh], NT_DIMS, preferred_element_type=F32))
        dca = dca + gr_s[...]
        dba_ref[...] += jnp.sum(dca, axis=0, keepdims=True)
        dext[0:tt, :] = dca
        dext[tt:, :] = dcarry[...]
        dcarry[...] = dca[0:HALO_A, :]
        dxa = jnp.zeros_like(dca)
        for k in range(CONV_A):
            off = HALO_A - (CONV_A - 1) + k
            dwa_ref[k:k + 1, :] += jnp.sum(dext[0:tt, :] * xext[pl.ds(off, tt), :], axis=0, keepdims=True)
            dxa = dxa + wa_ref[k:k + 1, :] * dext[pl.ds(CONV_A - 1 - k, tt), :]
        dxa_ref[...] = dxa.astype(BF16)

    rev = lambda i: nt - 1 - i
    vec = _whole((1, d))
    hw = _whole((nh, HEAD_DIM, HEAD_DIM))
    tile = lambda s: pl.BlockSpec((tt, d), lambda i: (rev(i), s))
    halo = pl.BlockSpec((HALO_A, d), lambda i: (jnp.maximum(rev(i) * hb - 1, 0), 0))
    big = pltpu.VMEM((tt + HALO_A, d), F32)
    full = pltpu.VMEM((tt, d), F32)
    return pl.pallas_call(
        body, grid=(nt,),
        in_specs=[tile(0), halo, tile(1), tile(0), halo, tile(0), _whole((CONV_A, d)), vec, hw, vec, hw, vec, vec],
        out_specs=[tile(0), tile(0), _whole((CONV_A, d)), vec, vec, vec, vec, hw, hw],
        out_shape=[jax.ShapeDtypeStruct((t, d), BF16)] * 2 + [jax.ShapeDtypeStruct((CONV_A, d), F32)]
        + [jax.ShapeDtypeStruct((1, d), F32)] * 4 + [jax.ShapeDtypeStruct((nh, HEAD_DIM, HEAD_DIM), F32)] * 2,
        scratch_shapes=[big, big, big, full, full, full, full, pltpu.VMEM((1, d), F32), pltpu.VMEM((HALO_A, d), F32)],
        compiler_params=_params("arbitrary"), name="branch_a_bwd")(z, z, z, hl, hl, dya, wa, ba, wr, br, wi, bi, lam)


def _rows_block(rows, cols, n_arrays=1):
    cap = max(8, BLOCK_BYTES // (4 * cols * n_arrays))
    return rows if rows <= cap else _pick(rows, cap, 8)


def _sum_halves(g, xo, c_idx, dtype):
    nq, r, c = g.shape
    rh = r // 2
    tr = _rows_block(rh, c)
    nb = rh // tr

    def body(c_ref, g_ref, x_ref, o_ref):
        o_ref[...] = (g_ref[...] + x_ref[...]).astype(dtype)

    grid_spec = pltpu.PrefetchScalarGridSpec(
        num_scalar_prefetch=1, grid=(nq, nb),
        in_specs=[pl.BlockSpec((None, tr, c), lambda q, i, cr: (q, cr[0] * nb + i, 0)),
                  pl.BlockSpec((None, tr, c), lambda q, i, cr: (q, i, 0))],
        out_specs=pl.BlockSpec((None, tr, c), lambda q, i, cr: (q, i, 0)))
    return pl.pallas_call(
        body, grid_spec=grid_spec, out_shape=jax.ShapeDtypeStruct((nq, rh, c), dtype),
        compiler_params=_params("parallel", "parallel"), name="sum_halves")(c_idx, g, xo)


def _sum_slots(y, name):
    ns, r, c = y.shape
    tr = _rows_block(r, c, ns)

    def body(y_ref, o_ref):
        acc = y_ref[0].astype(F32)
        for s in range(1, ns):
            acc = acc + y_ref[s].astype(F32)
        o_ref[...] = acc

    return pl.pallas_call(
        body, grid=(r // tr,), in_specs=[pl.BlockSpec((ns, tr, c), lambda i: (0, i, 0))],
        out_specs=pl.BlockSpec((tr, c), lambda i: (i, 0)), out_shape=jax.ShapeDtypeStruct((r, c), F32),
        compiler_params=_params("parallel"), name=name)(y)


def _adamw(w, g, m, v, name):
    shape = w.shape
    c = shape[-1]
    r = w.size // c
    tr = _rows_block(r, c)
    c1 = 1.0 / (1.0 - ADAM_B1 ** ADAM_STEP)
    c2 = 1.0 / (1.0 - ADAM_B2 ** ADAM_STEP)

    def body(w_ref, g_ref, m_ref, v_ref, go_ref, d_ref, mo_ref, vo_ref):
        gv = g_ref[...]
        mn = ADAM_B1 * m_ref[...] + (1.0 - ADAM_B1) * gv
        vn = ADAM_B2 * v_ref[...] + (1.0 - ADAM_B2) * (gv * gv)
        go_ref[...] = gv
        mo_ref[...] = mn
        vo_ref[...] = vn
        d_ref[...] = -ADAM_LR * ((mn * c1) / (jnp.sqrt(vn * c2) + ADAM_EPS) + ADAM_WD * w_ref[...])

    blk = pl.BlockSpec((tr, c), lambda i: (i, 0))
    outs = pl.pallas_call(
        body, grid=(r // tr,), in_specs=[blk] * 4, out_specs=[blk] * 4,
        out_shape=[jax.ShapeDtypeStruct((r, c), F32)] * 4,
        compiler_params=_params("parallel"), name=name)(*(a.reshape(r, c) for a in (w, g, m, v)))
    return tuple(o.reshape(shape) for o in outs)


def _adamw_layers(w, mine, theirs, m, v, c_idx, name):
    depth, r, c = w.shape
    tr = _rows_block(r // 2, c, 2)
    nbh = r // 2 // tr
    c1 = 1.0 / (1.0 - ADAM_B1 ** ADAM_STEP)
    c2 = 1.0 / (1.0 - ADAM_B2 ** ADAM_STEP)

    def body(c_ref, w_ref, m_ref, v_ref, *rest):
        g_refs = rest[:2 * depth]
        go_ref, d_ref, mo_ref, vo_ref = rest[2 * depth:]
        layer = pl.program_id(0)
        own = (pl.program_id(1) // nbh) == c_ref[0]

        def update(g_ref):
            gv = g_ref[...]
            mn = ADAM_B1 * m_ref[...] + (1.0 - ADAM_B1) * gv
            vn = ADAM_B2 * v_ref[...] + (1.0 - ADAM_B2) * (gv * gv)
            go_ref[...] = gv
            mo_ref[...] = mn
            vo_ref[...] = vn
            d_ref[...] = -ADAM_LR * ((mn * c1) / (jnp.sqrt(vn * c2) + ADAM_EPS) + ADAM_WD * w_ref[...])

        for s in range(depth):
            @pl.when((layer == s) & own)
            def _(s=s):
                update(g_refs[2 * s])

            @pl.when((layer == s) & jnp.logical_not(own))
            def _(s=s):
                update(g_refs[2 * s + 1])

    stacked = pl.BlockSpec((None, tr, c), lambda l, i, cr: (l, i, 0))

    def g_spec(s, is_mine):
        def imap(l, i, cr):
            own = (i // nbh) == cr[0]
            on = (l == s) & (own if is_mine else jnp.logical_not(own))
            return (jnp.where(on, i % nbh, 0), 0)
        return pl.BlockSpec((tr, c), imap)

    g_specs = [g_spec(s, is_mine) for s in range(depth) for is_mine in (True, False)]
    gs = [a for s in range(depth) for a in (mine[s], theirs[s])]
    grid_spec = pltpu.PrefetchScalarGridSpec(
        num_scalar_prefetch=1, grid=(depth, 2 * nbh), in_specs=[stacked] * 3 + g_specs, out_specs=[stacked] * 4)
    return pl.pallas_call(
        body, grid_spec=grid_spec, out_shape=[jax.ShapeDtypeStruct(w.shape, F32)] * 4,
        compiler_params=_params("arbitrary", "arbitrary"), name=name)(c_idx, w, m, v, *gs)


HBM_SPEC = pl.BlockSpec(memory_space=pltpu.HBM)


def _place():
    return lax.axis_index("x"), lax.axis_index("y"), lax.axis_index("c")


def _other_chips(x, y):
    return [(1 - x, y), (x, 1 - y), (1 - x, 1 - y)]


def _half(ref, which):
    rows = ref.shape[0] // 2
    return ref.at[pl.ds(which * rows, rows)]


def _gather_chips(arrays, name):
    partial = _run_rider(_gather_ici_rider(arrays), name + "_ici")
    return _run_rider(_gather_d2d_rider(partial), name + "_d2d")


class _Rider:
    def __init__(self, operands, out_shape, scratch, start, finish, aliases=None):
        self.operands, self.out_shape, self.scratch = list(operands), list(out_shape), list(scratch)
        self.start, self.finish, self.aliases = start, finish, dict(aliases or {})


def _run_rider(rider, name):
    n_in, n_out = len(rider.operands), len(rider.out_shape)

    def body(*refs):
        ins, outs, sems = refs[:n_in], refs[n_in:n_in + n_out], refs[n_in + n_out:]
        rider.start(ins, outs, sems)
        rider.finish(ins, outs, sems)

    return pl.pallas_call(
        body, in_specs=[HBM_SPEC] * n_in, out_specs=[HBM_SPEC] * n_out, out_shape=rider.out_shape,
        scratch_shapes=rider.scratch, input_output_aliases=rider.aliases, name=name)(*rider.operands)


def _hosted(body, rider, *, grid, in_specs, out_specs, out_shape, scratch_shapes, name):
    n_in, n_out, n_scr = len(in_specs), len(out_specs), len(scratch_shapes)
    r_in, r_out = len(rider.operands), len(rider.out_shape)

    def wrapped(*refs):
        ins, refs = refs[:n_in], refs[n_in:]
        r_ins, refs = refs[:r_in], refs[r_in:]
        outs, refs = refs[:n_out], refs[n_out:]
        r_outs, refs = refs[:r_out], refs[r_out:]
        scr, sems = refs[:n_scr], refs[n_scr:]
        first = functools.reduce(jnp.logical_and, [pl.program_id(a) == 0 for a in range(len(grid))])
        last = functools.reduce(jnp.logical_and, [pl.program_id(a) == g - 1 for a, g in enumerate(grid)])

        @pl.when(first)
        def _():
            rider.start(r_ins, r_outs, sems)
        body(*ins, *outs, *scr)

        @pl.when(last)
        def _():
            rider.finish(r_ins, r_outs, sems)

    res = pl.pallas_call(
        wrapped, grid=grid, in_specs=list(in_specs) + [HBM_SPEC] * r_in, out_specs=list(out_specs) + [HBM_SPEC] * r_out,
        out_shape=list(out_shape) + rider.out_shape, scratch_shapes=list(scratch_shapes) + rider.scratch,
        input_output_aliases={n_in + i: n_out + o for i, o in rider.aliases.items()},
        compiler_params=_params(*(["arbitrary"] * len(grid))), name=name)
    return lambda *operands: (lambda r: (r[:n_out], r[n_out:]))(res(*operands, *rider.operands))


def _gather_ici_rider(arrays):
    n = len(arrays)

    def copies(ins, outs, sems):
        send, recv = sems
        x, y, c = _place()
        me = 2 * x + y
        chips = _other_chips(x, y)
        sends, arrivals = [], []
        for a in range(n):
            for group in (sends, arrivals):
                group.append(pltpu.make_async_remote_copy(
                    src_ref=ins[a], dst_ref=outs[a].at[me], send_sem=send.at[a, 3], recv_sem=recv.at[a, 3],
                    device_id=(x, y, 1 - c), device_id_type=MESH))
            for j in range(3):
                q = 2 * chips[j][0] + chips[j][1]
                for dst, group in ((_half(outs[a].at[me], c), sends), (_half(outs[a].at[q], c), arrivals)):
                    group.append(pltpu.make_async_remote_copy(
                        src_ref=_half(ins[a], c), dst_ref=dst, send_sem=send.at[a, j], recv_sem=recv.at[a, j],
                        device_id=(*chips[j], c), device_id_type=MESH))
        return sends, arrivals

    def start(ins, outs, sems):
        for cp in copies(ins, outs, sems)[0]:
            cp.start()

    def finish(ins, outs, sems):
        sends, arrivals = copies(ins, outs, sems)
        for cp in arrivals:
            cp.wait_recv()
        for cp in sends:
            cp.wait_send()

    sems = pltpu.SemaphoreType.DMA((n, 4))
    return _Rider(arrays, [jax.ShapeDtypeStruct((N_CHIPS,) + a.shape, a.dtype) for a in arrays],
                  [sems, sems], start, finish)


def _gather_d2d_rider(partial):
    n = len(partial)

    def copies(outs, sems):
        send, recv = sems
        x, y, c = _place()
        chips = _other_chips(x, y)
        sends, arrivals = [], []
        for a in range(n):
            for j in range(3):
                q = 2 * chips[j][0] + chips[j][1]
                for which, group in ((c, sends), (1 - c, arrivals)):
                    ref = _half(outs[a].at[q], which)
                    group.append(pltpu.make_async_remote_copy(
                        src_ref=ref, dst_ref=ref, send_sem=send.at[a, j], recv_sem=recv.at[a, j],
                        device_id=(x, y, 1 - c), device_id_type=MESH))
        return sends, arrivals

    def start(ins, outs, sems):
        for cp in copies(outs, sems)[0]:
            cp.start()

    def finish(ins, outs, sems):
        sends, arrivals = copies(outs, sems)
        for cp in arrivals:
            cp.wait_recv()
        for cp in sends:
            cp.wait_send()

    sems = pltpu.SemaphoreType.DMA((n, 3))
    return _Rider(partial, [jax.ShapeDtypeStruct(a.shape, a.dtype) for a in partial], [sems, sems], start, finish,
                  aliases={k: k for k in range(n)})


def _swap_halves_rider(grads):
    n = len(grads)

    def copies(ins, outs, sems):
        send, recv = sems
        x, y, c = _place()
        made = []
        for a in range(n):
            rows = ins[a].shape[1] // 2
            made.append(pltpu.make_async_remote_copy(
                src_ref=ins[a].at[:, pl.ds((1 - c) * rows, rows)], dst_ref=outs[a],
                send_sem=send.at[a], recv_sem=recv.at[a], device_id=(x, y, 1 - c), device_id_type=MESH))
        return made

    def start(ins, outs, sems):
        for cp in copies(ins, outs, sems):
            cp.start()

    def finish(ins, outs, sems):
        for cp in copies(ins, outs, sems):
            cp.wait()

    sems = pltpu.SemaphoreType.DMA((n,))
    return _Rider(grads, [jax.ShapeDtypeStruct((g.shape[0], g.shape[1] // 2) + g.shape[2:], g.dtype) for g in grads],
                  [sems, sems], start, finish)


def _scatter_rider(parts):
    n = len(parts)

    def copies(ins, outs, sems):
        send, recv, lsem = sems
        x, y, c = _place()
        me = 2 * x + y
        chips = _other_chips(x, y)
        local = [pltpu.make_async_copy(ins[a].at[me], outs[a].at[me], lsem.at[a]) for a in range(n)]
        sends, arrivals = [], []
        for a in range(n):
            for j in range(3):
                q = 2 * chips[j][0] + chips[j][1]
                for dst, group in ((outs[a].at[me], sends), (outs[a].at[q], arrivals)):
                    group.append(pltpu.make_async_remote_copy(
                        src_ref=ins[a].at[q], dst_ref=dst, send_sem=send.at[a, j], recv_sem=recv.at[a, j],
                        device_id=(*chips[j], c), device_id_type=MESH))
        return local, sends, arrivals

    def start(ins, outs, sems):
        local, sends, _ = copies(ins, outs, sems)
        for cp in local + sends:
            cp.start()

    def finish(ins, outs, sems):
        local, sends, arrivals = copies(ins, outs, sems)
        for cp in arrivals:
            cp.wait_recv()
        for cp in sends:
            cp.wait_send()
        for cp in local:
            cp.wait()

    sems = pltpu.SemaphoreType.DMA((n, 3))
    return _Rider(parts, [jax.ShapeDtypeStruct(p.shape, p.dtype) for p in parts],
                  [sems, sems, pltpu.SemaphoreType.DMA((n,))], start, finish)


def _swap_sibling_rider(arrays):
    n = len(arrays)

    def copies(ins, outs, sems):
        send, recv = sems
        x, y, c = _place()
        return [pltpu.make_async_remote_copy(
            src_ref=ins[a], dst_ref=outs[a], send_sem=send.at[a], recv_sem=recv.at[a],
            device_id=(x, y, 1 - c), device_id_type=MESH) for a in range(n)]

    def start(ins, outs, sems):
        for cp in copies(ins, outs, sems):
            cp.start()

    def finish(ins, outs, sems):
        for cp in copies(ins, outs, sems):
            cp.wait()

    sems = pltpu.SemaphoreType.DMA((n,))
    return _Rider(arrays, [jax.ShapeDtypeStruct(a.shape, a.dtype) for a in arrays], [sems, sems], start, finish)


def _join_riders(first, second):
    n_in, n_out, n_sem = len(first.operands), len(first.out_shape), len(first.scratch)

    def both(method):
        def run(ins, outs, sems):
            getattr(first, method)(ins[:n_in], outs[:n_out], sems[:n_sem])
            getattr(second, method)(ins[n_in:], outs[n_out:], sems[n_sem:])
        return run

    aliases = {**first.aliases, **{n_in + i: n_out + o for i, o in second.aliases.items()}}
    return _Rider(first.operands + second.operands, first.out_shape + second.out_shape,
                  first.scratch + second.scratch, both("start"), both("finish"), aliases)


def _layer_fwd(h, p, next_shards):
    ride = None if next_shards is None else _gather_ici_rider(next_shards)
    (z, hn), partial = _in_proj_fwd(h, p["norm_g"], p["w_in"], ride)
    ya, hl = _branch_a_fwd(z, p["conv_a_w"], p["conv_a_b"], p["w_rg"], p["b_rg"], p["w_ig"], p["b_ig"], p["lam"])
    ride = None if partial is None else _gather_d2d_rider(partial)
    (yb, cb), gathered = _branch_b_fwd(z, p["conv_b_w"], p["conv_b_b"], p["ln_b_g"], p["ln_b_b"], ride)
    merged, pa, pb = _merge_fwd(ya, yb, z, p["w_pa"], p["w_pb"])
    return _out_fwd(h, merged, p["w_out"]), (h, hn, z, hl, cb, ya, yb, pa, pb, merged), gathered


def _layer_bwd(dho, p, saved, chip_sums, c_idx):
    h, hn, z, hl, cb, ya, yb, pa, pb, merged = saved
    dpa, dpb, dma, dmb = _dmerged_bwd(dho, p["w_out"], z, pa, pb)
    g = {"w_out": _matmul_tn(merged, dho, "dw_out"), "w_proj_a": _matmul_tn(ya, dpa, "dw_proj_a"),
         "w_proj_b": _matmul_tn(yb, dpb, "dw_proj_b")}
    dya = _matmul_nt(dpa, p["w_pa"], "dya")
    dyb = _matmul_nt(dpb, p["w_pb"], "dyb")
    (dxa, dsa, g["conv_a_w"], g["conv_a_b"], g["b_rg"], g["b_ig"], g["lru_lambda"], g["w_rg"], g["w_ig"]) = \
        _branch_a_bwd(z, hl, dya, p["conv_a_w"], p["conv_a_b"], p["w_rg"], p["b_rg"], p["w_ig"], p["b_ig"], p["lam"])
    dvb, dgb, dsb, g["conv_b_w"], g["conv_b_b"], g["ln_b_g"], g["ln_b_b"] = \
        _branch_b_bwd(z, cb, dyb, p["conv_b_w"], p["ln_b_g"], p["ln_b_b"])
    dzs = (dxa, dsa, dvb, dgb, dsb, dma, dmb)
    (g["w_in"],), landed = _dwin_bwd(hn, dzs, None if chip_sums is None else _scatter_rider(chip_sums))
    d = h.shape[1]
    parts = [g["w_in"]] + [g[k].reshape(N_CHIPS, d // N_CHIPS, d) for k in BIG[1:]]
    rider = _swap_halves_rider(parts)
    mine = None if landed is None else [_sum_slots(y, "sum_chips") for y in landed]
    if mine is not None:
        rider = _join_riders(rider, _swap_sibling_rider(mine))
    (dh, g["norm_g"]), swapped = _din_bwd(dzs, p["w_in"], h, p["norm_g"], dho, rider)
    other, theirs = swapped[:len(parts)], swapped[len(parts):]
    above = None if mine is None else (mine, theirs)
    return dh, g, above, [_sum_halves(a, b, c_idx, BF16) for a, b in zip(parts, other)]


BIG = ("w_in", "w_proj_a", "w_proj_b", "w_out")
VECS = ("norm_g", "conv_a_b", "b_rg", "b_ig", "lru_lambda", "conv_b_b", "ln_b_g", "ln_b_b")


def _reduce_finish(landed):
    mine = [_sum_slots(y, "sum_chips") for y in landed]
    return mine, _run_rider(_swap_sibling_rider(mine), "swap_sibling")


def _forward_backward(h, target, layers, shards, final_g, c_idx):
    depth = len(layers)
    d = h.shape[1]
    gathered = _gather_chips(shards[0], "gather_weights")
    params, saved = [], []
    for l in range(depth):
        win, wpa, wpb, wo = gathered
        params.append({**layers[l], "w_in": win, "w_pa": wpa.reshape(d, d), "w_pb": wpb.reshape(d, d),
                       "w_out": wo.reshape(d, d)})
        h, s, gathered = _layer_fwd(h, params[l], shards[l + 1] if l + 1 < depth else None)
        saved.append(s)
    target = jnp.concatenate([jnp.zeros((N_META, d), F32), target], axis=0)
    loss_lanes, dh, g_final = _loss_head(h, final_g, target)
    grads, reduced, chip_sums = [None] * depth, [None] * depth, None
    for l in reversed(range(depth)):
        dh, grads[l], above, chip_sums = _layer_bwd(dh, params[l], saved[l], chip_sums, c_idx)
        if above is not None:
            reduced[l + 1] = above
    return loss_lanes, dh, g_final, grads, reduced, chip_sums


def kernel(x, meta, norm_g, w_in, conv_a_w, conv_a_b, w_rg, b_rg, w_ig, b_ig, lru_lambda, conv_b_w, conv_b_b, ln_b_g, ln_b_b, w_proj_a, w_proj_b, w_out, final_g, loss_target, m_meta, m_norm_g, m_w_in, m_conv_a_w, m_conv_a_b, m_w_rg, m_b_rg, m_w_ig, m_b_ig, m_lru_lambda, m_conv_b_w, m_conv_b_b, m_ln_b_g, m_ln_b_b, m_w_proj_a, m_w_proj_b, m_w_out, m_final_g, v_meta, v_norm_g, v_w_in, v_conv_a_w, v_conv_a_b, v_w_rg, v_b_rg, v_w_ig, v_b_ig, v_lru_lambda, v_conv_b_w, v_conv_b_b, v_ln_b_g, v_ln_b_b, v_w_proj_a, v_w_proj_b, v_w_out, v_final_g):
    names = ("meta", "norm_g", "w_in", "conv_a_w", "conv_a_b", "w_rg", "b_rg", "w_ig", "b_ig", "lru_lambda",
             "conv_b_w", "conv_b_b", "ln_b_g", "ln_b_b", "w_proj_a", "w_proj_b", "w_out", "final_g")
    w = dict(zip(names, (meta, norm_g, w_in, conv_a_w, conv_a_b, w_rg, b_rg, w_ig, b_ig, lru_lambda, conv_b_w,
                         conv_b_b, ln_b_g, ln_b_b, w_proj_a, w_proj_b, w_out, final_g)))
    m = dict(zip(names, (m_meta, m_norm_g, m_w_in, m_conv_a_w, m_conv_a_b, m_w_rg, m_b_rg, m_w_ig, m_b_ig,
                         m_lru_lambda, m_conv_b_w, m_conv_b_b, m_ln_b_g, m_ln_b_b, m_w_proj_a, m_w_proj_b, m_w_out,
                         m_final_g)))
    v = dict(zip(names, (v_meta, v_norm_g, v_w_in, v_conv_a_w, v_conv_a_b, v_w_rg, v_b_rg, v_w_ig, v_b_ig,
                         v_lru_lambda, v_conv_b_w, v_conv_b_b, v_ln_b_g, v_ln_b_b, v_w_proj_a, v_w_proj_b, v_w_out,
                         v_final_g)))
    depth, d = norm_g.shape
    dq = d // N_CHIPS
    px, py, pc = _place()
    chip = 2 * px + py
    c_idx = pc.astype(jnp.int32).reshape(1)

    meta_s, conv_a_s, conv_b_s = _gather_chips([meta, conv_a_w, conv_b_w], "gather_small")
    unshard = lambda s: jnp.concatenate([s[q] for q in range(N_CHIPS)], axis=-1)
    meta_f, conv_a_f, conv_b_f = unshard(meta_s), unshard(conv_a_s), unshard(conv_b_s)
    layers, shards = [], []
    for l in range(depth):
        shards.append([w_in[l].astype(BF16), w_proj_a[l].astype(BF16), w_proj_b[l].astype(BF16),
                       w_out[l].astype(BF16)])
        layers.append({
            "norm_g": norm_g[l][None], "conv_a_w": conv_a_f[l], "conv_a_b": conv_a_b[l][None],
            "w_rg": w_rg[l].astype(BF16), "b_rg": b_rg[l][None], "w_ig": w_ig[l].astype(BF16), "b_ig": b_ig[l][None],
            "lam": lru_lambda[l][None], "conv_b_w": conv_b_f[l], "conv_b_b": conv_b_b[l][None],
            "ln_b_g": ln_b_g[l][None], "ln_b_b": ln_b_b[l][None]})

    loss_lanes, dh, g_final, small, reduced, chip_sums = _forward_backward(
        jnp.concatenate([meta_f, x[0]], axis=0), loss_target[0], layers, shards, final_g[None], c_idx)
    loss = lax.psum(jnp.sum(loss_lanes), ("x", "y", "c"))
    grads = {}

    vec_rows = [small[l][k] for k in VECS for l in range(depth)]
    vec_rows += [small[l]["conv_a_w"] for l in range(depth)] + [small[l]["conv_b_w"] for l in range(depth)]
    vec_rows += [dh[:N_META], g_final]
    n_rows = sum(r.shape[0] for r in vec_rows)
    vec_rows.append(jnp.zeros((-n_rows % (16 * N_CHIPS), d), F32))
    packed = [jnp.concatenate(vec_rows, axis=0).reshape(N_CHIPS, -1, d),
              jnp.concatenate([small[l][k].reshape(-1, HEAD_DIM) for k in ("w_rg", "w_ig") for l in range(depth)],
                              axis=0).reshape(N_CHIPS, -1, HEAD_DIM)]
    other = _run_rider(_swap_halves_rider(packed), "swap_halves_small")
    small_sums = [_sum_halves(packed[0], other[0], c_idx, F32), _sum_halves(packed[1], other[1], c_idx, BF16)]
    mine, theirs = _reduce_finish(_run_rider(_scatter_rider(chip_sums + small_sums), "scatter_chips"))
    reduced[0] = (mine[:len(BIG)], theirs[:len(BIG)])
    joined = [jnp.concatenate([jnp.where(pc == 0, a, b), jnp.where(pc == 0, b, a)], axis=0)
              for a, b in zip(mine[len(BIG):], theirs[len(BIG):])]
    sum_v, sum_g = _gather_chips(joined, "gather_small_grads")
    sum_v, sum_g = sum_v.reshape(-1, d), sum_g.reshape(-1, HEAD_DIM)
    cols = lambda a: lax.dynamic_slice_in_dim(a, chip * dq, dq, axis=-1)
    for n_k, k in enumerate(VECS):
        grads[k] = sum_v[n_k * depth:(n_k + 1) * depth]
    o = len(VECS) * depth
    grads["conv_a_w"] = cols(sum_v[o:o + depth * CONV_A].reshape(depth, CONV_A, d))
    o += depth * CONV_A
    grads["conv_b_w"] = cols(sum_v[o:o + depth * CONV_B].reshape(depth, CONV_B, d))
    o += depth * CONV_B
    grads["meta"] = cols(sum_v[o:o + N_META])
    grads["final_g"] = sum_v[o + N_META]
    gate_rows = depth * (d // HEAD_DIM) * HEAD_DIM
    grads["w_rg"] = sum_g[:gate_rows].reshape(w_rg.shape)
    grads["w_ig"] = sum_g[gate_rows:].reshape(w_ig.shape)

    outs = {k: _adamw(w[k], grads[k], m[k], v[k], f"adamw_{k}") for k in names if k not in BIG}
    for n_k, k in enumerate(BIG):
        outs[k] = _adamw_layers(w[k], [reduced[l][0][n_k] for l in range(depth)],
                                [reduced[l][1][n_k] for l in range(depth)], m[k], v[k], c_idx, f"adamw_{k}")
    dx = dh[N_META:][None]
    return (loss, dx, *[outs[k][0] for k in names], *[outs[k][1] for k in names],
            *[outs[k][2] for k in names], *[outs[k][3] for k in names])
```

```python
import functools

import jax
import jax.numpy as jnp
from jax import lax
from jax.experimental import pallas as pl
from jax.experimental.pallas import tpu as pltpu

F32 = jnp.float32
BF16 = jnp.bfloat16
MESH = pl.DeviceIdType.MESH

EPS = 1e-6
N_META = 16
HEAD_DIM = 128
CONV_A = 4
CONV_B = 31
LRU_C = 8.0
N_SPLIT = 7
N_CHIPS = 4

ADAM_LR = 0.001
ADAM_B1 = 0.9
ADAM_B2 = 0.999
ADAM_EPS = 1e-08
ADAM_WD = 0.01
ADAM_STEP = 10

HALO_A = 16
HALO_B = 48
ROW_TILE_MM = 912
ROW_TILE_EW = 144
CONV_ROWS = 72
ROW_BLOCK = 16
LANE_CHUNK = 256
COL_TILE = 512
COL_TILE_WIDE = 1024
COL_TILE_IN = 1792
VMEM_LIMIT = 56 * 1024 * 1024
BLOCK_BYTES = 2 * 1024 * 1024

NT_DIMS = (((1,), (1,)), ((), ()))
TN_DIMS = (((0,), (0,)), ((), ()))


def _pick(n, cap, mult):
    best = None
    for d in range(mult, min(n, cap) + 1, mult):
        if n % d == 0:
            best = d
    if best is None:
        raise ValueError(f"no tile for {n} (cap {cap}, multiple of {mult})")
    return best


def _params(*semantics):
    return pltpu.CompilerParams(dimension_semantics=semantics, vmem_limit_bytes=VMEM_LIMIT)


def _whole(shape):
    return pl.BlockSpec(shape, lambda *_: (0,) * len(shape))


def _sigmoid(v):
    return 0.5 * jnp.tanh(0.5 * v) + 0.5


def _dsilu(v, s):
    return s * (1.0 + v * (1.0 - s))


def _log1p(e):
    w = 1.0 + e
    return jnp.where(w == 1.0, e, e * jnp.log(w) / (w - 1.0))


def _softplus(v):
    return jnp.maximum(v, 0.0) + _log1p(jnp.exp(-jnp.abs(v)))


def _neg_expm1_twice(v):
    t = jnp.tanh(v)
    return -2.0 * t / (1.0 - t)


def _row_iota(shape):
    return lax.broadcasted_iota(jnp.int32, shape, 0)


def _call(body, rider, operands, *, grid, in_specs, out_specs, out_shape, scratch_shapes, semantics, name):
    if rider is None:
        return pl.pallas_call(
            body, grid=grid, in_specs=in_specs, out_specs=out_specs, out_shape=out_shape,
            scratch_shapes=scratch_shapes, compiler_params=_params(*semantics), name=name)(*operands), None
    return _hosted(body, rider, grid=grid, in_specs=in_specs, out_specs=out_specs, out_shape=out_shape,
                   scratch_shapes=scratch_shapes, name=name)(*operands)


def _in_proj_fwd(h, g, w, rider=None):
    t, d = h.shape
    nq = w.shape[2]
    tm = _pick(t, ROW_TILE_MM, 16)
    tn = _pick(nq, COL_TILE_IN, 128)
    bps = nq // tn

    def body(h_ref, g_ref, w_ref, z_ref, hn_ref):
        @pl.when(pl.program_id(1) == 0)
        def _():
            def norm(r0):
                rows = _rows(r0)
                v = h_ref[rows, :]
                rinv = lax.rsqrt(jnp.mean(v * v, axis=-1, keepdims=True) + EPS)
                hn_ref[rows, :] = (v * rinv * g_ref[...]).astype(BF16)
            _row_blocks(tm, norm)
        z_ref[...] = jnp.dot(hn_ref[...], w_ref[...], preferred_element_type=F32)

    return _call(
        body, rider, (h, g, w), grid=(t // tm, N_CHIPS * bps),
        in_specs=[pl.BlockSpec((tm, d), lambda i, j: (i, 0), pipeline_mode=pl.Buffered(1)), _whole((1, d)),
                  pl.BlockSpec((None, d, tn), lambda i, j: (j // bps, 0, j % bps))],
        out_specs=[pl.BlockSpec((tm, tn), lambda i, j: (i, j)), pl.BlockSpec((tm, d), lambda i, j: (i, 0))],
        out_shape=[jax.ShapeDtypeStruct((t, N_CHIPS * nq), F32), jax.ShapeDtypeStruct((t, d), BF16)],
        scratch_shapes=[], semantics=("parallel", "arbitrary"), name="in_proj_fwd")


def _conv_a(xext, wa_ref, ba, tt):
    acc = ba
    for k in range(CONV_A):
        acc = acc + wa_ref[k:k + 1, :] * xext[pl.ds(HALO_A - (CONV_A - 1) + k, tt), :]
    return acc


def _gates_a(ca, wr_ref, wi_ref, br, bi, gr_s, gi_s):
    cab = ca.astype(BF16)
    for hh in range(ca.shape[1] // HEAD_DIM):
        sl = slice(hh * HEAD_DIM, (hh + 1) * HEAD_DIM)
        gr_s[:, sl] = jnp.dot(cab[:, sl], wr_ref[hh], preferred_element_type=F32)
        gi_s[:, sl] = jnp.dot(cab[:, sl], wi_ref[hh], preferred_element_type=F32)
    return _sigmoid(gr_s[...] + br), _sigmoid(gi_s[...] + bi)


def _decay_a(r, sp):
    log_a = -LRU_C * r * sp
    a = jnp.exp(log_a)
    mult = jnp.sqrt(jnp.maximum(_neg_expm1_twice(log_a), 0.0))
    return a, mult


def _scan_fwd(a_ref, u_ref, out_ref, h0, nblk):
    def blk(b, hprev):
        rows = pl.ds(pl.multiple_of(b * 8, 8), 8)
        ca, cb = a_ref[rows, :], u_ref[rows, :]
        row = _row_iota(ca.shape)
        for dist in (1, 2, 4):
            m = row >= dist
            cb = jnp.where(m, ca * pltpu.roll(cb, dist, 0) + cb, cb)
            ca = jnp.where(m, ca * pltpu.roll(ca, dist, 0), ca)
        hb = ca * hprev + cb
        out_ref[rows, :] = hb
        return hb[7:8, :]
    return lax.fori_loop(0, nblk, blk, h0)


def _scan_rev(a_ref, d_ref, out_ref, g0, nblk):
    def blk(k, g):
        rows = pl.ds(pl.multiple_of((nblk - 1 - k) * 8, 8), 8)
        a, cb = a_ref[rows, :], d_ref[rows, :]
        row = _row_iota(a.shape)
        ca = jnp.where(row == 7, 1.0, pltpu.roll(a, 7, 0))
        for dist in (1, 2, 4):
            m = row < 8 - dist
            cb = jnp.where(m, cb + ca * pltpu.roll(cb, 8 - dist, 0), cb)
            ca = jnp.where(m, ca * pltpu.roll(ca, 8 - dist, 0), ca)
        lam = cb + ca * g
        out_ref[rows, :] = lam
        return a[0:1, :] * lam[0:1, :]
    return lax.fori_loop(0, nblk, blk, g0)


def _branch_a_fwd(z, wa, ba, wr, br, wi, bi, lam):
    t = z.shape[0]
    d = wa.shape[1]
    tt = _pick(t, ROW_TILE_EW, HALO_B)
    hb = tt // HALO_A
    nh = d // HEAD_DIM

    def body(xa_ref, halo_ref, sa_ref, wa_ref, ba_ref, wr_ref, br_ref, wi_ref, bi_ref, lam_ref,
             ya_ref, hl_ref, xext, gr_s, gi_s, a_s, carry):
        i = pl.program_id(0)

        @pl.when(i == 0)
        def _():
            carry[...] = jnp.zeros_like(carry)
        xext[0:HALO_A, :] = jnp.where(i == 0, 0.0, halo_ref[...])
        xext[HALO_A:, :] = xa_ref[...]
        ca = _conv_a(xext, wa_ref, ba_ref[...], tt)
        r, ig = _gates_a(ca, wr_ref, wi_ref, br_ref[...], bi_ref[...], gr_s, gi_s)
        a, mult = _decay_a(r, _softplus(-lam_ref[...]))
        a_s[...] = a
        hl_ref[...] = mult * (ig * ca)
        carry[...] = _scan_fwd(a_s, hl_ref, hl_ref, carry[...], tt // 8)
        sa = sa_ref[...]
        ya_ref[...] = (hl_ref[...] * (sa * _sigmoid(sa))).astype(BF16)

    vec = _whole((1, d))
    return pl.pallas_call(
        body, grid=(t // tt,),
        in_specs=[pl.BlockSpec((tt, d), lambda i: (i, 0)),
                  pl.BlockSpec((HALO_A, d), lambda i: (jnp.maximum(i * hb - 1, 0), 0)),
                  pl.BlockSpec((tt, d), lambda i: (i, 1)),
                  _whole((CONV_A, d)), vec, _whole((nh, HEAD_DIM, HEAD_DIM)), vec,
                  _whole((nh, HEAD_DIM, HEAD_DIM)), vec, vec],
        out_specs=[pl.BlockSpec((tt, d), lambda i: (i, 0)), pl.BlockSpec((tt, d), lambda i: (i, 0))],
        out_shape=[jax.ShapeDtypeStruct((t, d), BF16), jax.ShapeDtypeStruct((t, d), F32)],
        scratch_shapes=[pltpu.VMEM((tt + HALO_A, d), F32), pltpu.VMEM((tt, d), F32), pltpu.VMEM((tt, d), F32),
                        pltpu.VMEM((tt, d), F32), pltpu.VMEM((1, d), F32)],
        compiler_params=_params("arbitrary"), name="branch_a_fwd")(z, z, z, wa, ba, wr, br, wi, bi, lam)


def _row_blocks(n_rows, body):
    n = n_rows // ROW_BLOCK
    unroll = 3 if n % 3 == 0 else 1

    def step(b, carry):
        for u in range(unroll):
            body(pl.multiple_of((b * unroll + u) * ROW_BLOCK, ROW_BLOCK))
        return carry
    lax.fori_loop(0, n // unroll, step, 0)


def _rows(r0, offset=0):
    return pl.ds(pl.multiple_of(r0 + offset, ROW_BLOCK), ROW_BLOCK)


def _glu_ext(gext, vb_ref, gb_ref, vbh_ref, gbh_ref, first, tt):
    def halo(r0):
        rows = _rows(r0)
        gext[rows, :] = jnp.where(first, 0.0, vbh_ref[rows, :] * _sigmoid(gbh_ref[rows, :]))
    _row_blocks(HALO_B, halo)

    def tile(r0):
        rows = _rows(r0)
        gext[_rows(r0, HALO_B), :] = vb_ref[rows, :] * _sigmoid(gb_ref[rows, :])
    _row_blocks(tt, tile)


def _shifted(src, dst, cs, n_rows):
    for b in range(1, 8):
        dst[b - 1, 0:n_rows, :] = src[pl.ds(b, n_rows), cs]


def _tap(src, shifted, cs, off, r0, n):
    b = off % 8
    if b == 0:
        return src[pl.ds(off + r0, n), cs]
    return shifted[b - 1, pl.ds(off - b + r0, n), :]


def _conv_rows(tt):
    return _pick(tt, CONV_ROWS, 8)


def _conv_b_fwd(gext, gs, wb_ref, bb_ref, cb_s, tt, d):
    rc = _conv_rows(tt)
    for c in range(d // LANE_CHUNK):
        cs = slice(c * LANE_CHUNK, (c + 1) * LANE_CHUNK)
        _shifted(gext, gs, cs, tt + HALO_B - 8)
        for r0 in range(0, tt, rc):
            acc = jnp.broadcast_to(bb_ref[:, cs], (rc, LANE_CHUNK))
            for k in range(CONV_B):
                acc = acc + wb_ref[k:k + 1, cs] * _tap(gext, gs, cs, HALO_B - (CONV_B - 1) + k, r0, rc)
            cb_s[r0:r0 + rc, cs] = acc


def _layer_norm_stats(cb):
    mu = jnp.mean(cb, axis=-1, keepdims=True)
    xc = cb - mu
    rstd = lax.rsqrt(jnp.mean(xc * xc, axis=-1, keepdims=True) + EPS)
    return xc * rstd, rstd


def _branch_b_fwd(z, wb, bb, lg, lb, rider=None):
    t = z.shape[0]
    d = wb.shape[1]
    tt = _pick(t, ROW_TILE_EW, HALO_B)
    hb = tt // HALO_B

    def body(vb_ref, gb_ref, vbh_ref, gbh_ref, sb_ref, wb_ref, bb_ref, lg_ref, lb_ref, yb_ref, cb_s, gext, gs):
        _glu_ext(gext, vb_ref, gb_ref, vbh_ref, gbh_ref, pl.program_id(0) == 0, tt)
        _conv_b_fwd(gext, gs, wb_ref, bb_ref, cb_s, tt, d)

        def finish(r0):
            rows = _rows(r0)
            xhat, _ = _layer_norm_stats(cb_s[rows, :])
            ln = xhat * lg_ref[...] + lb_ref[...]
            sb = sb_ref[rows, :]
            yb_ref[rows, :] = (ln * _sigmoid(ln) * (sb * _sigmoid(sb))).astype(BF16)
        _row_blocks(tt, finish)

    vec = _whole((1, d))
    tile = lambda s: pl.BlockSpec((tt, d), lambda i: (i, s))
    halo = lambda s: pl.BlockSpec((HALO_B, d), lambda i: (jnp.maximum(i * hb - 1, 0), s))
    return _call(
        body, rider, (z, z, z, z, z, wb, bb, lg, lb), grid=(t // tt,),
        in_specs=[tile(2), tile(3), halo(2), halo(3), tile(4), _whole((CONV_B, d)), vec, vec, vec],
        out_specs=[pl.BlockSpec((tt, d), lambda i: (i, 0))] * 2,
        out_shape=[jax.ShapeDtypeStruct((t, d), BF16), jax.ShapeDtypeStruct((t, d), F32)],
        scratch_shapes=[pltpu.VMEM((tt + HALO_B, d), F32), pltpu.VMEM((7, tt + HALO_B - 8, LANE_CHUNK), F32)],
        semantics=("parallel",), name="branch_b_fwd")


def _merge_fwd(ya, yb, z, wpa, wpb):
    t, d = ya.shape
    tm = _pick(t, ROW_TILE_MM, 16)
    tn = _pick(d, COL_TILE, 128)
    nb = d // tn

    def body(ya_ref, yb_ref, ma_ref, mb_ref, wpa_ref, wpb_ref, mg_ref, pa_ref, pb_ref):
        pa = jnp.dot(ya_ref[...], wpa_ref[...], preferred_element_type=F32)
        pb = jnp.dot(yb_ref[...], wpb_ref[...], preferred_element_type=F32)
        mg_ref[...] = (_sigmoid(ma_ref[...]) * pa + _sigmoid(mb_ref[...]) * pb).astype(BF16)
        pa_ref[...] = pa.astype(BF16)
        pb_ref[...] = pb.astype(BF16)

    rows = pl.BlockSpec((tm, d), lambda i, j: (i, 0))
    wcol = pl.BlockSpec((d, tn), lambda i, j: (0, j))
    outb = pl.BlockSpec((tm, tn), lambda i, j: (i, j))
    return pl.pallas_call(
        body, grid=(t // tm, nb),
        in_specs=[rows, rows, pl.BlockSpec((tm, tn), lambda i, j: (i, 5 * nb + j)),
                  pl.BlockSpec((tm, tn), lambda i, j: (i, 6 * nb + j)), wcol, wcol],
        out_specs=[outb, outb, outb],
        out_shape=[jax.ShapeDtypeStruct((t, d), BF16)] * 3,
        compiler_params=_params("parallel", "arbitrary"), name="merge_fwd")(ya, yb, z, z, wpa, wpb)


def _out_fwd(h, merged, wout):
    t, d = h.shape
    tm = _pick(t, ROW_TILE_MM, 16)
    tn = _pick(d, COL_TILE_WIDE, 128)

    def body(h_ref, mg_ref, w_ref, o_ref):
        o_ref[...] = h_ref[...] + jnp.dot(mg_ref[...], w_ref[...], preferred_element_type=F32)

    return pl.pallas_call(
        body, grid=(t // tm, d // tn),
        in_specs=[pl.BlockSpec((tm, tn), lambda i, j: (i, j)), pl.BlockSpec((tm, d), lambda i, j: (i, 0)),
                  pl.BlockSpec((d, tn), lambda i, j: (0, j))],
        out_specs=pl.BlockSpec((tm, tn), lambda i, j: (i, j)),
        out_shape=jax.ShapeDtypeStruct((t, d), F32),
        compiler_params=_params("parallel", "arbitrary"), name="out_fwd")(h, merged, wout)


def _loss_head(h, g, target):
    t, d = h.shape
    tt = _pick(t, ROW_TILE_EW, 16)

    def body(h_ref, g_ref, tg_ref, loss_ref, dh_ref, dg_ref):
        i = pl.program_id(0)

        @pl.when(i == 0)
        def _():
            loss_ref[...] = jnp.zeros_like(loss_ref)
            dg_ref[...] = jnp.zeros_like(dg_ref)
        v = h_ref[...]
        g = g_ref[...]
        rinv = lax.rsqrt(jnp.mean(v * v, axis=-1, keepdims=True) + EPS)
        xh = v * rinv
        valid = (i * tt + _row_iota(v.shape)) >= N_META
        diff = jnp.where(valid, xh * g - tg_ref[...], 0.0)
        loss_ref[...] += (0.5 / d) * jnp.sum(diff * diff, axis=0, keepdims=True)
        dy = diff * (1.0 / d)
        dg_ref[...] += jnp.sum(dy * xh, axis=0, keepdims=True)
        dxh = dy * g
        dh_ref[...] = rinv * (dxh - xh * jnp.mean(dxh * xh, axis=-1, keepdims=True))

    tile = pl.BlockSpec((tt, d), lambda i: (i, 0))
    vec = _whole((1, d))
    return pl.pallas_call(
        body, grid=(t // tt,), in_specs=[tile, vec, tile], out_specs=[vec, tile, vec],
        out_shape=[jax.ShapeDtypeStruct((1, d), F32), jax.ShapeDtypeStruct((t, d), F32),
                   jax.ShapeDtypeStruct((1, d), F32)],
        compiler_params=_params("arbitrary"), name="loss_head")(h, g, target)


def _dmerged_bwd(dho, wout, z, pa, pb):
    t, d = dho.shape
    tm = _pick(t, ROW_TILE_MM, 16)
    tn = _pick(d, COL_TILE, 128)
    nb = d // tn

    def body(dho_ref, w_ref, ma_ref, mb_ref, pa_ref, pb_ref, dpa_ref, dpb_ref, dma_ref, dmb_ref, dho_s):
        @pl.when(pl.program_id(1) == 0)
        def _():
            dho_s[...] = dho_ref[...].astype(BF16)
        dm = lax.dot_general(dho_s[...], w_ref[...], NT_DIMS, preferred_element_type=F32)
        sa = _sigmoid(ma_ref[...])
        sb = _sigmoid(mb_ref[...])
        dpa_ref[...] = (dm * sa).astype(BF16)
        dpb_ref[...] = (dm * sb).astype(BF16)
        dma_ref[...] = (dm * pa_ref[...].astype(F32) * (sa * (1.0 - sa))).astype(BF16)
        dmb_ref[...] = (dm * pb_ref[...].astype(F32) * (sb * (1.0 - sb))).astype(BF16)

    blk = pl.BlockSpec((tm, tn), lambda i, j: (i, j))
    return pl.pallas_call(
        body, grid=(t // tm, nb),
        in_specs=[pl.BlockSpec((tm, d), lambda i, j: (i, 0)), pl.BlockSpec((tn, d), lambda i, j: (j, 0)),
                  pl.BlockSpec((tm, tn), lambda i, j: (i, 5 * nb + j)),
                  pl.BlockSpec((tm, tn), lambda i, j: (i, 6 * nb + j)), blk, blk],
        out_specs=[blk, blk, blk, blk],
        out_shape=[jax.ShapeDtypeStruct((t, d), BF16)] * 4,
        scratch_shapes=[pltpu.VMEM((tm, d), BF16)],
        compiler_params=_params("parallel", "arbitrary"), name="dmerged_bwd")(dho, wout, z, z, pa, pb)


def _matmul_nt(a, w, name):
    t, k = a.shape
    n = w.shape[0]
    tm = _pick(t, ROW_TILE_MM, 16)
    tn = _pick(n, COL_TILE_WIDE, 128)

    def body(a_ref, w_ref, o_ref):
        o_ref[...] = lax.dot_general(a_ref[...], w_ref[...], NT_DIMS, preferred_element_type=F32)

    return pl.pallas_call(
        body, grid=(t // tm, n // tn),
        in_specs=[pl.BlockSpec((tm, k), lambda i, j: (i, 0)), pl.BlockSpec((tn, k), lambda i, j: (j, 0))],
        out_specs=pl.BlockSpec((tm, tn), lambda i, j: (i, j)),
        out_shape=jax.ShapeDtypeStruct((t, n), F32),
        compiler_params=_params("parallel", "arbitrary"), name=name)(a, w)


def _matmul_tn(a, b, name):
    t, m = a.shape
    n = b.shape[1]
    tk = _pick(t, ROW_TILE_MM, 16)
    tn = _pick(n, COL_TILE_WIDE, 128)

    def body(a_ref, b_ref, o_ref):
        @pl.when(pl.program_id(1) == 0)
        def _():
            o_ref[...] = jnp.zeros_like(o_ref)
        o_ref[...] += lax.dot_general(a_ref[...].astype(BF16), b_ref[...].astype(BF16), TN_DIMS,
                                      preferred_element_type=F32)

    return pl.pallas_call(
        body, grid=(n // tn, t // tk),
        in_specs=[pl.BlockSpec((tk, m), lambda j, s: (s, 0)), pl.BlockSpec((tk, tn), lambda j, s: (s, j))],
        out_specs=pl.BlockSpec((m, tn), lambda j, s: (0, j)),
        out_shape=jax.ShapeDtypeStruct((m, n), F32),
        compiler_params=_params("parallel", "arbitrary"), name=name)(a, b)


def _dwin_bwd(hn, dzs, rider=None):
    t, d = hn.shape
    nq = N_SPLIT * d // N_CHIPS
    tk = _pick(t, ROW_TILE_MM, 16)
    tn = _pick(nq, COL_TILE, 128)
    bps = nq // tn
    bpz = d // tn

    def body(hn_ref, *rest):
        dz_refs, o_ref = rest[:N_SPLIT], rest[N_SPLIT]
        j = pl.program_id(0)

        @pl.when(pl.program_id(1) == 0)
        def _():
            o_ref[...] = jnp.zeros_like(o_ref)
        for s in range(N_SPLIT):
            @pl.when(j // bpz == s)
            def _(s=s):
                o_ref[...] += lax.dot_general(hn_ref[...], dz_refs[s][...], TN_DIMS, preferred_element_type=F32)

    def dz_spec(s):
        def imap(j, r):
            on = (j // bpz) == s
            return (jnp.where(on, r, 0), jnp.where(on, j % bpz, 0))
        return pl.BlockSpec((tk, tn), imap)

    return _call(
        body, rider, (hn, *dzs), grid=(N_SPLIT * bpz, t // tk),
        in_specs=[pl.BlockSpec((tk, d), lambda j, r: (r, 0))] + [dz_spec(s) for s in range(N_SPLIT)],
        out_specs=[pl.BlockSpec((None, d, tn), lambda j, r: (j // bps, 0, j % bps))],
        out_shape=[jax.ShapeDtypeStruct((N_CHIPS, d, nq), F32)],
        scratch_shapes=[], semantics=("parallel", "arbitrary"), name="dwin_bwd")


def _din_bwd(dzs, w, h, g, dho, rider=None):
    t, d = h.shape
    nq = w.shape[2]
    tm = _pick(t, ROW_TILE_MM, 16)
    tk = _pick(nq, COL_TILE, 128)
    bps = nq // tk
    bpz = d // tk
    nk = N_SPLIT * bpz

    def body(*refs):
        dz_refs = refs[:N_SPLIT]
        w_ref, h_ref, g_ref, dho_ref, dh_ref, dg_ref, dg8 = refs[N_SPLIT:]
        i, k = pl.program_id(0), pl.program_id(1)

        @pl.when(k == 0)
        def _():
            dh_ref[...] = jnp.zeros_like(dh_ref)

        @pl.when((i == 0) & (k == 0))
        def _():
            dg8[...] = jnp.zeros_like(dg8)
        for s in range(N_SPLIT):
            @pl.when(k // bpz == s)
            def _(s=s):
                dh_ref[...] += lax.dot_general(dz_refs[s][...], w_ref[...], NT_DIMS, preferred_element_type=F32)

        @pl.when(k == nk - 1)
        def _():
            def norm_bwd(r0):
                rows = _rows(r0)
                v = h_ref[rows, :]
                rinv = lax.rsqrt(jnp.mean(v * v, axis=-1, keepdims=True) + EPS)
                xh = v * rinv
                dhn = dh_ref[rows, :]
                part = dhn * xh
                dg8[...] += part[0:8, :] + part[8:16, :]
                dxh = dhn * g_ref[...]
                dh_ref[rows, :] = dho_ref[rows, :] + rinv * (dxh - xh * jnp.mean(dxh * xh, axis=-1, keepdims=True))
            _row_blocks(tm, norm_bwd)
            dg_ref[...] = jnp.sum(dg8[...], axis=0, keepdims=True)

    def dz_spec(s):
        return pl.BlockSpec((tm, tk), lambda i, k: (i, jnp.clip(k - s * bpz, 0, bpz - 1)))

    rows = pl.BlockSpec((tm, d), lambda i, k: (i, 0))
    once = pl.BlockSpec((tm, d), lambda i, k: (i, 0), pipeline_mode=pl.Buffered(1))
    vec = _whole((1, d))
    return _call(
        body, rider, (*dzs, w, h, g, dho), grid=(t // tm, nk),
        in_specs=[dz_spec(s) for s in range(N_SPLIT)]
        + [pl.BlockSpec((None, d, tk), lambda i, k: (k // bps, 0, k % bps)), once, vec, once],
        out_specs=[rows, vec],
        out_shape=[jax.ShapeDtypeStruct((t, d), F32), jax.ShapeDtypeStruct((1, d), F32)],
        scratch_shapes=[pltpu.VMEM((8, d), F32)], semantics=("arbitrary", "arbitrary"), name="din_bwd")


def _branch_b_bwd(z, cb, dyb, wb, lg, lb):
    t = z.shape[0]
    d = wb.shape[1]
    tt = _pick(t, ROW_TILE_EW, HALO_B)
    hb = tt // HALO_B
    nt = t // tt
    rc = _conv_rows(tt)

    def body(vb_ref, gb_ref, vbh_ref, gbh_ref, sb_ref, cb_ref, dyb_ref, wb_ref, lg_ref, lb_ref,
             dvb_ref, dgb_ref, dsb_ref, dwb_ref, dbb_ref, dlg_ref, dlb_ref,
             gext, dext, gs, ds, cb_s, carry, dwb8, vec8):
        i = pl.program_id(0)

        @pl.when(i == 0)
        def _():
            for ref in (carry, dwb8, vec8):
                ref[...] = jnp.zeros_like(ref)
        _glu_ext(gext, vb_ref, gb_ref, vbh_ref, gbh_ref, i == nt - 1, tt)

        def norm_bwd(r0):
            rows = _rows(r0)
            xhat, rstd = _layer_norm_stats(cb_ref[rows, :])
            lg = lg_ref[...]
            ln = xhat * lg + lb_ref[...]
            sl = _sigmoid(ln)
            sb = sb_ref[rows, :]
            ss = _sigmoid(sb)
            dyb = dyb_ref[rows, :]
            dln = dyb * (sb * ss) * _dsilu(ln, sl)
            dsb_ref[rows, :] = (dyb * (ln * sl) * _dsilu(sb, ss)).astype(BF16)
            dxh = dln * lg
            dcb = rstd * (dxh - jnp.mean(dxh, axis=-1, keepdims=True)
                          - xhat * jnp.mean(dxh * xhat, axis=-1, keepdims=True))
            dext[rows, :] = dcb
            for n, val in enumerate((dln * xhat, dln, dcb)):
                vec8[n] += val[0:8, :] + val[8:16, :]
        _row_blocks(tt, norm_bwd)
        dext[tt:, :] = carry[...]
        carry[...] = dext[0:HALO_B, :]
        for c in range(d // LANE_CHUNK):
            cs = slice(c * LANE_CHUNK, (c + 1) * LANE_CHUNK)
            _shifted(gext, gs, cs, tt + HALO_B - 8)
            _shifted(dext, ds, cs, tt + HALO_B - 8)
            for r0 in range(0, tt, rc):
                dcb = dext[r0:r0 + rc, cs]
                acc = jnp.zeros((rc, LANE_CHUNK), F32)
                for k in range(CONV_B):
                    acc = acc + wb_ref[k:k + 1, cs] * _tap(dext, ds, cs, CONV_B - 1 - k, r0, rc)
                    prod = dcb * _tap(gext, gs, cs, HALO_B - (CONV_B - 1) + k, r0, rc)
                    part = prod[0:8, :]
                    for j in range(8, rc, 8):
                        part = part + prod[j:j + 8, :]
                    dwb8[k, :, cs] += part
                cb_s[r0:r0 + rc, cs] = acc

        def glu_bwd(r0):
            rows = _rows(r0)
            dglu = cb_s[rows, :]
            sg = _sigmoid(gb_ref[rows, :])
            dvb_ref[rows, :] = (dglu * sg).astype(BF16)
            dgb_ref[rows, :] = (dglu * vb_ref[rows, :] * (sg * (1.0 - sg))).astype(BF16)
        _row_blocks(tt, glu_bwd)

        @pl.when(i == nt - 1)
        def _():
            dwb_ref[...] = jnp.sum(dwb8[...], axis=1)
            dlg_ref[...] = jnp.sum(vec8[0], axis=0, keepdims=True)
            dlb_ref[...] = jnp.sum(vec8[1], axis=0, keepdims=True)
            dbb_ref[...] = jnp.sum(vec8[2], axis=0, keepdims=True)

    rev = lambda i: nt - 1 - i
    vec = _whole((1, d))
    tile = lambda s: pl.BlockSpec((tt, d), lambda i: (rev(i), s))
    halo = lambda s: pl.BlockSpec((HALO_B, d), lambda i: (jnp.maximum(rev(i) * hb - 1, 0), s))
    otile = pl.BlockSpec((tt, d), lambda i: (rev(i), 0))
    ext = pltpu.VMEM((tt + HALO_B, d), F32)
    shifts = pltpu.VMEM((7, tt + HALO_B - 8, LANE_CHUNK), F32)
    return pl.pallas_call(
        body, grid=(nt,),
        in_specs=[tile(2), tile(3), halo(2), halo(3), tile(4), otile, otile, _whole((CONV_B, d)), vec, vec],
        out_specs=[otile, otile, otile, _whole((CONV_B, d)), vec, vec, vec],
        out_shape=[jax.ShapeDtypeStruct((t, d), BF16)] * 3
        + [jax.ShapeDtypeStruct((CONV_B, d), F32)] + [jax.ShapeDtypeStruct((1, d), F32)] * 3,
        scratch_shapes=[ext, ext, shifts, shifts, pltpu.VMEM((tt, d), F32), pltpu.VMEM((HALO_B, d), F32),
                        pltpu.VMEM((CONV_B, 8, d), F32), pltpu.VMEM((3, 8, d), F32)],
        compiler_params=_params("arbitrary"), name="branch_b_bwd")(z, z, z, z, z, cb, dyb, wb, lg, lb)


def _branch_a_bwd(z, hl, dya, wa, ba, wr, br, wi, bi, lam):
    t = z.shape[0]
    d = wa.shape[1]
    tt = _pick(t, ROW_TILE_EW, HALO_B)
    hb = tt // HALO_A
    nt = t // tt
    nh = d // HEAD_DIM

    def body(xa_ref, xah_ref, sa_ref, hl_ref, hlh_ref, dya_ref, wa_ref, ba_ref, wr_ref, br_ref, wi_ref, bi_ref,
             lam_ref, dxa_ref, dsa_ref, dwa_ref, dba_ref, dbr_ref, dbi_ref, dlam_ref, dwr_ref, dwi_ref,
             xext, hext, dext, gr_s, gi_s, a_s, lam_s, gcarry, dcarry):
        i = pl.program_id(0)
        first = i == nt - 1

        @pl.when(i == 0)
        def _():
            for ref in (dwa_ref, dba_ref, dbr_ref, dbi_ref, dlam_ref, dwr_ref, dwi_ref, gcarry, dcarry):
                ref[...] = jnp.zeros_like(ref)
        xext[0:HALO_A, :] = jnp.where(first, 0.0, xah_ref[...])
        xext[HALO_A:, :] = xa_ref[...]
        hext[0:HALO_A, :] = jnp.where(first, 0.0, hlh_ref[...])
        hext[HALO_A:, :] = hl_ref[...]
        ca = _conv_a(xext, wa_ref, ba_ref[...], tt)
        r, ig = _gates_a(ca, wr_ref, wi_ref, br_ref[...], bi_ref[...], gr_s, gi_s)
        lam = lam_ref[...]
        sp = _softplus(-lam)
        a, mult = _decay_a(r, sp)
        sa = sa_ref[...]
        ss = _sigmoid(sa)
        dya = dya_ref[...]
        dsa_ref[...] = (dya * hl_ref[...] * _dsilu(sa, ss)).astype(BF16)
        a_s[...] = a
        lam_s[...] = dya * (sa * ss)
        gcarry[...] = _scan_rev(a_s, lam_s, lam_s, gcarry[...], tt // 8)
        du = lam_s[...]
        da = du * hext[pl.ds(HALO_A - 1, tt), :]
        dig = du * mult * ca
        dca = du * mult * ig
        dlog_a = da * a - (du * ig * ca) * jnp.where(mult > 0.0, a * a / mult, 0.0)
        dgr = (dlog_a * (-LRU_C * sp)) * r * (1.0 - r)
        dgi = dig * ig * (1.0 - ig)
        dlam_ref[...] += jnp.sum(dlog_a * r, axis=0, keepdims=True) * (LRU_C * _sigmoid(-lam))
        dbr_ref[...] += jnp.sum(dgr, axis=0, keepdims=True)
        dbi_ref[...] += jnp.sum(dgi, axis=0, keepdims=True)
        cab = ca.astype(BF16)
        dgrb = dgr.astype(BF16)
        dgib = dgi.astype(BF16)
        for hh in range(nh):
            sl = slice(hh * HEAD_DIM, (hh + 1) * HEAD_DIM)
            dwr_ref[hh] += lax.dot_general(cab[:, sl], dgrb[:, sl], TN_DIMS, preferred_element_type=F32)
            dwi_ref[hh] += lax.dot_general(cab[:, sl], dgib[:, sl], TN_DIMS, preferred_element_type=F32)
            gr_s[:, sl] = (lax.dot_general(dgrb[:, sl], wr_ref[hh], NT_DIMS, preferred_element_type=F32)
                           + lax.dot_general(dgib[:, sl], wi_ref[hh], NT_DIMS, preferred_element_type=F32))
        dca = dca + gr_s[...]
        dba_ref[...] += jnp.sum(dca, axis=0, keepdims=True)
        dext[0:tt, :] = dca
        dext[tt:, :] = dcarry[...]
        dcarry[...] = dca[0:HALO_A, :]
        dxa = jnp.zeros_like(dca)
        for k in range(CONV_A):
            off = HALO_A - (CONV_A - 1) + k
            dwa_ref[k:k + 1, :] += jnp.sum(dext[0:tt, :] * xext[pl.ds(off, tt), :], axis=0, keepdims=True)
            dxa = dxa + wa_ref[k:k + 1, :] * dext[pl.ds(CONV_A - 1 - k, tt), :]
        dxa_ref[...] = dxa.astype(BF16)

    rev = lambda i: nt - 1 - i
    vec = _whole((1, d))
    hw = _whole((nh, HEAD_DIM, HEAD_DIM))
    tile = lambda s: pl.BlockSpec((tt, d), lambda i: (rev(i), s))
    halo = pl.BlockSpec((HALO_A, d), lambda i: (jnp.maximum(rev(i) * hb - 1, 0), 0))
    big = pltpu.VMEM((tt + HALO_A, d), F32)
    full = pltpu.VMEM((tt, d), F32)
    return pl.pallas_call(
        body, grid=(nt,),
        in_specs=[tile(0), halo, tile(1), tile(0), halo, tile(0), _whole((CONV_A, d)), vec, hw, vec, hw, vec, vec],
        out_specs=[tile(0), tile(0), _whole((CONV_A, d)), vec, vec, vec, vec, hw, hw],
        out_shape=[jax.ShapeDtypeStruct((t, d), BF16)] * 2 + [jax.ShapeDtypeStruct((CONV_A, d), F32)]
        + [jax.ShapeDtypeStruct((1, d), F32)] * 4 + [jax.ShapeDtypeStruct((nh, HEAD_DIM, HEAD_DIM), F32)] * 2,
        scratch_shapes=[big, big, big, full, full, full, full, pltpu.VMEM((1, d), F32), pltpu.VMEM((HALO_A, d), F32)],
        compiler_params=_params("arbitrary"), name="branch_a_bwd")(z, z, z, hl, hl, dya, wa, ba, wr, br, wi, bi, lam)


def _rows_block(rows, cols, n_arrays=1):
    cap = max(8, BLOCK_BYTES // (4 * cols * n_arrays))
    return rows if rows <= cap else _pick(rows, cap, 8)


def _sum_halves(g, xo, c_idx, dtype):
    nq, r, c = g.shape
    rh = r // 2
    tr = _rows_block(rh, c)
    nb = rh // tr

    def body(c_ref, g_ref, x_ref, o_ref):
        o_ref[...] = (g_ref[...] + x_ref[...]).astype(dtype)

    grid_spec = pltpu.PrefetchScalarGridSpec(
        num_scalar_prefetch=1, grid=(nq, nb),
        in_specs=[pl.BlockSpec((None, tr, c), lambda q, i, cr: (q, cr[0] * nb + i, 0)),
                  pl.BlockSpec((None, tr, c), lambda q, i, cr: (q, i, 0))],
        out_specs=pl.BlockSpec((None, tr, c), lambda q, i, cr: (q, i, 0)))
    return pl.pallas_call(
        body, grid_spec=grid_spec, out_shape=jax.ShapeDtypeStruct((nq, rh, c), dtype),
        compiler_params=_params("parallel", "parallel"), name="sum_halves")(c_idx, g, xo)


def _sum_slots(y, name):
    ns, r, c = y.shape
    tr = _rows_block(r, c, ns)

    def body(y_ref, o_ref):
        acc = y_ref[0].astype(F32)
        for s in range(1, ns):
            acc = acc + y_ref[s].astype(F32)
        o_ref[...] = acc

    return pl.pallas_call(
        body, grid=(r // tr,), in_specs=[pl.BlockSpec((ns, tr, c), lambda i: (0, i, 0))],
        out_specs=pl.BlockSpec((tr, c), lambda i: (i, 0)), out_shape=jax.ShapeDtypeStruct((r, c), F32),
        compiler_params=_params("parallel"), name=name)(y)


def _adamw(w, g, m, v, name):
    shape = w.shape
    c = shape[-1]
    r = w.size // c
    tr = _rows_block(r, c)
    c1 = 1.0 / (1.0 - ADAM_B1 ** ADAM_STEP)
    c2 = 1.0 / (1.0 - ADAM_B2 ** ADAM_STEP)

    def body(w_ref, g_ref, m_ref, v_ref, go_ref, d_ref, mo_ref, vo_ref):
        gv = g_ref[...]
        mn = ADAM_B1 * m_ref[...] + (1.0 - ADAM_B1) * gv
        vn = ADAM_B2 * v_ref[...] + (1.0 - ADAM_B2) * (gv * gv)
        go_ref[...] = gv
        mo_ref[...] = mn
        vo_ref[...] = vn
        d_ref[...] = -ADAM_LR * ((mn * c1) / (jnp.sqrt(vn * c2) + ADAM_EPS) + ADAM_WD * w_ref[...])

    blk = pl.BlockSpec((tr, c), lambda i: (i, 0))
    outs = pl.pallas_call(
        body, grid=(r // tr,), in_specs=[blk] * 4, out_specs=[blk] * 4,
        out_shape=[jax.ShapeDtypeStruct((r, c), F32)] * 4,
        compiler_params=_params("parallel"), name=name)(*(a.reshape(r, c) for a in (w, g, m, v)))
    return tuple(o.reshape(shape) for o in outs)


def _adamw_layers(w, mine, theirs, m, v, c_idx, name):
    depth, r, c = w.shape
    tr = _rows_block(r // 2, c, 2)
    nbh = r // 2 // tr
    c1 = 1.0 / (1.0 - ADAM_B1 ** ADAM_STEP)
    c2 = 1.0 / (1.0 - ADAM_B2 ** ADAM_STEP)

    def body(c_ref, w_ref, m_ref, v_ref, *rest):
        g_refs = rest[:2 * depth]
        go_ref, d_ref, mo_ref, vo_ref = rest[2 * depth:]
        layer = pl.program_id(0)
        own = (pl.program_id(1) // nbh) == c_ref[0]

        def update(g_ref):
            gv = g_ref[...]
            mn = ADAM_B1 * m_ref[...] + (1.0 - ADAM_B1) * gv
            vn = ADAM_B2 * v_ref[...] + (1.0 - ADAM_B2) * (gv * gv)
            go_ref[...] = gv
            mo_ref[...] = mn
            vo_ref[...] = vn
            d_ref[...] = -ADAM_LR * ((mn * c1) / (jnp.sqrt(vn * c2) + ADAM_EPS) + ADAM_WD * w_ref[...])

        for s in range(depth):
            @pl.when((layer == s) & own)
            def _(s=s):
                update(g_refs[2 * s])

            @pl.when((layer == s) & jnp.logical_not(own))
            def _(s=s):
                update(g_refs[2 * s + 1])

    stacked = pl.BlockSpec((None, tr, c), lambda l, i, cr: (l, i, 0))

    def g_spec(s, is_mine):
        def imap(l, i, cr):
            own = (i // nbh) == cr[0]
            on = (l == s) & (own if is_mine else jnp.logical_not(own))
            return (jnp.where(on, i % nbh, 0), 0)
        return pl.BlockSpec((tr, c), imap)

    g_specs = [g_spec(s, is_mine) for s in range(depth) for is_mine in (True, False)]
    gs = [a for s in range(depth) for a in (mine[s], theirs[s])]
    grid_spec = pltpu.PrefetchScalarGridSpec(
        num_scalar_prefetch=1, grid=(depth, 2 * nbh), in_specs=[stacked] * 3 + g_specs, out_specs=[stacked] * 4)
    return pl.pallas_call(
        body, grid_spec=grid_spec, out_shape=[jax.ShapeDtypeStruct(w.shape, F32)] * 4,
        compiler_params=_params("arbitrary", "arbitrary"), name=name)(c_idx, w, m, v, *gs)


HBM_SPEC = pl.BlockSpec(memory_space=pltpu.HBM)


def _place():
    return lax.axis_index("x"), lax.axis_index("y"), lax.axis_index("c")


def _other_chips(x, y):
    return [(1 - x, y), (x, 1 - y), (1 - x, 1 - y)]


def _half(ref, which):
    rows = ref.shape[0] // 2
    return ref.at[pl.ds(which * rows, rows)]


def _gather_chips(arrays, name):
    partial = _run_rider(_gather_ici_rider(arrays), name + "_ici")
    return _run_rider(_gather_d2d_rider(partial), name + "_d2d")


class _Rider:
    def __init__(self, operands, out_shape, scratch, start, finish, aliases=None):
        self.operands, self.out_shape, self.scratch = list(operands), list(out_shape), list(scratch)
        self.start, self.finish, self.aliases = start, finish, dict(aliases or {})


def _run_rider(rider, name):
    n_in, n_out = len(rider.operands), len(rider.out_shape)

    def body(*refs):
        ins, outs, sems = refs[:n_in], refs[n_in:n_in + n_out], refs[n_in + n_out:]
        rider.start(ins, outs, sems)
        rider.finish(ins, outs, sems)

    return pl.pallas_call(
        body, in_specs=[HBM_SPEC] * n_in, out_specs=[HBM_SPEC] * n_out, out_shape=rider.out_shape,
        scratch_shapes=rider.scratch, input_output_aliases=rider.aliases, name=name)(*rider.operands)


def _hosted(body, rider, *, grid, in_specs, out_specs, out_shape, scratch_shapes, name):
    n_in, n_out, n_scr = len(in_specs), len(out_specs), len(scratch_shapes)
    r_in, r_out = len(rider.operands), len(rider.out_shape)

    def wrapped(*refs):
        ins, refs = refs[:n_in], refs[n_in:]
        r_ins, refs = refs[:r_in], refs[r_in:]
        outs, refs = refs[:n_out], refs[n_out:]
        r_outs, refs = refs[:r_out], refs[r_out:]
        scr, sems = refs[:n_scr], refs[n_scr:]
        first = functools.reduce(jnp.logical_and, [pl.program_id(a) == 0 for a in range(len(grid))])
        last = functools.reduce(jnp.logical_and, [pl.program_id(a) == g - 1 for a, g in enumerate(grid)])

        @pl.when(first)
        def _():
            rider.start(r_ins, r_outs, sems)
        body(*ins, *outs, *scr)

        @pl.when(last)
        def _():
            rider.finish(r_ins, r_outs, sems)

    res = pl.pallas_call(
        wrapped, grid=grid, in_specs=list(in_specs) + [HBM_SPEC] * r_in, out_specs=list(out_specs) + [HBM_SPEC] * r_out,
        out_shape=list(out_shape) + rider.out_shape, scratch_shapes=list(scratch_shapes) + rider.scratch,
        input_output_aliases={n_in + i: n_out + o for i, o in rider.aliases.items()},
        compiler_params=_params(*(["arbitrary"] * len(grid))), name=name)
    return lambda *operands: (lambda r: (r[:n_out], r[n_out:]))(res(*operands, *rider.operands))


def _gather_ici_rider(arrays):
    n = len(arrays)

    def copies(ins, outs, sems):
        send, recv = sems
        x, y, c = _place()
        me = 2 * x + y
        chips = _other_chips(x, y)
        sends, arrivals = [], []
        for a in range(n):
            for group in (sends, arrivals):
                group.append(pltpu.make_async_remote_copy(
                    src_ref=ins[a], dst_ref=outs[a].at[me], send_sem=send.at[a, 3], recv_sem=recv.at[a, 3],
                    device_id=(x, y, 1 - c), device_id_type=MESH))
            for j in range(3):
                q = 2 * chips[j][0] + chips[j][1]
                for dst, group in ((_half(outs[a].at[me], c), sends), (_half(outs[a].at[q], c), arrivals)):
                    group.append(pltpu.make_async_remote_copy(
                        src_ref=_half(ins[a], c), dst_ref=dst, send_sem=send.at[a, j], recv_sem=recv.at[a, j],
                        device_id=(*chips[j], c), device_id_type=MESH))
        return sends, arrivals

    def start(ins, outs, sems):
        for cp in copies(ins, outs, sems)[0]:
            cp.start()

    def finish(ins, outs, sems):
        sends, arrivals = copies(ins, outs, sems)
        for cp in arrivals:
            cp.wait_recv()
        for cp in sends:
            cp.wait_send()

    sems = pltpu.SemaphoreType.DMA((n, 4))
    return _Rider(arrays, [jax.ShapeDtypeStruct((N_CHIPS,) + a.shape, a.dtype) for a in arrays],
                  [sems, sems], start, finish)


def _gather_d2d_rider(partial):
    n = len(partial)

    def copies(outs, sems):
        send, recv = sems
        x, y, c = _place()
        chips = _other_chips(x, y)
        sends, arrivals = [], []
        for a in range(n):
            for j in range(3):
                q = 2 * chips[j][0] + chips[j][1]
                for which, group in ((c, sends), (1 - c, arrivals)):
                    ref = _half(outs[a].at[q], which)
                    group.append(pltpu.make_async_remote_copy(
                        src_ref=ref, dst_ref=ref, send_sem=send.at[a, j], recv_sem=recv.at[a, j],
                        device_id=(x, y, 1 - c), device_id_type=MESH))
        return sends, arrivals

    def start(ins, outs, sems):
        for cp in copies(outs, sems)[0]:
            cp.start()

    def finish(ins, outs, sems):
        sends, arrivals = copies(outs, sems)
        for cp in arrivals:
            cp.wait_recv()
        for cp in sends:
            cp.wait_send()

    sems = pltpu.SemaphoreType.DMA((n, 3))
    return _Rider(partial, [jax.ShapeDtypeStruct(a.shape, a.dtype) for a in partial], [sems, sems], start, finish,
                  aliases={k: k for k in range(n)})


def _swap_halves_rider(grads):
    n = len(grads)

    def copies(ins, outs, sems):
        send, recv = sems
        x, y, c = _place()
        made = []
        for a in range(n):
            rows = ins[a].shape[1] // 2
            made.append(pltpu.make_async_remote_copy(
                src_ref=ins[a].at[:, pl.ds((1 - c) * rows, rows)], dst_ref=outs[a],
                send_sem=send.at[a], recv_sem=recv.at[a], device_id=(x, y, 1 - c), device_id_type=MESH))
        return made

    def start(ins, outs, sems):
        for cp in copies(ins, outs, sems):
            cp.start()

    def finish(ins, outs, sems):
        for cp in copies(ins, outs, sems):
            cp.wait()

    sems = pltpu.SemaphoreType.DMA((n,))
    return _Rider(grads, [jax.ShapeDtypeStruct((g.shape[0], g.shape[1] // 2) + g.shape[2:], g.dtype) for g in grads],
                  [sems, sems], start, finish)


def _scatter_rider(parts):
    n = len(parts)

    def copies(ins, outs, sems):
        send, recv, lsem = sems
        x, y, c = _place()
        me = 2 * x + y
        chips = _other_chips(x, y)
        local = [pltpu.make_async_copy(ins[a].at[me], outs[a].at[me], lsem.at[a]) for a in range(n)]
        sends, arrivals = [], []
        for a in range(n):
            for j in range(3):
                q = 2 * chips[j][0] + chips[j][1]
                for dst, group in ((outs[a].at[me], sends), (outs[a].at[q], arrivals)):
                    group.append(pltpu.make_async_remote_copy(
                        src_ref=ins[a].at[q], dst_ref=dst, send_sem=send.at[a, j], recv_sem=recv.at[a, j],
                        device_id=(*chips[j], c), device_id_type=MESH))
        return local, sends, arrivals

    def start(ins, outs, sems):
        local, sends, _ = copies(ins, outs, sems)
        for cp in local + sends:
            cp.start()

    def finish(ins, outs, sems):
        local, sends, arrivals = copies(ins, outs, sems)
        for cp in arrivals:
            cp.wait_recv()
        for cp in sends:
            cp.wait_send()
        for cp in local:
            cp.wait()

    sems = pltpu.SemaphoreType.DMA((n, 3))
    return _Rider(parts, [jax.ShapeDtypeStruct(p.shape, p.dtype) for p in parts],
                  [sems, sems, pltpu.SemaphoreType.DMA((n,))], start, finish)


def _swap_sibling_rider(arrays):
    n = len(arrays)

    def copies(ins, outs, sems):
        send, recv = sems
        x, y, c = _place()
        return [pltpu.make_async_remote_copy(
            src_ref=ins[a], dst_ref=outs[a], send_sem=send.at[a], recv_sem=recv.at[a],
            device_id=(x, y, 1 - c), device_id_type=MESH) for a in range(n)]

    def start(ins, outs, sems):
        for cp in copies(ins, outs, sems):
            cp.start()

    def finish(ins, outs, sems):
        for cp in copies(ins, outs, sems):
            cp.wait()

    sems = pltpu.SemaphoreType.DMA((n,))
    return _Rider(arrays, [jax.ShapeDtypeStruct(a.shape, a.dtype) for a in arrays], [sems, sems], start, finish)


def _join_riders(first, second):
    n_in, n_out, n_sem = len(first.operands), len(first.out_shape), len(first.scratch)

    def both(method):
        def run(ins, outs, sems):
            getattr(first, method)(ins[:n_in], outs[:n_out], sems[:n_sem])
            getattr(second, method)(ins[n_in:], outs[n_out:], sems[n_sem:])
        return run

    aliases = {**first.aliases, **{n_in + i: n_out + o for i, o in second.aliases.items()}}
    return _Rider(first.operands + second.operands, first.out_shape + second.out_shape,
                  first.scratch + second.scratch, both("start"), both("finish"), aliases)


def _layer_fwd(h, p, next_shards):
    ride = None if next_shards is None else _gather_ici_rider(next_shards)
    (z, hn), partial = _in_proj_fwd(h, p["norm_g"], p["w_in"], ride)
    ya, hl = _branch_a_fwd(z, p["conv_a_w"], p["conv_a_b"], p["w_rg"], p["b_rg"], p["w_ig"], p["b_ig"], p["lam"])
    ride = None if partial is None else _gather_d2d_rider(partial)
    (yb, cb), gathered = _branch_b_fwd(z, p["conv_b_w"], p["conv_b_b"], p["ln_b_g"], p["ln_b_b"], ride)
    merged, pa, pb = _merge_fwd(ya, yb, z, p["w_pa"], p["w_pb"])
    return _out_fwd(h, merged, p["w_out"]), (h, hn, z, hl, cb, ya, yb, pa, pb, merged), gathered


def _layer_bwd(dho, p, saved, chip_sums, c_idx):
    h, hn, z, hl, cb, ya, yb, pa, pb, merged = saved
    dpa, dpb, dma, dmb = _dmerged_bwd(dho, p["w_out"], z, pa, pb)
    g = {"w_out": _matmul_tn(merged, dho, "dw_out"), "w_proj_a": _matmul_tn(ya, dpa, "dw_proj_a"),
         "w_proj_b": _matmul_tn(yb, dpb, "dw_proj_b")}
    dya = _matmul_nt(dpa, p["w_pa"], "dya")
    dyb = _matmul_nt(dpb, p["w_pb"], "dyb")
    (dxa, dsa, g["conv_a_w"], g["conv_a_b"], g["b_rg"], g["b_ig"], g["lru_lambda"], g["w_rg"], g["w_ig"]) = \
        _branch_a_bwd(z, hl, dya, p["conv_a_w"], p["conv_a_b"], p["w_rg"], p["b_rg"], p["w_ig"], p["b_ig"], p["lam"])
    dvb, dgb, dsb, g["conv_b_w"], g["conv_b_b"], g["ln_b_g"], g["ln_b_b"] = \
        _branch_b_bwd(z, cb, dyb, p["conv_b_w"], p["ln_b_g"], p["ln_b_b"])
    dzs = (dxa, dsa, dvb, dgb, dsb, dma, dmb)
    (g["w_in"],), landed = _dwin_bwd(hn, dzs, None if chip_sums is None else _scatter_rider(chip_sums))
    d = h.shape[1]
    parts = [g["w_in"]] + [g[k].reshape(N_CHIPS, d // N_CHIPS, d) for k in BIG[1:]]
    rider = _swap_halves_rider(parts)
    mine = None if landed is None else [_sum_slots(y, "sum_chips") for y in landed]
    if mine is not None:
        rider = _join_riders(rider, _swap_sibling_rider(mine))
    (dh, g["norm_g"]), swapped = _din_bwd(dzs, p["w_in"], h, p["norm_g"], dho, rider)
    other, theirs = swapped[:len(parts)], swapped[len(parts):]
    above = None if mine is None else (mine, theirs)
    return dh, g, above, [_sum_halves(a, b, c_idx, BF16) for a, b in zip(parts, other)]


BIG = ("w_in", "w_proj_a", "w_proj_b", "w_out")
VECS = ("norm_g", "conv_a_b", "b_rg", "b_ig", "lru_lambda", "conv_b_b", "ln_b_g", "ln_b_b")


def _reduce_finish(landed):
    mine = [_sum_slots(y, "sum_chips") for y in landed]
    return mine, _run_rider(_swap_sibling_rider(mine), "swap_sibling")


def _forward_backward(h, target, layers, shards, final_g, c_idx):
    depth = len(layers)
    d = h.shape[1]
    gathered = _gather_chips(shards[0], "gather_weights")
    params, saved = [], []
    for l in range(depth):
        win, wpa, wpb, wo = gathered
        params.append({**layers[l], "w_in": win, "w_pa": wpa.reshape(d, d), "w_pb": wpb.reshape(d, d),
                       "w_out": wo.reshape(d, d)})
        h, s, gathered = _layer_fwd(h, params[l], shards[l + 1] if l + 1 < depth else None)
        saved.append(s)
    target = jnp.concatenate([jnp.zeros((N_META, d), F32), target], axis=0)
    loss_lanes, dh, g_final = _loss_head(h, final_g, target)
    grads, reduced, chip_sums = [None] * depth, [None] * depth, None
    for l in reversed(range(depth)):
        dh, grads[l], above, chip_sums = _layer_bwd(dh, params[l], saved[l], chip_sums, c_idx)
        if above is not None:
            reduced[l + 1] = above
    return loss_lanes, dh, g_final, grads, reduced, chip_sums


def kernel(x, meta, norm_g, w_in, conv_a_w, conv_a_b, w_rg, b_rg, w_ig, b_ig, lru_lambda, conv_b_w, conv_b_b, ln_b_g, ln_b_b, w_proj_a, w_proj_b, w_out, final_g, loss_target, m_meta, m_norm_g, m_w_in, m_conv_a_w, m_conv_a_b, m_w_rg, m_b_rg, m_w_ig, m_b_ig, m_lru_lambda, m_conv_b_w, m_conv_b_b, m_ln_b_g, m_ln_b_b, m_w_proj_a, m_w_proj_b, m_w_out, m_final_g, v_meta, v_norm_g, v_w_in, v_conv_a_w, v_conv_a_b, v_w_rg, v_b_rg, v_w_ig, v_b_ig, v_lru_lambda, v_conv_b_w, v_conv_b_b, v_ln_b_g, v_ln_b_b, v_w_proj_a, v_w_proj_b, v_w_out, v_final_g):
    names = ("meta", "norm_g", "w_in", "conv_a_w", "conv_a_b", "w_rg", "b_rg", "w_ig", "b_ig", "lru_lambda",
             "conv_b_w", "conv_b_b", "ln_b_g", "ln_b_b", "w_proj_a", "w_proj_b", "w_out", "final_g")
    w = dict(zip(names, (meta, norm_g, w_in, conv_a_w, conv_a_b, w_rg, b_rg, w_ig, b_ig, lru_lambda, conv_b_w,
                         conv_b_b, ln_b_g, ln_b_b, w_proj_a, w_proj_b, w_out, final_g)))
    m = dict(zip(names, (m_meta, m_norm_g, m_w_in, m_conv_a_w, m_conv_a_b, m_w_rg, m_b_rg, m_w_ig, m_b_ig,
                         m_lru_lambda, m_conv_b_w, m_conv_b_b, m_ln_b_g, m_ln_b_b, m_w_proj_a, m_w_proj_b, m_w_out,
                         m_final_g)))
    v = dict(zip(names, (v_meta, v_norm_g, v_w_in, v_conv_a_w, v_conv_a_b, v_w_rg, v_b_rg, v_w_ig, v_b_ig,
                         v_lru_lambda, v_conv_b_w, v_conv_b_b, v_ln_b_g, v_ln_b_b, v_w_proj_a, v_w_proj_b, v_w_out,
                         v_final_g)))
    depth, d = norm_g.shape
    dq = d // N_CHIPS
    px, py, pc = _place()
    chip = 2 * px + py
    c_idx = pc.astype(jnp.int32).reshape(1)

    meta_s, conv_a_s, conv_b_s = _gather_chips([meta, conv_a_w, conv_b_w], "gather_small")
    unshard = lambda s: jnp.concatenate([s[q] for q in range(N_CHIPS)], axis=-1)
    meta_f, conv_a_f, conv_b_f = unshard(meta_s), unshard(conv_a_s), unshard(conv_b_s)
    layers, shards = [], []
    for l in range(depth):
        shards.append([w_in[l].astype(BF16), w_proj_a[l].astype(BF16), w_proj_b[l].astype(BF16),
                       w_out[l].astype(BF16)])
        layers.append({
            "norm_g": norm_g[l][None], "conv_a_w": conv_a_f[l], "conv_a_b": conv_a_b[l][None],
            "w_rg": w_rg[l].astype(BF16), "b_rg": b_rg[l][None], "w_ig": w_ig[l].astype(BF16), "b_ig": b_ig[l][None],
            "lam": lru_lambda[l][None], "conv_b_w": conv_b_f[l], "conv_b_b": conv_b_b[l][None],
            "ln_b_g": ln_b_g[l][None], "ln_b_b": ln_b_b[l][None]})

    loss_lanes, dh, g_final, small, reduced, chip_sums = _forward_backward(
        jnp.concatenate([meta_f, x[0]], axis=0), loss_target[0], layers, shards, final_g[None], c_idx)
    loss = lax.psum(jnp.sum(loss_lanes), ("x", "y", "c"))
    grads = {}

    vec_rows = [small[l][k] for k in VECS for l in range(depth)]
    vec_rows += [small[l]["conv_a_w"] for l in range(depth)] + [small[l]["conv_b_w"] for l in range(depth)]
    vec_rows += [dh[:N_META], g_final]
    n_rows = sum(r.shape[0] for r in vec_rows)
    vec_rows.append(jnp.zeros((-n_rows % (16 * N_CHIPS), d), F32))
    packed = [jnp.concatenate(vec_rows, axis=0).reshape(N_CHIPS, -1, d),
              jnp.concatenate([small[l][k].reshape(-1, HEAD_DIM) for k in ("w_rg", "w_ig") for l in range(depth)],
                              axis=0).reshape(N_CHIPS, -1, HEAD_DIM)]
    other = _run_rider(_swap_halves_rider(packed), "swap_halves_small")
    small_sums = [_sum_halves(packed[0], other[0], c_idx, F32), _sum_halves(packed[1], other[1], c_idx, BF16)]
    mine, theirs = _reduce_finish(_run_rider(_scatter_rider(chip_sums + small_sums), "scatter_chips"))
    reduced[0] = (mine[:len(BIG)], theirs[:len(BIG)])
    joined = [jnp.concatenate([jnp.where(pc == 0, a, b), jnp.where(pc == 0, b, a)], axis=0)
              for a, b in zip(mine[len(BIG):], theirs[len(BIG):])]
    sum_v, sum_g = _gather_chips(joined, "gather_small_grads")
    sum_v, sum_g = sum_v.reshape(-1, d), sum_g.reshape(-1, HEAD_DIM)
    cols = lambda a: lax.dynamic_slice_in_dim(a, chip * dq, dq, axis=-1)
    for n_k, k in enumerate(VECS):
        grads[k] = sum_v[n_k * depth:(n_k + 1) * depth]
    o = len(VECS) * depth
    grads["conv_a_w"] = cols(sum_v[o:o + depth * CONV_A].reshape(depth, CONV_A, d))
    o += depth * CONV_A
    grads["conv_b_w"] = cols(sum_v[o:o + depth * CONV_B].reshape(depth, CONV_B, d))
    o += depth * CONV_B
    grads["meta"] = cols(sum_v[o:o + N_META])
    grads["final_g"] = sum_v[o + N_META]
    gate_rows = depth * (d // HEAD_DIM) * HEAD_DIM
    grads["w_rg"] = sum_g[:gate_rows].reshape(w_rg.shape)
    grads["w_ig"] = sum_g[gate_rows:].reshape(w_ig.shape)

    outs = {k: _adamw(w[k], grads[k], m[k], v[k], f"adamw_{k}") for k in names if k not in BIG}
    for n_k, k in enumerate(BIG):
        outs[k] = _adamw_layers(w[k], [reduced[l][0][n_k] for l in range(depth)],
                                [reduced[l][1][n_k] for l in range(depth)], m[k], v[k], c_idx, f"adamw_{k}")
    dx = dh[N_META:][None]
    return (loss, dx, *[outs[k][0] for k in names], *[outs[k][1] for k in names],
            *[outs[k][2] for k in names], *[outs[k][3] for k in names])
```

```python
import functools

import jax
import jax.numpy as jnp
from jax import lax
from jax.experimental import pallas as pl
from jax.experimental.pallas import tpu as pltpu

F32 = jnp.float32
BF16 = jnp.bfloat16
MESH = pl.DeviceIdType.MESH

EPS = 1e-6
N_META = 16
HEAD_DIM = 128
CONV_A = 4
CONV_B = 31
LRU_C = 8.0
N_SPLIT = 7
N_CHIPS = 4

ADAM_LR = 0.001
ADAM_B1 = 0.9
ADAM_B2 = 0.999
ADAM_EPS = 1e-08
ADAM_WD = 0.01
ADAM_STEP = 10

HALO_A = 16
HALO_B = 48
ROW_TILE_MM = 912
ROW_TILE_EW = 144
CONV_ROWS = 72
ROW_BLOCK = 16
LANE_CHUNK = 256
COL_TILE = 512
COL_TILE_WIDE = 1024
COL_TILE_IN = 1792
VMEM_LIMIT = 56 * 1024 * 1024
BLOCK_BYTES = 2 * 1024 * 1024

NT_DIMS = (((1,), (1,)), ((), ()))
TN_DIMS = (((0,), (0,)), ((), ()))


def _pick(n, cap, mult):
    best = None
    for d in range(mult, min(n, cap) + 1, mult):
        if n % d == 0:
            best = d
    if best is None:
        raise ValueError(f"no tile for {n} (cap {cap}, multiple of {mult})")
    return best


def _params(*semantics):
    return pltpu.CompilerParams(dimension_semantics=semantics, vmem_limit_bytes=VMEM_LIMIT)


def _whole(shape):
    return pl.BlockSpec(shape, lambda *_: (0,) * len(shape))


def _sigmoid(v):
    return 0.5 * jnp.tanh(0.5 * v) + 0.5


def _dsilu(v, s):
    return s * (1.0 + v * (1.0 - s))


def _log1p(e):
    w = 1.0 + e
    return jnp.where(w == 1.0, e, e * jnp.log(w) / (w - 1.0))


def _softplus(v):
    return jnp.maximum(v, 0.0) + _log1p(jnp.exp(-jnp.abs(v)))


def _neg_expm1_twice(v):
    t = jnp.tanh(v)
    return -2.0 * t / (1.0 - t)


def _row_iota(shape):
    return lax.broadcasted_iota(jnp.int32, shape, 0)


def _call(body, rider, operands, *, grid, in_specs, out_specs, out_shape, scratch_shapes, semantics, name):
    if rider is None:
        return pl.pallas_call(
            body, grid=grid, in_specs=in_specs, out_specs=out_specs, out_shape=out_shape,
            scratch_shapes=scratch_shapes, compiler_params=_params(*semantics), name=name)(*operands), None
    return _hosted(body, rider, grid=grid, in_specs=in_specs, out_specs=out_specs, out_shape=out_shape,
                   scratch_shapes=scratch_shapes, name=name)(*operands)


def _in_proj_fwd(h, g, w, rider=None):
    t, d = h.shape
    nq = w.shape[2]
    tm = _pick(t, ROW_TILE_MM, 16)
    tn = _pick(nq, COL_TILE_IN, 128)
    bps = nq // tn

    def body(h_ref, g_ref, w_ref, z_ref, hn_ref):
        @pl.when(pl.program_id(1) == 0)
        def _():
            def norm(r0):
                rows = _rows(r0)
                v = h_ref[rows, :]
                rinv = lax.rsqrt(jnp.mean(v * v, axis=-1, keepdims=True) + EPS)
                hn_ref[rows, :] = (v * rinv * g_ref[...]).astype(BF16)
            _row_blocks(tm, norm)
        z_ref[...] = jnp.dot(hn_ref[...], w_ref[...], preferred_element_type=F32)

    return _call(
        body, rider, (h, g, w), grid=(t // tm, N_CHIPS * bps),
        in_specs=[pl.BlockSpec((tm, d), lambda i, j: (i, 0), pipeline_mode=pl.Buffered(1)), _whole((1, d)),
                  pl.BlockSpec((None, d, tn), lambda i, j: (j // bps, 0, j % bps))],
        out_specs=[pl.BlockSpec((tm, tn), lambda i, j: (i, j)), pl.BlockSpec((tm, d), lambda i, j: (i, 0))],
        out_shape=[jax.ShapeDtypeStruct((t, N_CHIPS * nq), F32), jax.ShapeDtypeStruct((t, d), BF16)],
        scratch_shapes=[], semantics=("parallel", "arbitrary"), name="in_proj_fwd")


def _conv_a(xext, wa_ref, ba, tt):
    acc = ba
    for k in range(CONV_A):
        acc = acc + wa_ref[k:k + 1, :] * xext[pl.ds(HALO_A - (CONV_A - 1) + k, tt), :]
    return acc


def _gates_a(ca, wr_ref, wi_ref, br, bi, gr_s, gi_s):
    cab = ca.astype(BF16)
    for hh in range(ca.shape[1] // HEAD_DIM):
        sl = slice(hh * HEAD_DIM, (hh + 1) * HEAD_DIM)
        gr_s[:, sl] = jnp.dot(cab[:, sl], wr_ref[hh], preferred_element_type=F32)
        gi_s[:, sl] = jnp.dot(cab[:, sl], wi_ref[hh], preferred_element_type=F32)
    return _sigmoid(gr_s[...] + br), _sigmoid(gi_s[...] + bi)


def _decay_a(r, sp):
    log_a = -LRU_C * r * sp
    a = jnp.exp(log_a)
    mult = jnp.sqrt(jnp.maximum(_neg_expm1_twice(log_a), 0.0))
    return a, mult


def _scan_fwd(a_ref, u_ref, out_ref, h0, nblk):
    def blk(b, hprev):
        rows = pl.ds(pl.multiple_of(b * 8, 8), 8)
        ca, cb = a_ref[rows, :], u_ref[rows, :]
        row = _row_iota(ca.shape)
        for dist in (1, 2, 4):
            m = row >= dist
            cb = jnp.where(m, ca * pltpu.roll(cb, dist, 0) + cb, cb)
            ca = jnp.where(m, ca * pltpu.roll(ca, dist, 0), ca)
        hb = ca * hprev + cb
        out_ref[rows, :] = hb
        return hb[7:8, :]
    return lax.fori_loop(0, nblk, blk, h0)


def _scan_rev(a_ref, d_ref, out_ref, g0, nblk):
    def blk(k, g):
        rows = pl.ds(pl.multiple_of((nblk - 1 - k) * 8, 8), 8)
        a, cb = a_ref[rows, :], d_ref[rows, :]
        row = _row_iota(a.shape)
        ca = jnp.where(row == 7, 1.0, pltpu.roll(a, 7, 0))
        for dist in (1, 2, 4):
            m = row < 8 - dist
            cb = jnp.where(m, cb + ca * pltpu.roll(cb, 8 - dist, 0), cb)
            ca = jnp.where(m, ca * pltpu.roll(ca, 8 - dist, 0), ca)
        lam = cb + ca * g
        out_ref[rows, :] = lam
        return a[0:1, :] * lam[0:1, :]
    return lax.fori_loop(0, nblk, blk, g0)


def _branch_a_fwd(z, wa, ba, wr, br, wi, bi, lam):
    t = z.shape[0]
    d = wa.shape[1]
    tt = _pick(t, ROW_TILE_EW, HALO_B)
    hb = tt // HALO_A
    nh = d // HEAD_DIM

    def body(xa_ref, halo_ref, sa_ref, wa_ref, ba_ref, wr_ref, br_ref, wi_ref, bi_ref, lam_ref,
             ya_ref, hl_ref, xext, gr_s, gi_s, a_s, carry):
        i = pl.program_id(0)

        @pl.when(i == 0)
        def _():
            carry[...] = jnp.zeros_like(carry)
        xext[0:HALO_A, :] = jnp.where(i == 0, 0.0, halo_ref[...])
        xext[HALO_A:, :] = xa_ref[...]
        ca = _conv_a(xext, wa_ref, ba_ref[...], tt)
        r, ig = _gates_a(ca, wr_ref, wi_ref, br_ref[...], bi_ref[...], gr_s, gi_s)
        a, mult = _decay_a(r, _softplus(-lam_ref[...]))
        a_s[...] = a
        hl_ref[...] = mult * (ig * ca)
        carry[...] = _scan_fwd(a_s, hl_ref, hl_ref, carry[...], tt // 8)
        sa = sa_ref[...]
        ya_ref[...] = (hl_ref[...] * (sa * _sigmoid(sa))).astype(BF16)

    vec = _whole((1, d))
    return pl.pallas_call(
        body, grid=(t // tt,),
        in_specs=[pl.BlockSpec((tt, d), lambda i: (i, 0)),
                  pl.BlockSpec((HALO_A, d), lambda i: (jnp.maximum(i * hb - 1, 0), 0)),
                  pl.BlockSpec((tt, d), lambda i: (i, 1)),
                  _whole((CONV_A, d)), vec, _whole((nh, HEAD_DIM, HEAD_DIM)), vec,
                  _whole((nh, HEAD_DIM, HEAD_DIM)), vec, vec],
        out_specs=[pl.BlockSpec((tt, d), lambda i: (i, 0)), pl.BlockSpec((tt, d), lambda i: (i, 0))],
        out_shape=[jax.ShapeDtypeStruct((t, d), BF16), jax.ShapeDtypeStruct((t, d), F32)],
        scratch_shapes=[pltpu.VMEM((tt + HALO_A, d), F32), pltpu.VMEM((tt, d), F32), pltpu.VMEM((tt, d), F32),
                        pltpu.VMEM((tt, d), F32), pltpu.VMEM((1, d), F32)],
        compiler_params=_params("arbitrary"), name="branch_a_fwd")(z, z, z, wa, ba, wr, br, wi, bi, lam)


def _row_blocks(n_rows, body):
    n = n_rows // ROW_BLOCK
    unroll = 3 if n % 3 == 0 else 1

    def step(b, carry):
        for u in range(unroll):
            body(pl.multiple_of((b * unroll + u) * ROW_BLOCK, ROW_BLOCK))
        return carry
    lax.fori_loop(0, n // unroll, step, 0)


def _rows(r0, offset=0):
    return pl.ds(pl.multiple_of(r0 + offset, ROW_BLOCK), ROW_BLOCK)


def _glu_ext(gext, vb_ref, gb_ref, vbh_ref, gbh_ref, first, tt):
    def halo(r0):
        rows = _rows(r0)
        gext[rows, :] = jnp.where(first, 0.0, vbh_ref[rows, :] * _sigmoid(gbh_ref[rows, :]))
    _row_blocks(HALO_B, halo)

    def tile(r0):
        rows = _rows(r0)
        gext[_rows(r0, HALO_B), :] = vb_ref[rows, :] * _sigmoid(gb_ref[rows, :])
    _row_blocks(tt, tile)


def _shifted(src, dst, cs, n_rows):
    for b in range(1, 8):
        dst[b - 1, 0:n_rows, :] = src[pl.ds(b, n_rows), cs]


def _tap(src, shifted, cs, off, r0, n):
    b = off % 8
    if b == 0:
        return src[pl.ds(off + r0, n), cs]
    return shifted[b - 1, pl.ds(off - b + r0, n), :]


def _conv_rows(tt):
    return _pick(tt, CONV_ROWS, 8)


def _conv_b_fwd(gext, gs, wb_ref, bb_ref, cb_s, tt, d):
    rc = _conv_rows(tt)
    for c in range(d // LANE_CHUNK):
        cs = slice(c * LANE_CHUNK, (c + 1) * LANE_CHUNK)
        _shifted(gext, gs, cs, tt + HALO_B - 8)
        for r0 in range(0, tt, rc):
            acc = jnp.broadcast_to(bb_ref[:, cs], (rc, LANE_CHUNK))
            for k in range(CONV_B):
                acc = acc + wb_ref[k:k + 1, cs] * _tap(gext, gs, cs, HALO_B - (CONV_B - 1) + k, r0, rc)
            cb_s[r0:r0 + rc, cs] = acc


def _layer_norm_stats(cb):
    mu = jnp.mean(cb, axis=-1, keepdims=True)
    xc = cb - mu
    rstd = lax.rsqrt(jnp.mean(xc * xc, axis=-1, keepdims=True) + EPS)
    return xc * rstd, rstd


def _branch_b_fwd(z, wb, bb, lg, lb, rider=None):
    t = z.shape[0]
    d = wb.shape[1]
    tt = _pick(t, ROW_TILE_EW, HALO_B)
    hb = tt // HALO_B

    def body(vb_ref, gb_ref, vbh_ref, gbh_ref, sb_ref, wb_ref, bb_ref, lg_ref, lb_ref, yb_ref, cb_s, gext, gs):
        _glu_ext(gext, vb_ref, gb_ref, vbh_ref, gbh_ref, pl.program_id(0) == 0, tt)
        _conv_b_fwd(gext, gs, wb_ref, bb_ref, cb_s, tt, d)

        def finish(r0):
            rows = _rows(r0)
            xhat, _ = _layer_norm_stats(cb_s[rows, :])
            ln = xhat * lg_ref[...] + lb_ref[...]
            sb = sb_ref[rows, :]
            yb_ref[rows, :] = (ln * _sigmoid(ln) * (sb * _sigmoid(sb))).astype(BF16)
        _row_blocks(tt, finish)

    vec = _whole((1, d))
    tile = lambda s: pl.BlockSpec((tt, d), lambda i: (i, s))
    halo = lambda s: pl.BlockSpec((HALO_B, d), lambda i: (jnp.maximum(i * hb - 1, 0), s))
    return _call(
        body, rider, (z, z, z, z, z, wb, bb, lg, lb), grid=(t // tt,),
        in_specs=[tile(2), tile(3), halo(2), halo(3), tile(4), _whole((CONV_B, d)), vec, vec, vec],
        out_specs=[pl.BlockSpec((tt, d), lambda i: (i, 0))] * 2,
        out_shape=[jax.ShapeDtypeStruct((t, d), BF16), jax.ShapeDtypeStruct((t, d), F32)],
        scratch_shapes=[pltpu.VMEM((tt + HALO_B, d), F32), pltpu.VMEM((7, tt + HALO_B - 8, LANE_CHUNK), F32)],
        semantics=("parallel",), name="branch_b_fwd")


def _merge_fwd(ya, yb, z, wpa, wpb):
    t, d = ya.shape
    tm = _pick(t, ROW_TILE_MM, 16)
    tn = _pick(d, COL_TILE, 128)
    nb = d // tn

    def body(ya_ref, yb_ref, ma_ref, mb_ref, wpa_ref, wpb_ref, mg_ref, pa_ref, pb_ref):
        pa = jnp.dot(ya_ref[...], wpa_ref[...], preferred_element_type=F32)
        pb = jnp.dot(yb_ref[...], wpb_ref[...], preferred_element_type=F32)
        mg_ref[...] = (_sigmoid(ma_ref[...]) * pa + _sigmoid(mb_ref[...]) * pb).astype(BF16)
        pa_ref[...] = pa.astype(BF16)
        pb_ref[...] = pb.astype(BF16)

    rows = pl.BlockSpec((tm, d), lambda i, j: (i, 0))
    wcol = pl.BlockSpec((d, tn), lambda i, j: (0, j))
    outb = pl.BlockSpec((tm, tn), lambda i, j: (i, j))
    return pl.pallas_call(
        body, grid=(t // tm, nb),
        in_specs=[rows, rows, pl.BlockSpec((tm, tn), lambda i, j: (i, 5 * nb + j)),
                  pl.BlockSpec((tm, tn), lambda i, j: (i, 6 * nb + j)), wcol, wcol],
        out_specs=[outb, outb, outb],
        out_shape=[jax.ShapeDtypeStruct((t, d), BF16)] * 3,
        compiler_params=_params("parallel", "arbitrary"), name="merge_fwd")(ya, yb, z, z, wpa, wpb)


def _out_fwd(h, merged, wout):
    t, d = h.shape
    tm = _pick(t, ROW_TILE_MM, 16)
    tn = _pick(d, COL_TILE_WIDE, 128)

    def body(h_ref, mg_ref, w_ref, o_ref):
        o_ref[...] = h_ref[...] + jnp.dot(mg_ref[...], w_ref[...], preferred_element_type=F32)

    return pl.pallas_call(
        body, grid=(t // tm, d // tn),
        in_specs=[pl.BlockSpec((tm, tn), lambda i, j: (i, j)), pl.BlockSpec((tm, d), lambda i, j: (i, 0)),
                  pl.BlockSpec((d, tn), lambda i, j: (0, j))],
        out_specs=pl.BlockSpec((tm, tn), lambda i, j: (i, j)),
        out_shape=jax.ShapeDtypeStruct((t, d), F32),
        compiler_params=_params("parallel", "arbitrary"), name="out_fwd")(h, merged, wout)


def _loss_head(h, g, target):
    t, d = h.shape
    tt = _pick(t, ROW_TILE_EW, 16)

    def body(h_ref, g_ref, tg_ref, loss_ref, dh_ref, dg_ref):
        i = pl.program_id(0)

        @pl.when(i == 0)
        def _():
            loss_ref[...] = jnp.zeros_like(loss_ref)
            dg_ref[...] = jnp.zeros_like(dg_ref)
        v = h_ref[...]
        g = g_ref[...]
        rinv = lax.rsqrt(jnp.mean(v * v, axis=-1, keepdims=True) + EPS)
        xh = v * rinv
        valid = (i * tt + _row_iota(v.shape)) >= N_META
        diff = jnp.where(valid, xh * g - tg_ref[...], 0.0)
        loss_ref[...] += (0.5 / d) * jnp.sum(diff * diff, axis=0, keepdims=True)
        dy = diff * (1.0 / d)
        dg_ref[...] += jnp.sum(dy * xh, axis=0, keepdims=True)
        dxh = dy * g
        dh_ref[...] = rinv * (dxh - xh * jnp.mean(dxh * xh, axis=-1, keepdims=True))

    tile = pl.BlockSpec((tt, d), lambda i: (i, 0))
    vec = _whole((1, d))
    return pl.pallas_call(
        body, grid=(t // tt,), in_specs=[tile, vec, tile], out_specs=[vec, tile, vec],
        out_shape=[jax.ShapeDtypeStruct((1, d), F32), jax.ShapeDtypeStruct((t, d), F32),
                   jax.ShapeDtypeStruct((1, d), F32)],
        compiler_params=_params("arbitrary"), name="loss_head")(h, g, target)


def _dmerged_bwd(dho, wout, z, pa, pb):
    t, d = dho.shape
    tm = _pick(t, ROW_TILE_MM, 16)
    tn = _pick(d, COL_TILE, 128)
    nb = d // tn

    def body(dho_ref, w_ref, ma_ref, mb_ref, pa_ref, pb_ref, dpa_ref, dpb_ref, dma_ref, dmb_ref, dho_s):
        @pl.when(pl.program_id(1) == 0)
        def _():
            dho_s[...] = dho_ref[...].astype(BF16)
        dm = lax.dot_general(dho_s[...], w_ref[...], NT_DIMS, preferred_element_type=F32)
        sa = _sigmoid(ma_ref[...])
        sb = _sigmoid(mb_ref[...])
        dpa_ref[...] = (dm * sa).astype(BF16)
        dpb_ref[...] = (dm * sb).astype(BF16)
        dma_ref[...] = (dm * pa_ref[...].astype(F32) * (sa * (1.0 - sa))).astype(BF16)
        dmb_ref[...] = (dm * pb_ref[...].astype(F32) * (sb * (1.0 - sb))).astype(BF16)

    blk = pl.BlockSpec((tm, tn), lambda i, j: (i, j))
    return pl.pallas_call(
        body, grid=(t // tm, nb),
        in_specs=[pl.BlockSpec((tm, d), lambda i, j: (i, 0)), pl.BlockSpec((tn, d), lambda i, j: (j, 0)),
                  pl.BlockSpec((tm, tn), lambda i, j: (i, 5 * nb + j)),
                  pl.BlockSpec((tm, tn), lambda i, j: (i, 6 * nb + j)), blk, blk],
        out_specs=[blk, blk, blk, blk],
        out_shape=[jax.ShapeDtypeStruct((t, d), BF16)] * 4,
        scratch_shapes=[pltpu.VMEM((tm, d), BF16)],
        compiler_params=_params("parallel", "arbitrary"), name="dmerged_bwd")(dho, wout, z, z, pa, pb)


def _matmul_nt(a, w, name):
    t, k = a.shape
    n = w.shape[0]
    tm = _pick(t, ROW_TILE_MM, 16)
    tn = _pick(n, COL_TILE_WIDE, 128)

    def body(a_ref, w_ref, o_ref):
        o_ref[...] = lax.dot_general(a_ref[...], w_ref[...], NT_DIMS, preferred_element_type=F32)

    return pl.pallas_call(
        body, grid=(t // tm, n // tn),
        in_specs=[pl.BlockSpec((tm, k), lambda i, j: (i, 0)), pl.BlockSpec((tn, k), lambda i, j: (j, 0))],
        out_specs=pl.BlockSpec((tm, tn), lambda i, j: (i, j)),
        out_shape=jax.ShapeDtypeStruct((t, n), F32),
        compiler_params=_params("parallel", "arbitrary"), name=name)(a, w)


def _matmul_tn(a, b, name):
    t, m = a.shape
    n = b.shape[1]
    tk = _pick(t, ROW_TILE_MM, 16)
    tn = _pick(n, COL_TILE_WIDE, 128)

    def body(a_ref, b_ref, o_ref):
        @pl.when(pl.program_id(1) == 0)
        def _():
            o_ref[...] = jnp.zeros_like(o_ref)
        o_ref[...] += lax.dot_general(a_ref[...].astype(BF16), b_ref[...].astype(BF16), TN_DIMS,
                                      preferred_element_type=F32)

    return pl.pallas_call(
        body, grid=(n // tn, t // tk),
        in_specs=[pl.BlockSpec((tk, m), lambda j, s: (s, 0)), pl.BlockSpec((tk, tn), lambda j, s: (s, j))],
        out_specs=pl.BlockSpec((m, tn), lambda j, s: (0, j)),
        out_shape=jax.ShapeDtypeStruct((m, n), F32),
        compiler_params=_params("parallel", "arbitrary"), name=name)(a, b)


def _dwin_bwd(hn, dzs, rider=None):
    t, d = hn.shape
    nq = N_SPLIT * d // N_CHIPS
    tk = _pick(t, ROW_TILE_MM, 16)
    tn = _pick(nq, COL_TILE, 128)
    bps = nq // tn
    bpz = d // tn

    def body(hn_ref, *rest):
        dz_refs, o_ref = rest[:N_SPLIT], rest[N_SPLIT]
        j = pl.program_id(0)

        @pl.when(pl.program_id(1) == 0)
        def _():
            o_ref[...] = jnp.zeros_like(o_ref)
        for s in range(N_SPLIT):
            @pl.when(j // bpz == s)
            def _(s=s):
                o_ref[...] += lax.dot_general(hn_ref[...], dz_refs[s][...], TN_DIMS, preferred_element_type=F32)

    def dz_spec(s):
        def imap(j, r):
            on = (j // bpz) == s
            return (jnp.where(on, r, 0), jnp.where(on, j % bpz, 0))
        return pl.BlockSpec((tk, tn), imap)

    return _call(
        body, rider, (hn, *dzs), grid=(N_SPLIT * bpz, t // tk),
        in_specs=[pl.BlockSpec((tk, d), lambda j, r: (r, 0))] + [dz_spec(s) for s in range(N_SPLIT)],
        out_specs=[pl.BlockSpec((None, d, tn), lambda j, r: (j // bps, 0, j % bps))],
        out_shape=[jax.ShapeDtypeStruct((N_CHIPS, d, nq), F32)],
        scratch_shapes=[], semantics=("parallel", "arbitrary"), name="dwin_bwd")


def _din_bwd(dzs, w, h, g, dho, rider=None):
    t, d = h.shape
    nq = w.shape[2]
    tm = _pick(t, ROW_TILE_MM, 16)
    tk = _pick(nq, COL_TILE, 128)
    bps = nq // tk
    bpz = d // tk
    nk = N_SPLIT * bpz

    def body(*refs):
        dz_refs = refs[:N_SPLIT]
        w_ref, h_ref, g_ref, dho_ref, dh_ref, dg_ref, dg8 = refs[N_SPLIT:]
        i, k = pl.program_id(0), pl.program_id(1)

        @pl.when(k == 0)
        def _():
            dh_ref[...] = jnp.zeros_like(dh_ref)

        @pl.when((i == 0) & (k == 0))
        def _():
            dg8[...] = jnp.zeros_like(dg8)
        for s in range(N_SPLIT):
            @pl.when(k // bpz == s)
            def _(s=s):
                dh_ref[...] += lax.dot_general(dz_refs[s][...], w_ref[...], NT_DIMS, preferred_element_type=F32)

        @pl.when(k == nk - 1)
        def _():
            def norm_bwd(r0):
                rows = _rows(r0)
                v = h_ref[rows, :]
                rinv = lax.rsqrt(jnp.mean(v * v, axis=-1, keepdims=True) + EPS)
                xh = v * rinv
                dhn = dh_ref[rows, :]
                part = dhn * xh
                dg8[...] += part[0:8, :] + part[8:16, :]
                dxh = dhn * g_ref[...]
                dh_ref[rows, :] = dho_ref[rows, :] + rinv * (dxh - xh * jnp.mean(dxh * xh, axis=-1, keepdims=True))
            _row_blocks(tm, norm_bwd)
            dg_ref[...] = jnp.sum(dg8[...], axis=0, keepdims=True)

    def dz_spec(s):
        return pl.BlockSpec((tm, tk), lambda i, k: (i, jnp.clip(k - s * bpz, 0, bpz - 1)))

    rows = pl.BlockSpec((tm, d), lambda i, k: (i, 0))
    once = pl.BlockSpec((tm, d), lambda i, k: (i, 0), pipeline_mode=pl.Buffered(1))
    vec = _whole((1, d))
    return _call(
        body, rider, (*dzs, w, h, g, dho), grid=(t // tm, nk),
        in_specs=[dz_spec(s) for s in range(N_SPLIT)]
        + [pl.BlockSpec((None, d, tk), lambda i, k: (k // bps, 0, k % bps)), once, vec, once],
        out_specs=[rows, vec],
        out_shape=[jax.ShapeDtypeStruct((t, d), F32), jax.ShapeDtypeStruct((1, d), F32)],
        scratch_shapes=[pltpu.VMEM((8, d), F32)], semantics=("arbitrary", "arbitrary"), name="din_bwd")


def _branch_b_bwd(z, cb, dyb, wb, lg, lb):
    t = z.shape[0]
    d = wb.shape[1]
    tt = _pick(t, ROW_TILE_EW, HALO_B)
    hb = tt // HALO_B
    nt = t // tt
    rc = _conv_rows(tt)

    def body(vb_ref, gb_ref, vbh_ref, gbh_ref, sb_ref, cb_ref, dyb_ref, wb_ref, lg_ref, lb_ref,
             dvb_ref, dgb_ref, dsb_ref, dwb_ref, dbb_ref, dlg_ref, dlb_ref,
             gext, dext, gs, ds, cb_s, carry, dwb8, vec8):
        i = pl.program_id(0)

        @pl.when(i == 0)
        def _():
            for ref in (carry, dwb8, vec8):
                ref[...] = jnp.zeros_like(ref)
        _glu_ext(gext, vb_ref, gb_ref, vbh_ref, gbh_ref, i == nt - 1, tt)

        def norm_bwd(r0):
            rows = _rows(r0)
            xhat, rstd = _layer_norm_stats(cb_ref[rows, :])
            lg = lg_ref[...]
            ln = xhat * lg + lb_ref[...]
            sl = _sigmoid(ln)
            sb = sb_ref[rows, :]
            ss = _sigmoid(sb)
            dyb = dyb_ref[rows, :]
            dln = dyb * (sb * ss) * _dsilu(ln, sl)
            dsb_ref[rows, :] = (dyb * (ln * sl) * _dsilu(sb, ss)).astype(BF16)
            dxh = dln * lg
            dcb = rstd * (dxh - jnp.mean(dxh, axis=-1, keepdims=True)
                          - xhat * jnp.mean(dxh * xhat, axis=-1, keepdims=True))
            dext[rows, :] = dcb
            for n, val in enumerate((dln * xhat, dln, dcb)):
                vec8[n] += val[0:8, :] + val[8:16, :]
        _row_blocks(tt, norm_bwd)
        dext[tt:, :] = carry[...]
        carry[...] = dext[0:HALO_B, :]
        for c in range(d // LANE_CHUNK):
            cs = slice(c * LANE_CHUNK, (c + 1) * LANE_CHUNK)
            _shifted(gext, gs, cs, tt + HALO_B - 8)
            _shifted(dext, ds, cs, tt + HALO_B - 8)
            for r0 in range(0, tt, rc):
                dcb = dext[r0:r0 + rc, cs]
                acc = jnp.zeros((rc, LANE_CHUNK), F32)
                for k in range(CONV_B):
                    acc = acc + wb_ref[k:k + 1, cs] * _tap(dext, ds, cs, CONV_B - 1 - k, r0, rc)
                    prod = dcb * _tap(gext, gs, cs, HALO_B - (CONV_B - 1) + k, r0, rc)
                    part = prod[0:8, :]
                    for j in range(8, rc, 8):
                        part = part + prod[j:j + 8, :]
                    dwb8[k, :, cs] += part
                cb_s[r0:r0 + rc, cs] = acc

        def glu_bwd(r0):
            rows = _rows(r0)
            dglu = cb_s[rows, :]
            sg = _sigmoid(gb_ref[rows, :])
            dvb_ref[rows, :] = (dglu * sg).astype(BF16)
            dgb_ref[rows, :] = (dglu * vb_ref[rows, :] * (sg * (1.0 - sg))).astype(BF16)
        _row_blocks(tt, glu_bwd)

        @pl.when(i == nt - 1)
        def _():
            dwb_ref[...] = jnp.sum(dwb8[...], axis=1)
            dlg_ref[...] = jnp.sum(vec8[0], axis=0, keepdims=True)
            dlb_ref[...] = jnp.sum(vec8[1], axis=0, keepdims=True)
            dbb_ref[...] = jnp.sum(vec8[2], axis=0, keepdims=True)

    rev = lambda i: nt - 1 - i
    vec = _whole((1, d))
    tile = lambda s: pl.BlockSpec((tt, d), lambda i: (rev(i), s))
    halo = lambda s: pl.BlockSpec((HALO_B, d), lambda i: (jnp.maximum(rev(i) * hb - 1, 0), s))
    otile = pl.BlockSpec((tt, d), lambda i: (rev(i), 0))
    ext = pltpu.VMEM((tt + HALO_B, d), F32)
    shifts = pltpu.VMEM((7, tt + HALO_B - 8, LANE_CHUNK), F32)
    return pl.pallas_call(
        body, grid=(nt,),
        in_specs=[tile(2), tile(3), halo(2), halo(3), tile(4), otile, otile, _whole((CONV_B, d)), vec, vec],
        out_specs=[otile, otile, otile, _whole((CONV_B, d)), vec, vec, vec],
        out_shape=[jax.ShapeDtypeStruct((t, d), BF16)] * 3
        + [jax.ShapeDtypeStruct((CONV_B, d), F32)] + [jax.ShapeDtypeStruct((1, d), F32)] * 3,
        scratch_shapes=[ext, ext, shifts, shifts, pltpu.VMEM((tt, d), F32), pltpu.VMEM((HALO_B, d), F32),
                        pltpu.VMEM((CONV_B, 8, d), F32), pltpu.VMEM((3, 8, d), F32)],
        compiler_params=_params("arbitrary"), name="branch_b_bwd")(z, z, z, z, z, cb, dyb, wb, lg, lb)


def _branch_a_bwd(z, hl, dya, wa, ba, wr, br, wi, bi, lam):
    t = z.shape[0]
    d = wa.shape[1]
    tt = _pick(t, ROW_TILE_EW, HALO_B)
    hb = tt // HALO_A
    nt = t // tt
    nh = d // HEAD_DIM

    def body(xa_ref, xah_ref, sa_ref, hl_ref, hlh_ref, dya_ref, wa_ref, ba_ref, wr_ref, br_ref, wi_ref, bi_ref,
             lam_ref, dxa_ref, dsa_ref, dwa_ref, dba_ref, dbr_ref, dbi_ref, dlam_ref, dwr_ref, dwi_ref,
             xext, hext, dext, gr_s, gi_s, a_s, lam_s, gcarry, dcarry):
        i = pl.program_id(0)
        first = i == nt - 1

        @pl.when(i == 0)
        def _():
            for ref in (dwa_ref, dba_ref, dbr_ref, dbi_ref, dlam_ref, dwr_ref, dwi_ref, gcarry, dcarry):
                ref[...] = jnp.zeros_like(ref)
        xext[0:HALO_A, :] = jnp.where(first, 0.0, xah_ref[...])
        xext[HALO_A:, :] = xa_ref[...]
        hext[0:HALO_A, :] = jnp.where(first, 0.0, hlh_ref[...])
        hext[HALO_A:, :] = hl_ref[...]
        ca = _conv_a(xext, wa_ref, ba_ref[...], tt)
        r, ig = _gates_a(ca, wr_ref, wi_ref, br_ref[...], bi_ref[...], gr_s, gi_s)
        lam = lam_ref[...]
        sp = _softplus(-lam)
        a, mult = _decay_a(r, sp)
        sa = sa_ref[...]
        ss = _sigmoid(sa)
        dya = dya_ref[...]
        dsa_ref[...] = (dya * hl_ref[...] * _dsilu(sa, ss)).astype(BF16)
        a_s[...] = a
        lam_s[...] = dya * (sa * ss)
        gcarry[...] = _scan_rev(a_s, lam_s, lam_s, gcarry[...], tt // 8)
        du = lam_s[...]
        da = du * hext[pl.ds(HALO_A - 1, tt), :]
        dig = du * mult * ca
        dca = du * mult * ig
        dlog_a = da * a - (du * ig * ca) * jnp.where(mult > 0.0, a * a / mult, 0.0)
        dgr = (dlog_a * (-LRU_C * sp)) * r * (1.0 - r)
        dgi = dig * ig * (1.0 - ig)
        dlam_ref[...] += jnp.sum(dlog_a * r, axis=0, keepdims=True) * (LRU_C * _sigmoid(-lam))
        dbr_ref[...] += jnp.sum(dgr, axis=0, keepdims=True)
        dbi_ref[...] += jnp.sum(dgi, axis=0, keepdims=True)
        cab = ca.astype(BF16)
        dgrb = dgr.astype(BF16)
        dgib = dgi.astype(BF16)
        for hh in range(nh):
            sl = slice(hh * HEAD_DIM, (hh + 1) * HEAD_DIM)
            dwr_ref[hh] += lax.dot_general(cab[:, sl], dgrb[:, sl], TN_DIMS, preferred_element_type=F32)
            dwi_ref[hh] += lax.dot_general(cab[:, sl], dgib[:, sl], TN_DIMS, preferred_element_type=F32)
            gr_s[:, sl] = (lax.dot_general(dgrb[:, sl], wr_ref[hh], NT_DIMS, preferred_element_type=F32)
                           + lax.dot_general(dgib[:, sl], wi_ref[hh], NT_DIMS, preferred_element_type=F32))
        dca = dca + gr_s[...]
        dba_ref[...] += jnp.sum(dca, axis=0, keepdims=True)
        dext[0:tt, :] = dca
        dext[tt:, :] = dcarry[...]
        dcarry[...] = dca[0:HALO_A, :]
        dxa = jnp.zeros_like(dca)
        for k in range(CONV_A):
            off = HALO_A - (CONV_A - 1) + k
            dwa_ref[k:k + 1, :] += jnp.sum(dext[0:tt, :] * xext[pl.ds(off, tt), :], axis=0, keepdims=True)
            dxa = dxa + wa_ref[k:k + 1, :] * dext[pl.ds(CONV_A - 1 - k, tt), :]
        dxa_ref[...] = dxa.astype(BF16)

    rev = lambda i: nt - 1 - i
    vec = _whole((1, d))
    hw = _whole((nh, HEAD_DIM, HEAD_DIM))
    tile = lambda s: pl.BlockSpec((tt, d), lambda i: (rev(i), s))
    halo = pl.BlockSpec((HALO_A, d), lambda i: (jnp.maximum(rev(i) * hb - 1, 0), 0))
    big = pltpu.VMEM((tt + HALO_A, d), F32)
    full = pltpu.VMEM((tt, d), F32)
    return pl.pallas_call(
        body, grid=(nt,),
        in_specs=[tile(0), halo, tile(1), tile(0), halo, tile(0), _whole((CONV_A, d)), vec, hw, vec, hw, vec, vec],
        out_specs=[tile(0), tile(0), _whole((CONV_A, d)), vec, vec, vec, vec, hw, hw],
        out_shape=[jax.ShapeDtypeStruct((t, d), BF16)] * 2 + [jax.ShapeDtypeStruct((CONV_A, d), F32)]
        + [jax.ShapeDtypeStruct((1, d), F32)] * 4 + [jax.ShapeDtypeStruct((nh, HEAD_DIM, HEAD_DIM), F32)] * 2,
        scratch_shapes=[big, big, big, full, full, full, full, pltpu.VMEM((1, d), F32), pltpu.VMEM((HALO_A, d), F32)],
        compiler_params=_params("arbitrary"), name="branch_a_bwd")(z, z, z, hl, hl, dya, wa, ba, wr, br, wi, bi, lam)


def _rows_block(rows, cols, n_arrays=1):
    cap = max(8, BLOCK_BYTES // (4 * cols * n_arrays))
    return rows if rows <= cap else _pick(rows, cap, 8)


def _sum_halves(g, xo, c_idx, dtype):
    nq, r, c = g.shape
    rh = r // 2
    tr = _rows_block(rh, c)
    nb = rh // tr

    def body(c_ref, g_ref, x_ref, o_ref):
        o_ref[...] = (g_ref[...] + x_ref[...]).astype(dtype)

    grid_spec = pltpu.PrefetchScalarGridSpec(
        num_scalar_prefetch=1, grid=(nq, nb),
        in_specs=[pl.BlockSpec((None, tr, c), lambda q, i, cr: (q, cr[0] * nb + i, 0)),
                  pl.BlockSpec((None, tr, c), lambda q, i, cr: (q, i, 0))],
        out_specs=pl.BlockSpec((None, tr, c), lambda q, i, cr: (q, i, 0)))
    return pl.pallas_call(
        body, grid_spec=grid_spec, out_shape=jax.ShapeDtypeStruct((nq, rh, c), dtype),
        compiler_params=_params("parallel", "parallel"), name="sum_halves")(c_idx, g, xo)


def _sum_slots(y, name):
    ns, r, c = y.shape
    tr = _rows_block(r, c, ns)

    def body(y_ref, o_ref):
        acc = y_ref[0].astype(F32)
        for s in range(1, ns):
            acc = acc + y_ref[s].astype(F32)
        o_ref[...] = acc

    return pl.pallas_call(
        body, grid=(r // tr,), in_specs=[pl.BlockSpec((ns, tr, c), lambda i: (0, i, 0))],
        out_specs=pl.BlockSpec((tr, c), lambda i: (i, 0)), out_shape=jax.ShapeDtypeStruct((r, c), F32),
        compiler_params=_params("parallel"), name=name)(y)


def _adamw(w, g, m, v, name):
    shape = w.shape
    c = shape[-1]
    r = w.size // c
    tr = _rows_block(r, c)
    c1 = 1.0 / (1.0 - ADAM_B1 ** ADAM_STEP)
    c2 = 1.0 / (1.0 - ADAM_B2 ** ADAM_STEP)

    def body(w_ref, g_ref, m_ref, v_ref, go_ref, d_ref, mo_ref, vo_ref):
        gv = g_ref[...]
        mn = ADAM_B1 * m_ref[...] + (1.0 - ADAM_B1) * gv
        vn = ADAM_B2 * v_ref[...] + (1.0 - ADAM_B2) * (gv * gv)
        go_ref[...] = gv
        mo_ref[...] = mn
        vo_ref[...] = vn
        d_ref[...] = -ADAM_LR * ((mn * c1) / (jnp.sqrt(vn * c2) + ADAM_EPS) + ADAM_WD * w_ref[...])

    blk = pl.BlockSpec((tr, c), lambda i: (i, 0))
    outs = pl.pallas_call(
        body, grid=(r // tr,), in_specs=[blk] * 4, out_specs=[blk] * 4,
        out_shape=[jax.ShapeDtypeStruct((r, c), F32)] * 4,
        compiler_params=_params("parallel"), name=name)(*(a.reshape(r, c) for a in (w, g, m, v)))
    return tuple(o.reshape(shape) for o in outs)


def _adamw_layers(w, mine, theirs, m, v, c_idx, name):
    depth, r, c = w.shape
    tr = _rows_block(r // 2, c, 2)
    nbh = r // 2 // tr
    c1 = 1.0 / (1.0 - ADAM_B1 ** ADAM_STEP)
    c2 = 1.0 / (1.0 - ADAM_B2 ** ADAM_STEP)

    def body(c_ref, w_ref, m_ref, v_ref, *rest):
        g_refs = rest[:2 * depth]
        go_ref, d_ref, mo_ref, vo_ref = rest[2 * depth:]
        layer = pl.program_id(0)
        own = (pl.program_id(1) // nbh) == c_ref[0]

        def update(g_ref):
            gv = g_ref[...]
            mn = ADAM_B1 * m_ref[...] + (1.0 - ADAM_B1) * gv
            vn = ADAM_B2 * v_ref[...] + (1.0 - ADAM_B2) * (gv * gv)
            go_ref[...] = gv
            mo_ref[...] = mn
            vo_ref[...] = vn
            d_ref[...] = -ADAM_LR * ((mn * c1) / (jnp.sqrt(vn * c2) + ADAM_EPS) + ADAM_WD * w_ref[...])

        for s in range(depth):
            @pl.when((layer == s) & own)
            def _(s=s):
                update(g_refs[2 * s])

            @pl.when((layer == s) & jnp.logical_not(own))
            def _(s=s):
                update(g_refs[2 * s + 1])

    stacked = pl.BlockSpec((None, tr, c), lambda l, i, cr: (l, i, 0))

    def g_spec(s, is_mine):
        def imap(l, i, cr):
            own = (i // nbh) == cr[0]
            on = (l == s) & (own if is_mine else jnp.logical_not(own))
            return (jnp.where(on, i % nbh, 0), 0)
        return pl.BlockSpec((tr, c), imap)

    g_specs = [g_spec(s, is_mine) for s in range(depth) for is_mine in (True, False)]
    gs = [a for s in range(depth) for a in (mine[s], theirs[s])]
    grid_spec = pltpu.PrefetchScalarGridSpec(
        num_scalar_prefetch=1, grid=(depth, 2 * nbh), in_specs=[stacked] * 3 + g_specs, out_specs=[stacked] * 4)
    return pl.pallas_call(
        body, grid_spec=grid_spec, out_shape=[jax.ShapeDtypeStruct(w.shape, F32)] * 4,
        compiler_params=_params("arbitrary", "arbitrary"), name=name)(c_idx, w, m, v, *gs)


HBM_SPEC = pl.BlockSpec(memory_space=pltpu.HBM)


def _place():
    return lax.axis_index("x"), lax.axis_index("y"), lax.axis_index("c")


def _other_chips(x, y):
    return [(1 - x, y), (x, 1 - y), (1 - x, 1 - y)]


def _half(ref, which):
    rows = ref.shape[0] // 2
    return ref.at[pl.ds(which * rows, rows)]


def _gather_chips(arrays, name):
    partial = _run_rider(_gather_ici_rider(arrays), name + "_ici")
    return _run_rider(_gather_d2d_rider(partial), name + "_d2d")


class _Rider:
    def __init__(self, operands, out_shape, scratch, start, finish, aliases=None):
        self.operands, self.out_shape, self.scratch = list(operands), list(out_shape), list(scratch)
        self.start, self.finish, self.aliases = start, finish, dict(aliases or {})


def _run_rider(rider, name):
    n_in, n_out = len(rider.operands), len(rider.out_shape)

    def body(*refs):
        ins, outs, sems = refs[:n_in], refs[n_in:n_in + n_out], refs[n_in + n_out:]
        rider.start(ins, outs, sems)
        rider.finish(ins, outs, sems)

    return pl.pallas_call(
        body, in_specs=[HBM_SPEC] * n_in, out_specs=[HBM_SPEC] * n_out, out_shape=rider.out_shape,
        scratch_shapes=rider.scratch, input_output_aliases=rider.aliases, name=name)(*rider.operands)


def _hosted(body, rider, *, grid, in_specs, out_specs, out_shape, scratch_shapes, name):
    n_in, n_out, n_scr = len(in_specs), len(out_specs), len(scratch_shapes)
    r_in, r_out = len(rider.operands), len(rider.out_shape)

    def wrapped(*refs):
        ins, refs = refs[:n_in], refs[n_in:]
        r_ins, refs = refs[:r_in], refs[r_in:]
        outs, refs = refs[:n_out], refs[n_out:]
        r_outs, refs = refs[:r_out], refs[r_out:]
        scr, sems = refs[:n_scr], refs[n_scr:]
        first = functools.reduce(jnp.logical_and, [pl.program_id(a) == 0 for a in range(len(grid))])
        last = functools.reduce(jnp.logical_and, [pl.program_id(a) == g - 1 for a, g in enumerate(grid)])

        @pl.when(first)
        def _():
            rider.start(r_ins, r_outs, sems)
        body(*ins, *outs, *scr)

        @pl.when(last)
        def _():
            rider.finish(r_ins, r_outs, sems)

    res = pl.pallas_call(
        wrapped, grid=grid, in_specs=list(in_specs) + [HBM_SPEC] * r_in, out_specs=list(out_specs) + [HBM_SPEC] * r_out,
        out_shape=list(out_shape) + rider.out_shape, scratch_shapes=list(scratch_shapes) + rider.scratch,
        input_output_aliases={n_in + i: n_out + o for i, o in rider.aliases.items()},
        compiler_params=_params(*(["arbitrary"] * len(grid))), name=name)
    return lambda *operands: (lambda r: (r[:n_out], r[n_out:]))(res(*operands, *rider.operands))


def _gather_ici_rider(arrays):
    n = len(arrays)

    def copies(ins, outs, sems):
        send, recv = sems
        x, y, c = _place()
        me = 2 * x + y
        chips = _other_chips(x, y)
        sends, arrivals = [], []
        for a in range(n):
            for group in (sends, arrivals):
                group.append(pltpu.make_async_remote_copy(
                    src_ref=ins[a], dst_ref=outs[a].at[me], send_sem=send.at[a, 3], recv_sem=recv.at[a, 3],
                    device_id=(x, y, 1 - c), device_id_type=MESH))
            for j in range(3):
                q = 2 * chips[j][0] + chips[j][1]
                for dst, group in ((_half(outs[a].at[me], c), sends), (_half(outs[a].at[q], c), arrivals)):
                    group.append(pltpu.make_async_remote_copy(
                        src_ref=_half(ins[a], c), dst_ref=dst, send_sem=send.at[a, j], recv_sem=recv.at[a, j],
                        device_id=(*chips[j], c), device_id_type=MESH))
        return sends, arrivals

    def start(ins, outs, sems):
        for cp in copies(ins, outs, sems)[0]:
            cp.start()

    def finish(ins, outs, sems):
        sends, arrivals = copies(ins, outs, sems)
        for cp in arrivals:
            cp.wait_recv()
        for cp in sends:
            cp.wait_send()

    sems = pltpu.SemaphoreType.DMA((n, 4))
    return _Rider(arrays, [jax.ShapeDtypeStruct((N_CHIPS,) + a.shape, a.dtype) for a in arrays],
                  [sems, sems], start, finish)


def _gather_d2d_rider(partial):
    n = len(partial)

    def copies(outs, sems):
        send, recv = sems
        x, y, c = _place()
        chips = _other_chips(x, y)
        sends, arrivals = [], []
        for a in range(n):
            for j in range(3):
                q = 2 * chips[j][0] + chips[j][1]
                for which, group in ((c, sends), (1 - c, arrivals)):
                    ref = _half(outs[a].at[q], which)
                    group.append(pltpu.make_async_remote_copy(
                        src_ref=ref, dst_ref=ref, send_sem=send.at[a, j], recv_sem=recv.at[a, j],
                        device_id=(x, y, 1 - c), device_id_type=MESH))
        return sends, arrivals

    def start(ins, outs, sems):
        for cp in copies(outs, sems)[0]:
            cp.start()

    def finish(ins, outs, sems):
        sends, arrivals = copies(outs, sems)
        for cp in arrivals:
            cp.wait_recv()
        for cp in sends:
            cp.wait_send()

    sems = pltpu.SemaphoreType.DMA((n, 3))
    return _Rider(partial, [jax.ShapeDtypeStruct(a.shape, a.dtype) for a in partial], [sems, sems], start, finish,
                  aliases={k: k for k in range(n)})


def _swap_halves_rider(grads):
    n = len(grads)

    def copies(ins, outs, sems):
        send, recv = sems
        x, y, c = _place()
        made = []
        for a in range(n):
            rows = ins[a].shape[1] // 2
            made.append(pltpu.make_async_remote_copy(
                src_ref=ins[a].at[:, pl.ds((1 - c) * rows, rows)], dst_ref=outs[a],
                send_sem=send.at[a], recv_sem=recv.at[a], device_id=(x, y, 1 - c), device_id_type=MESH))
        return made

    def start(ins, outs, sems):
        for cp in copies(ins, outs, sems):
            cp.start()

    def finish(ins, outs, sems):
        for cp in copies(ins, outs, sems):
            cp.wait()

    sems = pltpu.SemaphoreType.DMA((n,))
    return _Rider(grads, [jax.ShapeDtypeStruct((g.shape[0], g.shape[1] // 2) + g.shape[2:], g.dtype) for g in grads],
                  [sems, sems], start, finish)


def _scatter_rider(parts):
    n = len(parts)

    def copies(ins, outs, sems):
        send, recv, lsem = sems
        x, y, c = _place()
        me = 2 * x + y
        chips = _other_chips(x, y)
        local = [pltpu.make_async_copy(ins[a].at[me], outs[a].at[me], lsem.at[a]) for a in range(n)]
        sends, arrivals = [], []
        for a in range(n):
            for j in range(3):
                q = 2 * chips[j][0] + chips[j][1]
                for dst, group in ((outs[a].at[me], sends), (outs[a].at[q], arrivals)):
                    group.append(pltpu.make_async_remote_copy(
                        src_ref=ins[a].at[q], dst_ref=dst, send_sem=send.at[a, j], recv_sem=recv.at[a, j],
                        device_id=(*chips[j], c), device_id_type=MESH))
        return local, sends, arrivals

    def start(ins, outs, sems):
        local, sends, _ = copies(ins, outs, sems)
        for cp in local + sends:
            cp.start()

    def finish(ins, outs, sems):
        local, sends, arrivals = copies(ins, outs, sems)
        for cp in arrivals:
            cp.wait_recv()
        for cp in sends:
            cp.wait_send()
        for cp in local:
            cp.wait()

    sems = pltpu.SemaphoreType.DMA((n, 3))
    return _Rider(parts, [jax.ShapeDtypeStruct(p.shape, p.dtype) for p in parts],
                  [sems, sems, pltpu.SemaphoreType.DMA((n,))], start, finish)


def _swap_sibling_rider(arrays):
    n = len(arrays)

    def copies(ins, outs, sems):
        send, recv = sems
        x, y, c = _place()
        return [pltpu.make_async_remote_copy(
            src_ref=ins[a], dst_ref=outs[a], send_sem=send.at[a], recv_sem=recv.at[a],
            device_id=(x, y, 1 - c), device_id_type=MESH) for a in range(n)]

    def start(ins, outs, sems):
        for cp in copies(ins, outs, sems):
            cp.start()

    def finish(ins, outs, sems):
        for cp in copies(ins, outs, sems):
            cp.wait()

    sems = pltpu.SemaphoreType.DMA((n,))
    return _Rider(arrays, [jax.ShapeDtypeStruct(a.shape, a.dtype) for a in arrays], [sems, sems], start, finish)


def _join_riders(first, second):
    n_in, n_out, n_sem = len(first.operands), len(first.out_shape), len(first.scratch)

    def both(method):
        def run(ins, outs, sems):
            getattr(first, method)(ins[:n_in], outs[:n_out], sems[:n_sem])
            getattr(second, method)(ins[n_in:], outs[n_out:], sems[n_sem:])
        return run

    aliases = {**first.aliases, **{n_in + i: n_out + o for i, o in second.aliases.items()}}
    return _Rider(first.operands + second.operands, first.out_shape + second.out_shape,
                  first.scratch + second.scratch, both("start"), both("finish"), aliases)


def _layer_fwd(h, p, w_in, proj, ride_shards):
    d = h.shape[1]
    ride = _gather_ici_rider(ride_shards) if ride_shards else None
    (z, hn), partial = _in_proj_fwd(h, p["norm_g"], w_in, ride)
    ya, hl = _branch_a_fwd(z, p["conv_a_w"], p["conv_a_b"], p["w_rg"], p["b_rg"], p["w_ig"], p["b_ig"], p["lam"])
    ride = None if partial is None else _gather_d2d_rider(partial)
    (yb, cb), gathered = _branch_b_fwd(z, p["conv_b_w"], p["conv_b_b"], p["ln_b_g"], p["ln_b_b"], ride)
    if proj is None:
        proj, gathered = gathered[:3], gathered[3:]
    big = dict(zip(("w_pa", "w_pb", "w_out"), (w.reshape(d, d) for w in proj)), w_in=w_in)
    merged, pa, pb = _merge_fwd(ya, yb, z, big["w_pa"], big["w_pb"])
    return _out_fwd(h, merged, big["w_out"]), (h, hn, z, hl, cb, ya, yb, pa, pb, merged), big, gathered


def _layer_bwd(dho, p, saved, chip_sums, c_idx):
    h, hn, z, hl, cb, ya, yb, pa, pb, merged = saved
    dpa, dpb, dma, dmb = _dmerged_bwd(dho, p["w_out"], z, pa, pb)
    g = {"w_out": _matmul_tn(merged, dho, "dw_out"), "w_proj_a": _matmul_tn(ya, dpa, "dw_proj_a"),
         "w_proj_b": _matmul_tn(yb, dpb, "dw_proj_b")}
    dya = _matmul_nt(dpa, p["w_pa"], "dya")
    dyb = _matmul_nt(dpb, p["w_pb"], "dyb")
    (dxa, dsa, g["conv_a_w"], g["conv_a_b"], g["b_rg"], g["b_ig"], g["lru_lambda"], g["w_rg"], g["w_ig"]) = \
        _branch_a_bwd(z, hl, dya, p["conv_a_w"], p["conv_a_b"], p["w_rg"], p["b_rg"], p["w_ig"], p["b_ig"], p["lam"])
    dvb, dgb, dsb, g["conv_b_w"], g["conv_b_b"], g["ln_b_g"], g["ln_b_b"] = \
        _branch_b_bwd(z, cb, dyb, p["conv_b_w"], p["ln_b_g"], p["ln_b_b"])
    dzs = (dxa, dsa, dvb, dgb, dsb, dma, dmb)
    (g["w_in"],), landed = _dwin_bwd(hn, dzs, None if chip_sums is None else _scatter_rider(chip_sums))
    d = h.shape[1]
    parts = [g["w_in"]] + [g[k].reshape(N_CHIPS, d // N_CHIPS, d) for k in BIG[1:]]
    rider = _swap_halves_rider(parts)
    mine = None if landed is None else [_sum_slots(y, "sum_chips") for y in landed]
    if mine is not None:
        rider = _join_riders(rider, _swap_sibling_rider(mine))
    (dh, g["norm_g"]), swapped = _din_bwd(dzs, p["w_in"], h, p["norm_g"], dho, rider)
    other, theirs = swapped[:len(parts)], swapped[len(parts):]
    above = None if mine is None else (mine, theirs)
    return dh, g, above, [_sum_halves(a, b, c_idx, BF16) for a, b in zip(parts, other)]


BIG = ("w_in", "w_proj_a", "w_proj_b", "w_out")
VECS = ("norm_g", "conv_a_b", "b_rg", "b_ig", "lru_lambda", "conv_b_b", "ln_b_g", "ln_b_b")


def _reduce_finish(landed):
    mine = [_sum_slots(y, "sum_chips") for y in landed]
    return mine, _run_rider(_swap_sibling_rider(mine), "swap_sibling")


def _forward_backward(h, target, layers, shards, final_g, c_idx):
    depth = len(layers)
    d = h.shape[1]
    w_in, = _gather_chips(shards[0][:1], "gather_weights")
    proj, params, saved = None, [], []
    for l in range(depth):
        ride_shards = (shards[l][1:] if proj is None else []) + (shards[l + 1] if l + 1 < depth else [])
        h, s, big, gathered = _layer_fwd(h, layers[l], w_in, proj, ride_shards)
        params.append({**layers[l], **big})
        saved.append(s)
        if l + 1 < depth:
            w_in, proj = gathered[0], gathered[1:]
    target = jnp.concatenate([jnp.zeros((N_META, d), F32), target], axis=0)
    loss_lanes, dh, g_final = _loss_head(h, final_g, target)
    grads, reduced, chip_sums = [None] * depth, [None] * depth, None
    for l in reversed(range(depth)):
        dh, grads[l], above, chip_sums = _layer_bwd(dh, params[l], saved[l], chip_sums, c_idx)
        if above is not None:
            reduced[l + 1] = above
    return loss_lanes, dh, g_final, grads, reduced, chip_sums


def kernel(x, meta, norm_g, w_in, conv_a_w, conv_a_b, w_rg, b_rg, w_ig, b_ig, lru_lambda, conv_b_w, conv_b_b, ln_b_g, ln_b_b, w_proj_a, w_proj_b, w_out, final_g, loss_target, m_meta, m_norm_g, m_w_in, m_conv_a_w, m_conv_a_b, m_w_rg, m_b_rg, m_w_ig, m_b_ig, m_lru_lambda, m_conv_b_w, m_conv_b_b, m_ln_b_g, m_ln_b_b, m_w_proj_a, m_w_proj_b, m_w_out, m_final_g, v_meta, v_norm_g, v_w_in, v_conv_a_w, v_conv_a_b, v_w_rg, v_b_rg, v_w_ig, v_b_ig, v_lru_lambda, v_conv_b_w, v_conv_b_b, v_ln_b_g, v_ln_b_b, v_w_proj_a, v_w_proj_b, v_w_out, v_final_g):
    names = ("meta", "norm_g", "w_in", "conv_a_w", "conv_a_b", "w_rg", "b_rg", "w_ig", "b_ig", "lru_lambda",
             "conv_b_w", "conv_b_b", "ln_b_g", "ln_b_b", "w_proj_a", "w_proj_b", "w_out", "final_g")
    w = dict(zip(names, (meta, norm_g, w_in, conv_a_w, conv_a_b, w_rg, b_rg, w_ig, b_ig, lru_lambda, conv_b_w,
                         conv_b_b, ln_b_g, ln_b_b, w_proj_a, w_proj_b, w_out, final_g)))
    m = dict(zip(names, (m_meta, m_norm_g, m_w_in, m_conv_a_w, m_conv_a_b, m_w_rg, m_b_rg, m_w_ig, m_b_ig,
                         m_lru_lambda, m_conv_b_w, m_conv_b_b, m_ln_b_g, m_ln_b_b, m_w_proj_a, m_w_proj_b, m_w_out,
                         m_final_g)))
    v = dict(zip(names, (v_meta, v_norm_g, v_w_in, v_conv_a_w, v_conv_a_b, v_w_rg, v_b_rg, v_w_ig, v_b_ig,
                         v_lru_lambda, v_conv_b_w, v_conv_b_b, v_ln_b_g, v_ln_b_b, v_w_proj_a, v_w_proj_b, v_w_out,
                         v_final_g)))
    depth, d = norm_g.shape
    dq = d // N_CHIPS
    px, py, pc = _place()
    chip = 2 * px + py
    c_idx = pc.astype(jnp.int32).reshape(1)

    meta_s, conv_a_s, conv_b_s = _gather_chips([meta, conv_a_w, conv_b_w], "gather_small")
    unshard = lambda s: jnp.concatenate([s[q] for q in range(N_CHIPS)], axis=-1)
    meta_f, conv_a_f, conv_b_f = unshard(meta_s), unshard(conv_a_s), unshard(conv_b_s)
    layers, shards = [], []
    for l in range(depth):
        shards.append([w_in[l].astype(BF16), w_proj_a[l].astype(BF16), w_proj_b[l].astype(BF16),
                       w_out[l].astype(BF16)])
        layers.append({
            "norm_g": norm_g[l][None], "conv_a_w": conv_a_f[l], "conv_a_b": conv_a_b[l][None],
            "w_rg": w_rg[l].astype(BF16), "b_rg": b_rg[l][None], "w_ig": w_ig[l].astype(BF16), "b_ig": b_ig[l][None],
            "lam": lru_lambda[l][None], "conv_b_w": conv_b_f[l], "conv_b_b": conv_b_b[l][None],
            "ln_b_g": ln_b_g[l][None], "ln_b_b": ln_b_b[l][None]})

    loss_lanes, dh, g_final, small, reduced, chip_sums = _forward_backward(
        jnp.concatenate([meta_f, x[0]], axis=0), loss_target[0], layers, shards, final_g[None], c_idx)
    loss = lax.psum(jnp.sum(loss_lanes), ("x", "y", "c"))
    grads = {}

    vec_rows = [small[l][k] for k in VECS for l in range(depth)]
    vec_rows += [small[l]["conv_a_w"] for l in range(depth)] + [small[l]["conv_b_w"] for l in range(depth)]
    vec_rows += [dh[:N_META], g_final]
    n_rows = sum(r.shape[0] for r in vec_rows)
    vec_rows.append(jnp.zeros((-n_rows % (16 * N_CHIPS), d), F32))
    packed = [jnp.concatenate(vec_rows, axis=0).reshape(N_CHIPS, -1, d),
              jnp.concatenate([small[l][k].reshape(-1, HEAD_DIM) for k in ("w_rg", "w_ig") for l in range(depth)],
                              axis=0).reshape(N_CHIPS, -1, HEAD_DIM)]
    other = _run_rider(_swap_halves_rider(packed), "swap_halves_small")
    small_sums = [_sum_halves(packed[0], other[0], c_idx, F32), _sum_halves(packed[1], other[1], c_idx, BF16)]
    mine, theirs = _reduce_finish(_run_rider(_scatter_rider(chip_sums + small_sums), "scatter_chips"))
    reduced[0] = (mine[:len(BIG)], theirs[:len(BIG)])
    joined = [jnp.concatenate([jnp.where(pc == 0, a, b), jnp.where(pc == 0, b, a)], axis=0)
              for a, b in zip(mine[len(BIG):], theirs[len(BIG):])]
    sum_v, sum_g = _gather_chips(joined, "gather_small_grads")
    sum_v, sum_g = sum_v.reshape(-1, d), sum_g.reshape(-1, HEAD_DIM)
    cols = lambda a: lax.dynamic_slice_in_dim(a, chip * dq, dq, axis=-1)
    for n_k, k in enumerate(VECS):
        grads[k] = sum_v[n_k * depth:(n_k + 1) * depth]
    o = len(VECS) * depth
    grads["conv_a_w"] = cols(sum_v[o:o + depth * CONV_A].reshape(depth, CONV_A, d))
    o += depth * CONV_A
    grads["conv_b_w"] = cols(sum_v[o:o + depth * CONV_B].reshape(depth, CONV_B, d))
    o += depth * CONV_B
    grads["meta"] = cols(sum_v[o:o + N_META])
    grads["final_g"] = sum_v[o + N_META]
    gate_rows = depth * (d // HEAD_DIM) * HEAD_DIM
    grads["w_rg"] = sum_g[:gate_rows].reshape(w_rg.shape)
    grads["w_ig"] = sum_g[gate_rows:].reshape(w_ig.shape)

    outs = {k: _adamw(w[k], grads[k], m[k], v[k], f"adamw_{k}") for k in names if k not in BIG}
    for n_k, k in enumerate(BIG):
        outs[k] = _adamw_layers(w[k], [reduced[l][0][n_k] for l in range(depth)],
                                [reduced[l][1][n_k] for l in range(depth)], m[k], v[k], c_idx, f"adamw_{k}")
    dx = dh[N_META:][None]
    return (loss, dx, *[outs[k][0] for k in names], *[outs[k][1] for k in names],
            *[outs[k][2] for k in names], *[outs[k][3] for k in names])
```
